```python
import math
import jax, jax.numpy as jnp
from jax import lax
import numpy as np

D_MODEL = 2048
BATCH = 4
SEQ = 2048
DEPTH = 4
DEC_BATCH = 128
DEC_SEQ = 4
PAST_LEN = 16384
PAGE_SIZE = 128

PLE_DIM = 256
CONV_W = 4
EPS = 1e-6
N_EVEN = (DEPTH + 1) // 2
N_ODD = DEPTH // 2
DN_HEADS = 8
DN_DK = 128
DN_DV = 128
DN_W = DN_HEADS * DN_DV
DN_QKV = 2 * DN_HEADS * DN_DK + DN_W
DN_CHUNK = 64
GLA_HEADS = 4
GLA_DK = 128
GLA_DV = 256
GLA_KW = GLA_HEADS * GLA_DK
GLA_W = GLA_HEADS * GLA_DV
GLA_RANK = 16
GLA_GATE_NORM = 16.0
GLA_CHUNK = 16
AB_SIZES = (DN_QKV, DN_HEADS, DN_HEADS, DN_W, GLA_KW, GLA_KW, GLA_W, GLA_RANK, GLA_W)
AB_IN = sum(AB_SIZES)
AB_OUT = DN_W + GLA_W
SSD_HEADS = 16
SSD_HEADDIM = 64
SSD_W = SSD_HEADS * SSD_HEADDIM
SSD_GROUPS = 4
SSD_STATE = 64
SSD_XBC = SSD_W + 2 * SSD_GROUPS * SSD_STATE
SSD_CHUNK = 64
LRU_W = 1024
LRU_BLOCKS = 16
LRU_BS = LRU_W // LRU_BLOCKS
LRU_C = 8.0
CD_SIZES = (SSD_W, SSD_XBC, SSD_HEADS, LRU_W, LRU_W)
CD_IN = sum(CD_SIZES)
CD_OUT = SSD_W + LRU_W

kernel_name = 'hybrid_deltanet_gla_ssd_rglru_step'


def _rmsnorm(x, g):
    xf = x.astype(jnp.float32)
    y = xf * lax.rsqrt(jnp.mean(xf * xf, axis=-1, keepdims=True) + EPS)
    return (y * g.astype(jnp.float32)).astype(x.dtype)


def _l2norm(x):
    xf = x.astype(jnp.float32)
    return (xf * lax.rsqrt(jnp.sum(xf * xf, axis=-1, keepdims=True) + EPS)).astype(x.dtype)


def _split(u, sizes):
    idx = np.cumsum(sizes)[:-1].tolist()
    return jnp.split(u, idx, axis=-1)


def _causal_conv(x, buf, w, b=None):
    L = x.shape[1]
    xx = jnp.concatenate([buf.astype(x.dtype), x], axis=1)
    out = xx[:, 0:L] * w[0]
    for tap in range(1, w.shape[0]):
        out = out + xx[:, tap:tap + L] * w[tap]
    if b is not None:
        out = out + b
    return out, xx[:, L:]


def _chunk_len(L, c):
    return c if L % c == 0 else L


def _to_chunks(t, C):
    B_, L = t.shape[:2]
    return jnp.moveaxis(t.reshape(B_, L // C, C, *t.shape[2:]), 2, 3)


def _from_chunks(o):
    B_, N, H, C = o.shape[:4]
    return jnp.moveaxis(o, 2, 3).reshape(B_, N * C, H, *o.shape[4:])


def _inter_chunk(s0, qg, kend, gend, A, v):
    def step(S, inp):
        qg_n, kend_n, gend_n, A_n, v_n = inp
        o = jnp.einsum('bhid,bhde->bhie', qg_n, S) + jnp.einsum('bhij,bhje->bhie', A_n, v_n)
        S = gend_n * S + jnp.einsum('bhid,bhie->bhde', kend_n, v_n)
        return S, o
    xs = tuple(jnp.moveaxis(t, 1, 0) for t in (qg, kend, gend, A, v))
    S, o = lax.scan(step, s0.astype(v.dtype), xs)
    return _from_chunks(jnp.moveaxis(o, 0, 1)), S


def _gated_delta(q, k, v, beta, g, s0):
    f32 = jnp.float32
    dt_ = v.dtype
    DV = v.shape[-1]
    C = _chunk_len(q.shape[1], DN_CHUNK)
    q, k, v, beta, g = (_to_chunks(t, C) for t in (q, k, v, beta, g))
    G = jnp.cumsum(g.astype(f32), axis=-1)
    causal = jnp.tril(jnp.ones((C, C), dtype=bool))
    strict = causal & ~jnp.eye(C, dtype=bool)
    decay = jnp.exp(jnp.where(causal, G[..., :, None] - G[..., None, :], -jnp.inf))
    bf = beta.astype(f32)
    kk = jnp.einsum('bnhid,bnhjd->bnhij', k, k).astype(f32)
    A = jnp.where(strict, bf[..., :, None] * kk * decay, 0.0)
    M = A + jnp.eye(C, dtype=f32)
    rhs = jnp.concatenate([bf[..., None] * v.astype(f32),
                           (bf * jnp.exp(G))[..., None] * k.astype(f32)], axis=-1)
    sol = lax.linalg.triangular_solve(M, rhs, left_side=True, lower=True)
    u0 = sol[..., :DV].astype(dt_)
    wk = sol[..., DV:].astype(dt_)
    qk = (jnp.einsum('bnhid,bnhjd->bnhij', q, k).astype(f32) * decay).astype(dt_)
    qg = q * jnp.exp(G)[..., None].astype(dt_)
    kend = k * jnp.exp(G[..., -1:] - G)[..., None].astype(dt_)
    gend = jnp.exp(G[..., -1]).astype(dt_)[..., None, None]

    def step(S, inp):
        u0_n, wk_n, qk_n, qg_n, kend_n, gend_n = inp
        u = u0_n - jnp.einsum('bhid,bhde->bhie', wk_n, S)
        o = jnp.einsum('bhid,bhde->bhie', qg_n, S) + jnp.einsum('bhij,bhje->bhie', qk_n, u)
        S = gend_n * S + jnp.einsum('bhid,bhie->bhde', kend_n, u)
        return S, o
    xs = tuple(jnp.moveaxis(t, 1, 0) for t in (u0, wk, qk, qg, kend, gend))
    S, o = lax.scan(step, s0.astype(dt_), xs)
    return _from_chunks(jnp.moveaxis(o, 0, 1)), S


def _gla(q, k, v, glog, s0):
    f32 = jnp.float32
    dt_ = v.dtype
    C = _chunk_len(q.shape[1], GLA_CHUNK)
    q, k, v, glog = (_to_chunks(t, C) for t in (q, k, v, glog))
    G = jnp.cumsum(glog.astype(f32), axis=3)
    causal = jnp.tril(jnp.ones((C, C), dtype=bool))
    decay = jnp.exp(jnp.where(causal[:, :, None], G[..., :, None, :] - G[..., None, :, :], -jnp.inf))
    A = jnp.einsum('bnhid,bnhjd,bnhijd->bnhij', q.astype(f32), k.astype(f32), decay).astype(dt_)
    qg = q * jnp.exp(G).astype(dt_)
    kend = k * jnp.exp(G[..., -1:, :] - G).astype(dt_)
    gend = jnp.exp(G[..., -1, :]).astype(dt_)[..., :, None]
    return _inter_chunk(s0, qg, kend, gend, A, v)


def _scalar_decay(q, k, v, g, s0):
    f32 = jnp.float32
    dt_ = v.dtype
    C = _chunk_len(q.shape[1], SSD_CHUNK)
    q, k, v, g = (_to_chunks(t, C) for t in (q, k, v, g))
    G = jnp.cumsum(g.astype(f32), axis=-1)
    causal = jnp.tril(jnp.ones((C, C), dtype=bool))
    decay = jnp.exp(jnp.where(causal, G[..., :, None] - G[..., None, :], -jnp.inf))
    A = (jnp.einsum('bnhid,bnhjd->bnhij', q, k).astype(f32) * decay).astype(dt_)
    qg = q * jnp.exp(G)[..., None].astype(dt_)
    kend = k * jnp.exp(G[..., -1:] - G)[..., None].astype(dt_)
    gend = jnp.exp(G[..., -1]).astype(dt_)[..., None, None]
    return _inter_chunk(s0, qg, kend, gend, A, v)


def _lin_combine(l, r):
    return (l[0] * r[0], r[0] * l[1] + r[1])


def _mixer_ab(xn, st_dn, st_conv, st_gla, w, j):
    B_, L, _ = xn.shape
    u = xn @ w['ab_w_in'][j]
    qkv, beta_in, a_in, z_dn, gq, gk, gv, g_lr, z_gla = _split(u, AB_SIZES)
    qkv, conv_new = _causal_conv(qkv, st_conv, w['dn_conv_w'][j])
    qkv = jax.nn.silu(qkv)
    q, k, v = _split(qkv, (DN_HEADS * DN_DK, DN_HEADS * DN_DK, DN_W))
    q = _l2norm(q.reshape(B_, L, DN_HEADS, DN_DK)) * (DN_DK ** -0.5)
    k = _l2norm(k.reshape(B_, L, DN_HEADS, DN_DK))
    v = v.reshape(B_, L, DN_HEADS, DN_DV)
    beta = jax.nn.sigmoid(beta_in)
    g = -jnp.exp(w['dn_a_log'][j]) * jax.nn.softplus(a_in + w['dn_dt_bias'][j])
    o_dn, dn_new = _gated_delta(q, k, v, beta, g, st_dn)
    o_dn = _rmsnorm(o_dn, w['dn_norm'][j]).reshape(B_, L, DN_W) * jax.nn.silu(z_dn)
    gq = gq.reshape(B_, L, GLA_HEADS, GLA_DK) * (GLA_DK ** -0.5)
    gk = gk.reshape(B_, L, GLA_HEADS, GLA_DK)
    gv = gv.reshape(B_, L, GLA_HEADS, GLA_DV)
    glog = jax.nn.log_sigmoid(g_lr @ w['gla_wa2'][j] + w['gla_ba'][j]) / GLA_GATE_NORM
    glog = glog.reshape(B_, L, GLA_HEADS, GLA_DK)
    o_gla, gla_new = _gla(gq, gk, gv, glog, st_gla)
    o_gla = _rmsnorm(o_gla, w['gla_norm'][j]).reshape(B_, L, GLA_W) * jax.nn.silu(z_gla)
    y = jnp.concatenate([o_dn, o_gla], axis=-1) @ w['ab_w_out'][j]
    return y, dn_new, conv_new, gla_new


def _mixer_cd(xn, st_ssd, st_sconv, st_lru, st_lconv, w, j):
    B_, L, _ = xn.shape
    u = xn @ w['cd_w_in'][j]
    z, xbc, dt_in, z_lru, x_lru = _split(u, CD_SIZES)
    xbc, sconv_new = _causal_conv(xbc, st_sconv, w['ssd_conv_w'][j], w['ssd_conv_b'][j])
    xbc = jax.nn.silu(xbc)
    xs, bm, cm = _split(xbc, (SSD_W, SSD_GROUPS * SSD_STATE, SSD_GROUPS * SSD_STATE))
    xs = xs.reshape(B_, L, SSD_HEADS, SSD_HEADDIM)
    rep = SSD_HEADS // SSD_GROUPS
    bm = jnp.repeat(bm.reshape(B_, L, SSD_GROUPS, SSD_STATE), rep, axis=2)
    cm = jnp.repeat(cm.reshape(B_, L, SSD_GROUPS, SSD_STATE), rep, axis=2)
    dt = jax.nn.softplus(dt_in + w['ssd_dt_bias'][j])
    g = -jnp.exp(w['ssd_a_log'][j]) * dt
    y, ssd_new = _scalar_decay(cm, bm, xs * dt[..., None], g, st_ssd)
    y = y + w['ssd_d'][j][:, None] * xs
    y = (y.reshape(B_, L, SSD_W) * jax.nn.silu(z)).reshape(B_, L, SSD_GROUPS, SSD_W // SSD_GROUPS)
    y_ssd = _rmsnorm(y, w['ssd_norm'][j].reshape(SSD_GROUPS, -1)).reshape(B_, L, SSD_W)
    xc, lconv_new = _causal_conv(x_lru, st_lconv, w['lru_conv_w'][j], w['lru_conv_b'][j])
    xb = xc.reshape(B_, L, LRU_BLOCKS, LRU_BS)
    r = jax.nn.sigmoid(jnp.einsum('blnc,ncd->blnd', xb, w['lru_wa'][j]).reshape(B_, L, LRU_W) + w['lru_ba'][j])
    i = jax.nn.sigmoid(jnp.einsum('blnc,ncd->blnd', xb, w['lru_wx'][j]).reshape(B_, L, LRU_W) + w['lru_bx'][j])
    log_a = -LRU_C * r * jax.nn.softplus(-w['lru_lambda'][j])
    a = jnp.exp(log_a)
    b = jnp.sqrt(-jnp.expm1(2.0 * log_a)) * (i * xc)
    b = b.at[:, 0].add(a[:, 0] * st_lru.astype(b.dtype))
    _, hs = lax.associative_scan(_lin_combine, (a, b), axis=1)
    y_lru = hs * jax.nn.silu(z_lru)
    y = jnp.concatenate([y_ssd, y_lru], axis=-1) @ w['cd_w_out'][j]
    return y, ssd_new, sconv_new, hs[:, -1], lconv_new


def _trunk(x, p, states, w):
    st_dn, st_dnc, st_gla, st_ssd, st_ssdc, st_lru, st_lruc = states
    outs = [[] for _ in range(7)]
    h = x
    for li in range(DEPTH):
        xn = _rmsnorm(h, w['norm_g'][li])
        j = li // 2
        if li % 2 == 0:
            y, *new = _mixer_ab(xn, st_dn[j], st_dnc[j], st_gla[j], w, j)
            slots = (0, 1, 2)
        else:
            y, *new = _mixer_cd(xn, st_ssd[j], st_ssdc[j], st_lru[j], st_lruc[j], w, j)
            slots = (3, 4, 5, 6)
        for s, n in zip(slots, new):
            outs[s].append(n)
        h = h + y
        gate = jax.nn.sigmoid(_rmsnorm(h, w['ple_norm'][li]) @ w['ple_w_gate'][li])
        h = h + gate * (p[li] @ w['ple_w_proj'][li])
    return _rmsnorm(h, w['final_norm']), [jnp.stack(o) for o in outs]


def setup_inputs(seed: int = 0) -> dict:
    key = jax.random.key(seed)
    ks = iter(jax.random.split(key, 64))

    def nrm(shape, scale):
        return scale * jax.random.normal(next(ks), shape, jnp.float32)

    def unif(shape, lo, hi):
        return jax.random.uniform(next(ks), shape, jnp.float32, lo, hi)

    def dt_bias(shape):
        dt = jnp.exp(unif(shape, math.log(1e-3), math.log(1e-1)))
        return dt + jnp.log(-jnp.expm1(-dt))

    s_lam = unif((N_ODD, LRU_W), 0.9, 0.999) ** (1.0 / LRU_C)
    return {
        'x_prompt': nrm((BATCH, SEQ, D_MODEL), 1.0),
        'x_sample': nrm((DEC_BATCH, DEC_SEQ, D_MODEL), 1.0),
        'state_dn': nrm((N_EVEN, DEC_BATCH, DN_HEADS, DN_DK, DN_DV), 0.1),
        'state_dn_conv': nrm((N_EVEN, DEC_BATCH, CONV_W - 1, DN_QKV), 1.0),
        'state_gla': nrm((N_EVEN, DEC_BATCH, GLA_HEADS, GLA_DK, GLA_DV), 0.3),
        'state_ssd': nrm((N_ODD, DEC_BATCH, SSD_HEADS, SSD_STATE, SSD_HEADDIM), 0.3),
        'state_ssd_conv': nrm((N_ODD, DEC_BATCH, CONV_W - 1, SSD_XBC), 1.0),
        'state_lru': nrm((N_ODD, DEC_BATCH, LRU_W), 0.5),
        'state_lru_conv': nrm((N_ODD, DEC_BATCH, CONV_W - 1, LRU_W), 1.0),
        'p_prompt': nrm((DEPTH, BATCH, SEQ, PLE_DIM), 1.0),
        'p_sample': nrm((DEPTH, DEC_BATCH, DEC_SEQ, PLE_DIM), 1.0),
        'norm_g': 1.0 + nrm((DEPTH, D_MODEL), 0.05),
        'final_norm': 1.0 + nrm((D_MODEL,), 0.05),
        'ab_w_in': nrm((N_EVEN, D_MODEL, AB_IN), D_MODEL ** -0.5),
        'dn_conv_w': nrm((N_EVEN, CONV_W, DN_QKV), 0.5),
        'dn_a_log': jnp.log(unif((N_EVEN, DN_HEADS), 1.0, 16.0)),
        'dn_dt_bias': dt_bias((N_EVEN, DN_HEADS)),
        'dn_norm': 1.0 + nrm((N_EVEN, DN_DV), 0.05),
        'gla_wa2': nrm((N_EVEN, GLA_RANK, GLA_KW), GLA_RANK ** -0.5),
        'gla_ba': nrm((N_EVEN, GLA_KW), 0.01),
        'gla_norm': 1.0 + nrm((N_EVEN, GLA_DV), 0.05),
        'ab_w_out': nrm((N_EVEN, AB_OUT, D_MODEL), AB_OUT ** -0.5),
        'cd_w_in': nrm((N_ODD, D_MODEL, CD_IN), D_MODEL ** -0.5),
        'ssd_conv_w': nrm((N_ODD, CONV_W, SSD_XBC), 0.5),
        'ssd_conv_b': nrm((N_ODD, SSD_XBC), 0.01),
        'ssd_a_log': jnp.log(unif((N_ODD, SSD_HEADS), 1.0, 16.0)),
        'ssd_dt_bias': dt_bias((N_ODD, SSD_HEADS)),
        'ssd_d': 1.0 + nrm((N_ODD, SSD_HEADS), 0.05),
        'ssd_norm': 1.0 + nrm((N_ODD, SSD_W), 0.05),
        'lru_conv_w': nrm((N_ODD, CONV_W, LRU_W), 0.5),
        'lru_conv_b': nrm((N_ODD, LRU_W), 0.01),
        'lru_wa': nrm((N_ODD, LRU_BLOCKS, LRU_BS, LRU_BS), LRU_BS ** -0.5),
        'lru_ba': nrm((N_ODD, LRU_W), 0.01),
        'lru_wx': nrm((N_ODD, LRU_BLOCKS, LRU_BS, LRU_BS), LRU_BS ** -0.5),
        'lru_bx': nrm((N_ODD, LRU_W), 0.01),
        'lru_lambda': jnp.log(s_lam) - jnp.log1p(-s_lam),
        'cd_w_out': nrm((N_ODD, CD_OUT, D_MODEL), CD_OUT ** -0.5),
        'ple_w_proj': nrm((DEPTH, PLE_DIM, D_MODEL), 0.5 * PLE_DIM ** -0.5),
        'ple_norm': 1.0 + nrm((DEPTH, D_MODEL), 0.05),
        'ple_w_gate': nrm((DEPTH, D_MODEL, D_MODEL), D_MODEL ** -0.5),
    }


def reference(x_prompt, x_sample, state_dn, state_dn_conv, state_gla, state_ssd, state_ssd_conv,
              state_lru, state_lru_conv, p_prompt, p_sample, norm_g, final_norm, ab_w_in, dn_conv_w,
              dn_a_log, dn_dt_bias, dn_norm, gla_wa2, gla_ba, gla_norm, ab_w_out, cd_w_in, ssd_conv_w,
              ssd_conv_b, ssd_a_log, ssd_dt_bias, ssd_d, ssd_norm, lru_conv_w, lru_conv_b, lru_wa, lru_ba,
              lru_wx, lru_bx, lru_lambda, cd_w_out, ple_w_proj, ple_norm, ple_w_gate):
    w = {
        'norm_g': norm_g, 'final_norm': final_norm,
        'ab_w_in': ab_w_in, 'dn_conv_w': dn_conv_w, 'dn_a_log': dn_a_log, 'dn_dt_bias': dn_dt_bias,
        'dn_norm': dn_norm, 'gla_wa2': gla_wa2, 'gla_ba': gla_ba, 'gla_norm': gla_norm, 'ab_w_out': ab_w_out,
        'cd_w_in': cd_w_in, 'ssd_conv_w': ssd_conv_w, 'ssd_conv_b': ssd_conv_b, 'ssd_a_log': ssd_a_log,
        'ssd_dt_bias': ssd_dt_bias, 'ssd_d': ssd_d, 'ssd_norm': ssd_norm,
        'lru_conv_w': lru_conv_w, 'lru_conv_b': lru_conv_b, 'lru_wa': lru_wa, 'lru_ba': lru_ba,
        'lru_wx': lru_wx, 'lru_bx': lru_bx, 'lru_lambda': lru_lambda, 'cd_w_out': cd_w_out,
        'ple_w_proj': ple_w_proj, 'ple_norm': ple_norm, 'ple_w_gate': ple_w_gate,
    }
    B_ = x_prompt.shape[0]
    dt_ = x_prompt.dtype
    zero_states = (
        jnp.zeros((N_EVEN, B_, DN_HEADS, DN_DK, DN_DV), dt_),
        jnp.zeros((N_EVEN, B_, CONV_W - 1, DN_QKV), dt_),
        jnp.zeros((N_EVEN, B_, GLA_HEADS, GLA_DK, GLA_DV), dt_),
        jnp.zeros((N_ODD, B_, SSD_HEADS, SSD_STATE, SSD_HEADDIM), dt_),
        jnp.zeros((N_ODD, B_, CONV_W - 1, SSD_XBC), dt_),
        jnp.zeros((N_ODD, B_, LRU_W), dt_),
        jnp.zeros((N_ODD, B_, CONV_W - 1, LRU_W), dt_),
    )
    y_prompt, (dn_p, dnc_p, gla_p, ssd_p, ssdc_p, lru_p, lruc_p) = _trunk(x_prompt, p_prompt, zero_states, w)
    sample_states = (state_dn, state_dn_conv, state_gla, state_ssd, state_ssd_conv, state_lru, state_lru_conv)
    y_sample, (dn_s, dnc_s, gla_s, ssd_s, ssdc_s, lru_s, lruc_s) = _trunk(x_sample, p_sample, sample_states, w)
    return (y_prompt, y_sample, dn_p, dnc_p, gla_p, ssd_p, ssdc_p, lru_p, lruc_p,
            dn_s, dnc_s, gla_s, ssd_s, ssdc_s, lru_s, lruc_s)
```

```python
import functools

import jax
import jax.numpy as jnp
from jax import lax
from jax.experimental import pallas as pl
from jax.experimental.pallas import tpu as pltpu

F32 = jnp.float32
BF16 = jnp.bfloat16
HI = lax.Precision.HIGHEST

EPS = 1e-6
CONV_W = 4
PLE_DIM = 256
DN_HEADS, DN_DK, DN_DV = 8, 128, 128
DN_W = DN_HEADS * DN_DV
DN_QKV = 2 * DN_HEADS * DN_DK + DN_W
GLA_HEADS, GLA_DK, GLA_DV = 4, 128, 256
GLA_KW = GLA_HEADS * GLA_DK
GLA_W = GLA_HEADS * GLA_DV
GLA_RANK = 16
GLA_GATE_NORM = 16.0
AB_SIZES = (DN_QKV, DN_HEADS, DN_HEADS, DN_W, GLA_KW, GLA_KW, GLA_W, GLA_RANK, GLA_W)
SSD_HEADS, SSD_HEADDIM, SSD_GROUPS, SSD_STATE = 16, 64, 4, 64
SSD_W = SSD_HEADS * SSD_HEADDIM
SSD_XBC = SSD_W + 2 * SSD_GROUPS * SSD_STATE
LRU_W, LRU_BLOCKS = 1024, 16
LRU_BS = LRU_W // LRU_BLOCKS
LRU_C = 8.0
CD_SIZES = (SSD_W, SSD_XBC, SSD_HEADS, LRU_W, LRU_W)

LANES = 128
SUBLANES = 8
VMEM_LIMIT = 52 * 1024 * 1024

AB_QKV, AB_ZDN, AB_GQ, AB_GK, AB_GV, AB_ZGLA = 0, 3072, 4096, 4608, 5120, 6144
AB_MAIN = 7168
CD_Z, CD_ZLRU, CD_XLRU, CD_XBC = 0, 1024, 2048, 3072
CD_MAIN = 4608


def _silu(x):
    return x * jax.nn.sigmoid(x)


def _nt(a, b):
    return lax.dot_general(a, b, (((1,), (1,)), ((), ())), preferred_element_type=F32)


def _tn(a, b):
    return lax.dot_general(a, b, (((0,), (0,)), ((), ())), preferred_element_type=F32)


def _mm(a, b):
    return jnp.dot(a, b, preferred_element_type=F32)


def _mm_hi(a, b):
    return jnp.dot(a, b, preferred_element_type=F32, precision=HI)


def _iota2(shape, dim):
    return lax.broadcasted_iota(jnp.int32, shape, dim)


def _conv_rows(halo, x, w, bias=None):
    cat = jnp.concatenate([halo, x], axis=0)
    out = x * w[3:4]
    for s in (1, 2, 3):
        out = out + pltpu.roll(cat, s, axis=0)[SUBLANES:] * w[3 - s:4 - s]
    if bias is not None:
        out = out + bias
    return out


def _params(**kw):
    return pltpu.CompilerParams(vmem_limit_bytes=VMEM_LIMIT, **kw)


def _inproj_kernel(h_ref, g_ref, wm_ref, ws_ref, um_ref, us_ref, xn_ref):
    @pl.when(pl.program_id(1) == 0)
    def _():
        x = h_ref[...]
        r = lax.rsqrt(jnp.mean(x * x, axis=-1, keepdims=True) + EPS)
        xn = (x * r * g_ref[...]).astype(BF16)
        xn_ref[...] = xn
        us_ref[...] = _mm(xn, ws_ref[...])

    um_ref[...] = _mm(xn_ref[...], wm_ref[...])


def _inproj(h, g, wm, ws):
    T, D = h.shape
    Nm = wm.shape[1]
    tm = min(T, 1024)
    tn = 512
    return pl.pallas_call(
        _inproj_kernel,
        grid=(T // tm, Nm // tn),
        in_specs=[pl.BlockSpec((tm, D), lambda i, j: (i, 0)),
                  pl.BlockSpec((1, D), lambda i, j: (0, 0)),
                  pl.BlockSpec((D, tn), lambda i, j: (0, j)),
                  pl.BlockSpec((D, LANES), lambda i, j: (0, 0))],
        out_specs=[pl.BlockSpec((tm, tn), lambda i, j: (i, j)),
                   pl.BlockSpec((tm, LANES), lambda i, j: (i, 0))],
        out_shape=[jax.ShapeDtypeStruct((T, Nm), F32), jax.ShapeDtypeStruct((T, LANES), F32)],
        scratch_shapes=[pltpu.VMEM((tm, D), BF16)],
        compiler_params=_params(dimension_semantics=("parallel", "arbitrary")),
        name="inproj",
    )(h, g, wm, ws)


def _outproj_kernel(oa_ref, ob_ref, h_ref, p_ref, wa_ref, wb_ref, pg_ref, wg_ref, wp_ref, fg_ref, out_ref, *, final):
    h1 = h_ref[...] + _mm(oa_ref[...], wa_ref[...]) + _mm(ob_ref[...], wb_ref[...])
    r = lax.rsqrt(jnp.mean(h1 * h1, axis=-1, keepdims=True) + EPS)
    hn = (h1 * r * pg_ref[...]).astype(BF16)
    gate = jax.nn.sigmoid(_mm(hn, wg_ref[...]))
    pp = _mm(p_ref[...].astype(BF16), wp_ref[...])
    h2 = h1 + gate * pp
    if final:
        r2 = lax.rsqrt(jnp.mean(h2 * h2, axis=-1, keepdims=True) + EPS)
        h2 = h2 * r2 * fg_ref[...]
    out_ref[...] = h2


def _outproj(oa, ob, h, p_all, li, wa, wb, pg, wg, wp, fg, final):
    T, D = h.shape
    Wa, Wb = oa.shape[1], ob.shape[1]
    tm = min(T, 256)
    const = lambda i: (0, 0)
    return pl.pallas_call(
        functools.partial(_outproj_kernel, final=final),
        grid=(T // tm,),
        in_specs=[pl.BlockSpec((tm, Wa), lambda i: (i, 0)),
                  pl.BlockSpec((tm, Wb), lambda i: (i, 0)),
                  pl.BlockSpec((tm, D), lambda i: (i, 0)),
                  pl.BlockSpec((None, tm, PLE_DIM), lambda i: (li, i, 0)),
                  pl.BlockSpec((Wa, D), const),
                  pl.BlockSpec((Wb, D), const),
                  pl.BlockSpec((1, D), const),
                  pl.BlockSpec((D, D), const),
                  pl.BlockSpec((PLE_DIM, D), const),
                  pl.BlockSpec((1, D), const)],
        out_specs=pl.BlockSpec((tm, D), lambda i: (i, 0)),
        out_shape=jax.ShapeDtypeStruct((T, D), F32),
        compiler_params=_params(dimension_semantics=("parallel",)),
        name="outproj",
    )(oa, ob, h, p_all, wa, wb, pg, wg, wp, fg)


def _unit_lower_inverse(A, C):
    eye = (_iota2((C, C), 0) == _iota2((C, C), 1)).astype(F32)
    T = eye - A
    X = A
    p = 2
    while p < C:
        X = _mm_hi(X, X)
        T = T + _mm_hi(T, X)
        p *= 2
    return T


def _dn_kernel(qkv_ref, z_ref, us_ref, cst_ref, st_ref, cw_ref, prm_ref, nrm_ref, o_ref, so_ref, halo_ref,
               *, bb, tb, C, lvalid, ltotal):
    t = pl.program_id(1)

    @pl.when(t == 0)
    def _():
        so_ref[...] = st_ref[...]
        halo_ref[:, 0:5, :] = jnp.zeros((bb, 5, DN_QKV), F32)
        halo_ref[:, 5:8, :] = cst_ref[...]

    ii = _iota2((C, C), 0)
    jj = _iota2((C, C), 1)
    tril = (jj <= ii).astype(F32)
    lower = jj <= ii
    strict = jj < ii
    lane = _iota2((1, LANES), 1)
    gmask = (lane >= DN_HEADS) & (lane < 2 * DN_HEADS)
    neg_a = -jnp.exp(prm_ref[0:1, :])
    bias = prm_ref[1:2, :]
    cw = cw_ref[...]
    nrm = nrm_ref[...]
    masked = lvalid < ltotal

    def chunk(bi, c):
        r0 = pl.multiple_of(c * C, C)
        x = qkv_ref[bi, pl.ds(r0, C), :]
        prev = qkv_ref[bi, pl.ds(pl.multiple_of(jnp.maximum(r0 - SUBLANES, 0), SUBLANES), SUBLANES), :]
        halo = jnp.where(c == 0, halo_ref[bi], prev)
        act = _silu(_conv_rows(halo, x, cw))
        us = us_ref[bi, pl.ds(r0, C), :]
        g = jnp.where(gmask, neg_a * jax.nn.softplus(us + bias), 0.0)
        beta_all = jax.nn.sigmoid(us)
        if masked:
            valid = (t * tb + r0 + _iota2((C, 1), 0)) < lvalid
            g = jnp.where(valid, g, 0.0)
            beta_all = jnp.where(valid, beta_all, 0.0)
        G = _mm_hi(tril, g)
        GT = G.T
        z = z_ref[bi, pl.ds(r0, C), :]
        for h in range(DN_HEADS):
            q = act[:, h * DN_DK:(h + 1) * DN_DK]
            k = act[:, DN_W + h * DN_DK:DN_W + (h + 1) * DN_DK]
            v = act[:, 2 * DN_W + h * DN_DV:2 * DN_W + (h + 1) * DN_DV]
            q = q * lax.rsqrt(jnp.sum(q * q, axis=-1, keepdims=True) + EPS) * (DN_DK ** -0.5)
            k = k * lax.rsqrt(jnp.sum(k * k, axis=-1, keepdims=True) + EPS)
            gc = G[:, DN_HEADS + h:DN_HEADS + h + 1]
            gr = GT[DN_HEADS + h:DN_HEADS + h + 1, :]
            beta = beta_all[:, h:h + 1]
            dm = jnp.exp(jnp.where(lower, gc - gr, -1e30))
            kb = k.astype(BF16)
            kk = _nt(kb, kb)
            qk = _nt(q.astype(BF16), kb)
            A = jnp.where(strict, beta * kk * dm, 0.0)
            T = _unit_lower_inverse(A, C)
            eg = jnp.exp(gc)
            rhs = jnp.concatenate([beta * v, (beta * eg) * k], axis=1)
            sol = _mm_hi(T, rhs)
            u0 = sol[:, :DN_DV]
            wk = sol[:, DN_DV:]
            S = so_ref[bi, h]
            Sb = S.astype(BF16)
            u = u0 - _mm(wk.astype(BF16), Sb)
            ub = u.astype(BF16)
            o = _mm((q * eg).astype(BF16), Sb) + _mm((qk * dm).astype(BF16), ub)
            glast = gc[C - 1:C, :]
            kend = (k * jnp.exp(glast - gc)).astype(BF16)
            so_ref[bi, h] = jnp.exp(glast) * S + _tn(kend, ub)
            o = o * lax.rsqrt(jnp.mean(o * o, axis=-1, keepdims=True) + EPS) * nrm
            o = o * _silu(z[:, h * DN_DV:(h + 1) * DN_DV])
            o_ref[bi, pl.ds(r0, C), h * DN_DV:(h + 1) * DN_DV] = o.astype(o_ref.dtype)

    def per_batch(bi, carry):
        def per_chunk(c, carry2):
            chunk(bi, c)
            return carry2
        lax.fori_loop(0, tb // C, per_chunk, 0)
        halo_ref[bi] = qkv_ref[bi, tb - SUBLANES:tb, :]
        return carry

    lax.fori_loop(0, bb, per_batch, 0)


def _dn_mixer(um, us, cst, st, cw, prm, nrm, *, bb, tb, C, lvalid):
    B, L, _ = um.shape
    kern = functools.partial(_dn_kernel, bb=bb, tb=tb, C=C, lvalid=lvalid, ltotal=L)
    const = lambda b, t: (0, 0)
    return pl.pallas_call(
        kern,
        grid=(B // bb, L // tb),
        in_specs=[pl.BlockSpec((bb, tb, DN_QKV), lambda b, t: (b, t, AB_QKV // DN_QKV)),
                  pl.BlockSpec((bb, tb, DN_W), lambda b, t: (b, t, AB_ZDN // DN_W)),
                  pl.BlockSpec((bb, tb, LANES), lambda b, t: (b, t, 0)),
                  pl.BlockSpec((bb, CONV_W - 1, DN_QKV), lambda b, t: (b, 0, 0)),
                  pl.BlockSpec((bb, DN_HEADS, DN_DK, DN_DV), lambda b, t: (b, 0, 0, 0)),
                  pl.BlockSpec((CONV_W, DN_QKV), const),
                  pl.BlockSpec((2, LANES), const),
                  pl.BlockSpec((1, DN_DV), const)],
        out_specs=[pl.BlockSpec((bb, tb, DN_W), lambda b, t: (b, t, 0)),
                   pl.BlockSpec((bb, DN_HEADS, DN_DK, DN_DV), lambda b, t: (b, 0, 0, 0))],
        out_shape=[jax.ShapeDtypeStruct((B, L, DN_W), BF16),
                   jax.ShapeDtypeStruct((B, DN_HEADS, DN_DK, DN_DV), F32)],
        scratch_shapes=[pltpu.VMEM((bb, SUBLANES, DN_QKV), F32)],
        compiler_params=_params(dimension_semantics=("parallel", "arbitrary")),
        name="dn_mixer",
    )(um, um, us, cst, st, cw, prm, nrm)


def _gla_kernel(q_ref, k_ref, v_ref, z_ref, us_ref, st_ref, wa_ref, ba_ref, nrm_ref, o_ref, so_ref,
                *, bb, tb, C, lvalid, ltotal):
    t = pl.program_id(1)

    @pl.when(t == 0)
    def _():
        so_ref[...] = st_ref[...]

    ii = _iota2((C, C), 0)
    jj = _iota2((C, C), 1)
    tril = (jj <= ii).astype(F32)
    wa = wa_ref[...]
    ba = ba_ref[...]
    nrm = nrm_ref[...]
    masked = lvalid < ltotal

    def chunk(bi, c):
        r0 = pl.multiple_of(c * C, C)
        q = q_ref[bi, pl.ds(r0, C), :] * (GLA_DK ** -0.5)
        k = k_ref[bi, pl.ds(r0, C), :]
        v = v_ref[bi, pl.ds(r0, C), :]
        z = z_ref[bi, pl.ds(r0, C), :]
        us = us_ref[bi, pl.ds(r0, C), :]
        glog = jax.nn.log_sigmoid(_mm(us.astype(BF16), wa) + ba) * (1.0 / GLA_GATE_NORM)
        if masked:
            valid = (t * tb + r0 + _iota2((C, 1), 0)) < lvalid
            glog = jnp.where(valid, glog, 0.0)
            k = jnp.where(valid, k, 0.0)
            v = jnp.where(valid, v, 0.0)
        G = _mm_hi(tril, glog)
        for h in range(GLA_HEADS):
            Gh = G[:, h * GLA_DK:(h + 1) * GLA_DK]
            qh = q[:, h * GLA_DK:(h + 1) * GLA_DK]
            kh = k[:, h * GLA_DK:(h + 1) * GLA_DK]
            vh = v[:, h * GLA_DV:(h + 1) * GLA_DV].astype(BF16)
            AT = jnp.zeros((C, C), F32)
            for i in range(C):
                e = jnp.exp(jnp.minimum(Gh[i:i + 1, :] - Gh, 0.0))
                col = jnp.sum(kh * e * qh[i:i + 1, :], axis=1, keepdims=True)
                AT = AT + jnp.where(jj == i, col, 0.0)
            AT = jnp.where(ii <= jj, AT, 0.0)
            S = so_ref[bi, h]
            o = _mm((qh * jnp.exp(Gh)).astype(BF16), S.astype(BF16)) + _tn(AT.astype(BF16), vh)
            glast = Gh[C - 1:C, :]
            kend = (kh * jnp.exp(glast - Gh)).astype(BF16)
            gend_col = jnp.exp(Gh.T[:, C - 1:C])
            so_ref[bi, h] = gend_col * S + _tn(kend, vh)
            o = o * lax.rsqrt(jnp.mean(o * o, axis=-1, keepdims=True) + EPS) * nrm
            o = o * _silu(z[:, h * GLA_DV:(h + 1) * GLA_DV])
            o_ref[bi, pl.ds(r0, C), h * GLA_DV:(h + 1) * GLA_DV] = o.astype(o_ref.dtype)

    def per_batch(bi, carry):
        def per_chunk(c, carry2):
            chunk(bi, c)
            return carry2
        lax.fori_loop(0, tb // C, per_chunk, 0)
        return carry

    lax.fori_loop(0, bb, per_batch, 0)


def _gla_mixer(um, us, st, wa, ba, nrm, *, bb, tb, C, lvalid):
    B, L, _ = um.shape
    kern = functools.partial(_gla_kernel, bb=bb, tb=tb, C=C, lvalid=lvalid, ltotal=L)
    const = lambda b, t: (0, 0)
    return pl.pallas_call(
        kern,
        grid=(B // bb, L // tb),
        in_specs=[pl.BlockSpec((bb, tb, GLA_KW), lambda b, t: (b, t, AB_GQ // GLA_KW)),
                  pl.BlockSpec((bb, tb, GLA_KW), lambda b, t: (b, t, AB_GK // GLA_KW)),
                  pl.BlockSpec((bb, tb, GLA_W), lambda b, t: (b, t, AB_GV // GLA_W)),
                  pl.BlockSpec((bb, tb, GLA_W), lambda b, t: (b, t, AB_ZGLA // GLA_W)),
                  pl.BlockSpec((bb, tb, LANES), lambda b, t: (b, t, 0)),
                  pl.BlockSpec((bb, GLA_HEADS, GLA_DK, GLA_DV), lambda b, t: (b, 0, 0, 0)),
                  pl.BlockSpec((LANES, GLA_KW), const),
                  pl.BlockSpec((1, GLA_KW), const),
                  pl.BlockSpec((1, GLA_DV), const)],
        out_specs=[pl.BlockSpec((bb, tb, GLA_W), lambda b, t: (b, t, 0)),
                   pl.BlockSpec((bb, GLA_HEADS, GLA_DK, GLA_DV), lambda b, t: (b, 0, 0, 0))],
        out_shape=[jax.ShapeDtypeStruct((B, L, GLA_W), BF16),
                   jax.ShapeDtypeStruct((B, GLA_HEADS, GLA_DK, GLA_DV), F32)],
        compiler_params=_params(dimension_semantics=("parallel", "arbitrary")),
        name="gla_mixer",
    )(um, um, um, um, us, st, wa, ba, nrm)


def _ssd_kernel(z_ref, xbc_ref, us_ref, cst_ref, st_ref, cw_ref, cb_ref, prm_ref, nrm_ref, o_ref, so_ref, halo_ref,
                *, bb, tb, C, lvalid, ltotal):
    t = pl.program_id(1)

    @pl.when(t == 0)
    def _():
        so_ref[...] = st_ref[...]
        halo_ref[:, 0:5, :] = jnp.zeros((bb, 5, SSD_XBC), F32)
        halo_ref[:, 5:8, :] = cst_ref[...]

    ii = _iota2((C, C), 0)
    jj = _iota2((C, C), 1)
    tril = (jj <= ii).astype(F32)
    lower = jj <= ii
    lane = _iota2((1, LANES), 1)
    hmask = lane < SSD_HEADS
    neg_a = -jnp.exp(prm_ref[0:1, :])
    bias = prm_ref[1:2, :]
    dskip = prm_ref[2:3, :]
    cw = cw_ref[...]
    cb = cb_ref[...]
    nrm = nrm_ref[...]
    masked = lvalid < ltotal
    gw = SSD_W // SSD_GROUPS
    rep = SSD_HEADS // SSD_GROUPS

    def chunk(bi, c):
        r0 = pl.multiple_of(c * C, C)
        x = xbc_ref[bi, pl.ds(r0, C), :]
        prev = xbc_ref[bi, pl.ds(pl.multiple_of(jnp.maximum(r0 - SUBLANES, 0), SUBLANES), SUBLANES), :]
        halo = jnp.where(c == 0, halo_ref[bi], prev)
        act = _silu(_conv_rows(halo, x, cw, cb))
        us = us_ref[bi, pl.ds(r0, C), :]
        dt = jnp.where(hmask, jax.nn.softplus(us + bias), 0.0)
        if masked:
            valid = (t * tb + r0 + _iota2((C, 1), 0)) < lvalid
            dt = jnp.where(valid, dt, 0.0)
        g = neg_a * dt
        G = _mm_hi(tril, g)
        GT = G.T
        z = z_ref[bi, pl.ds(r0, C), :]
        for gi in range(SSD_GROUPS):
            bm = act[:, SSD_W + gi * SSD_STATE:SSD_W + (gi + 1) * SSD_STATE]
            cm = act[:, SSD_W + SSD_GROUPS * SSD_STATE + gi * SSD_STATE:
                     SSD_W + SSD_GROUPS * SSD_STATE + (gi + 1) * SSD_STATE]
            cbm = _nt(cm.astype(BF16), bm.astype(BF16))
            ys = []
            for hj in range(rep):
                h = gi * rep + hj
                xs = act[:, h * SSD_HEADDIM:(h + 1) * SSD_HEADDIM]
                gc = G[:, h:h + 1]
                gr = GT[h:h + 1, :]
                dm = jnp.exp(jnp.where(lower, gc - gr, -1e30))
                v = (xs * dt[:, h:h + 1]).astype(BF16)
                S = so_ref[bi, h]
                o = _mm((cm * jnp.exp(gc)).astype(BF16), S.astype(BF16)) + _mm((cbm * dm).astype(BF16), v)
                glast = gc[C - 1:C, :]
                kend = (bm * jnp.exp(glast - gc)).astype(BF16)
                so_ref[bi, h] = jnp.exp(glast) * S + _tn(kend, v)
                ys.append(o + dskip[:, h:h + 1] * xs)
            y = jnp.concatenate(ys, axis=1) * _silu(z[:, gi * gw:(gi + 1) * gw])
            y = y * lax.rsqrt(jnp.mean(y * y, axis=-1, keepdims=True) + EPS) * nrm[:, gi * gw:(gi + 1) * gw]
            o_ref[bi, pl.ds(r0, C), gi * gw:(gi + 1) * gw] = y.astype(o_ref.dtype)

    def per_batch(bi, carry):
        def per_chunk(c, carry2):
            chunk(bi, c)
            return carry2
        lax.fori_loop(0, tb // C, per_chunk, 0)
        halo_ref[bi] = xbc_ref[bi, tb - SUBLANES:tb, :]
        return carry

    lax.fori_loop(0, bb, per_batch, 0)


def _ssd_mixer(um, us, cst, st, cw, cb, prm, nrm, *, bb, tb, C, lvalid):
    B, L, _ = um.shape
    kern = functools.partial(_ssd_kernel, bb=bb, tb=tb, C=C, lvalid=lvalid, ltotal=L)
    const = lambda b, t: (0, 0)
    return pl.pallas_call(
        kern,
        grid=(B // bb, L // tb),
        in_specs=[pl.BlockSpec((bb, tb, SSD_W), lambda b, t: (b, t, CD_Z // SSD_W)),
                  pl.BlockSpec((bb, tb, SSD_XBC), lambda b, t: (b, t, CD_XBC // SSD_XBC)),
                  pl.BlockSpec((bb, tb, LANES), lambda b, t: (b, t, 0)),
                  pl.BlockSpec((bb, CONV_W - 1, SSD_XBC), lambda b, t: (b, 0, 0)),
                  pl.BlockSpec((bb, SSD_HEADS, SSD_STATE, SSD_HEADDIM), lambda b, t: (b, 0, 0, 0)),
                  pl.BlockSpec((CONV_W, SSD_XBC), const),
                  pl.BlockSpec((1, SSD_XBC), const),
                  pl.BlockSpec((3, LANES), const),
                  pl.BlockSpec((1, SSD_W), const)],
        out_specs=[pl.BlockSpec((bb, tb, SSD_W), lambda b, t: (b, t, 0)),
                   pl.BlockSpec((bb, SSD_HEADS, SSD_STATE, SSD_HEADDIM), lambda b, t: (b, 0, 0, 0))],
        out_shape=[jax.ShapeDtypeStruct((B, L, SSD_W), BF16),
                   jax.ShapeDtypeStruct((B, SSD_HEADS, SSD_STATE, SSD_HEADDIM), F32)],
        scratch_shapes=[pltpu.VMEM((bb, SUBLANES, SSD_XBC), F32)],
        compiler_params=_params(dimension_semantics=("parallel", "arbitrary")),
        name="ssd_mixer",
    )(um, um, us, cst, st, cw, cb, prm, nrm)


def _lru_kernel(z_ref, x_ref, cst_ref, st_ref, cw_ref, cb_ref, wbd_ref, prm_ref, o_ref, so_ref, halo_ref, carry_ref,
                *, bb, tb, lvalid, ltotal):
    t = pl.program_id(1)
    nt = ltotal // tb

    @pl.when(t == 0)
    def _():
        carry_ref[...] = st_ref[...]
        halo_ref[:, 0:5, :] = jnp.zeros((bb, 5, LRU_W), F32)
        halo_ref[:, 5:8, :] = cst_ref[...]

    cw = cw_ref[...]
    cb = cb_ref[...]
    b_a = prm_ref[0:1, :]
    b_x = prm_ref[1:2, :]
    sp = jax.nn.softplus(-prm_ref[2:3, :])
    row = _iota2((tb, 1), 0)
    last_t, last_r = (lvalid - 1) // tb, (lvalid - 1) % tb

    def per_batch(bi, carry):
        x = x_ref[bi]
        xc = _conv_rows(halo_ref[bi], x, cw, cb)
        halo_ref[bi] = x[tb - SUBLANES:tb, :]
        xb = xc.astype(BF16)
        rs, is_ = [], []
        for cg in range(LRU_W // LANES):
            ri = _mm(xb[:, cg * LANES:(cg + 1) * LANES], wbd_ref[cg])
            rs.append(ri[:, :LANES])
            is_.append(ri[:, LANES:])
        r = jax.nn.sigmoid(jnp.concatenate(rs, axis=1) + b_a)
        i = jax.nn.sigmoid(jnp.concatenate(is_, axis=1) + b_x)
        log_a = (-LRU_C) * r * sp
        a = jnp.exp(log_a)
        b = jnp.sqrt(1.0 - jnp.exp(2.0 * log_a)) * (i * xc)
        d = 1
        while d < tb:
            m = row >= d
            a_s = pltpu.roll(a, d, axis=0)
            b_s = pltpu.roll(b, d, axis=0)
            b = jnp.where(m, a * b_s + b, b)
            a = jnp.where(m, a * a_s, a)
            d *= 2
        hs = a * carry_ref[bi] + b
        carry_ref[bi] = hs[tb - 1:tb, :]

        @pl.when(t == last_t)
        def _():
            so_ref[bi] = hs[last_r:last_r + 1, :]

        o_ref[bi] = (hs * _silu(z_ref[bi])).astype(o_ref.dtype)
        return carry

    lax.fori_loop(0, bb, per_batch, 0)


def _lru_mixer(um, cst, st, cw, cb, wbd, prm, *, bb, tb, lvalid):
    B, L, _ = um.shape
    kern = functools.partial(_lru_kernel, bb=bb, tb=tb, lvalid=lvalid, ltotal=L)
    const = lambda b, t: (0, 0)
    return pl.pallas_call(
        kern,
        grid=(B // bb, L // tb),
        in_specs=[pl.BlockSpec((bb, tb, LRU_W), lambda b, t: (b, t, CD_ZLRU // LRU_W)),
                  pl.BlockSpec((bb, tb, LRU_W), lambda b, t: (b, t, CD_XLRU // LRU_W)),
                  pl.BlockSpec((bb, CONV_W - 1, LRU_W), lambda b, t: (b, 0, 0)),
                  pl.BlockSpec((bb, 1, LRU_W), lambda b, t: (b, 0, 0)),
                  pl.BlockSpec((CONV_W, LRU_W), const),
                  pl.BlockSpec((1, LRU_W), const),
                  pl.BlockSpec((LRU_W // LANES, LANES, 2 * LANES), lambda b, t: (0, 0, 0)),
                  pl.BlockSpec((3, LRU_W), const)],
        out_specs=[pl.BlockSpec((bb, tb, LRU_W), lambda b, t: (b, t, 0)),
                   pl.BlockSpec((bb, 1, LRU_W), lambda b, t: (b, 0, 0))],
        out_shape=[jax.ShapeDtypeStruct((B, L, LRU_W), BF16),
                   jax.ShapeDtypeStruct((B, 1, LRU_W), F32)],
        scratch_shapes=[pltpu.VMEM((bb, SUBLANES, LRU_W), F32), pltpu.VMEM((bb, 1, LRU_W), F32)],
        compiler_params=_params(dimension_semantics=("parallel", "arbitrary")),
        name="lru_mixer",
    )(um, um, cst, st, cw, cb, wbd, prm)


def _split_cols(w, sizes):
    out, o = [], 0
    for s in sizes:
        out.append(w[:, o:o + s])
        o += s
    return out


def _lane_row(*pieces):
    v = jnp.concatenate([p.reshape(-1) for p in pieces])
    return jnp.pad(v, (0, LANES - v.shape[0])).reshape(1, LANES)


def _prep_weights(w):
    n_even = w['ab_w_in'].shape[0]
    n_odd = w['cd_w_in'].shape[0]
    D = w['ab_w_in'].shape[1]
    P = {'ab': [], 'cd': []}
    for j in range(n_even):
        qkv, beta, a, zdn, gq, gk, gv, glr, zgla = _split_cols(w['ab_w_in'][j], AB_SIZES)
        small = jnp.concatenate([beta, a, glr], axis=1)
        P['ab'].append(dict(
            w_main=jnp.concatenate([qkv, zdn, gq, gk, gv, zgla], axis=1).astype(BF16),
            w_small=jnp.pad(small, ((0, 0), (0, LANES - small.shape[1]))).astype(BF16),
            dn_cw=w['dn_conv_w'][j],
            dn_prm=jnp.concatenate([_lane_row(jnp.zeros((DN_HEADS,), F32), w['dn_a_log'][j]),
                                    _lane_row(jnp.zeros((DN_HEADS,), F32), w['dn_dt_bias'][j])], axis=0),
            dn_norm=w['dn_norm'][j].reshape(1, DN_DV),
            gla_wa=jnp.pad(w['gla_wa2'][j], ((2 * DN_HEADS, LANES - 2 * DN_HEADS - GLA_RANK), (0, 0))).astype(BF16),
            gla_ba=w['gla_ba'][j].reshape(1, GLA_KW),
            gla_norm=w['gla_norm'][j].reshape(1, GLA_DV),
            w_out_a=w['ab_w_out'][j][:DN_W].astype(BF16),
            w_out_b=w['ab_w_out'][j][DN_W:].astype(BF16),
        ))
    eye2 = jnp.eye(2, dtype=F32)
    for j in range(n_odd):
        z, xbc, dt, zlru, xlru = _split_cols(w['cd_w_in'][j], CD_SIZES)
        wa = w['lru_wa'][j].reshape(LRU_W // LANES, 2, LRU_BS, LRU_BS)
        wx = w['lru_wx'][j].reshape(LRU_W // LANES, 2, LRU_BS, LRU_BS)
        bd = lambda m: jnp.einsum('gbcd,be->gbced', m, eye2).reshape(LRU_W // LANES, LANES, LANES)
        P['cd'].append(dict(
            w_main=jnp.concatenate([z, zlru, xlru, xbc], axis=1).astype(BF16),
            w_small=jnp.pad(dt, ((0, 0), (0, LANES - dt.shape[1]))).astype(BF16),
            ssd_cw=w['ssd_conv_w'][j],
            ssd_cb=w['ssd_conv_b'][j].reshape(1, SSD_XBC),
            ssd_prm=jnp.concatenate([_lane_row(w['ssd_a_log'][j]), _lane_row(w['ssd_dt_bias'][j]),
                                     _lane_row(w['ssd_d'][j])], axis=0),
            ssd_norm=w['ssd_norm'][j].reshape(1, SSD_W),
            lru_cw=w['lru_conv_w'][j],
            lru_cb=w['lru_conv_b'][j].reshape(1, LRU_W),
            lru_wbd=jnp.concatenate([bd(wa), bd(wx)], axis=2).astype(BF16),
            lru_prm=jnp.stack([w['lru_ba'][j], w['lru_bx'][j], w['lru_lambda'][j]], axis=0),
            w_out_a=w['cd_w_out'][j][:SSD_W].astype(BF16),
            w_out_b=w['cd_w_out'][j][SSD_W:].astype(BF16),
        ))
    P['norm_g'] = w['norm_g'].reshape(-1, 1, D)
    P['ple_norm'] = w['ple_norm'].reshape(-1, 1, D)
    P['ple_w_gate'] = w['ple_w_gate'].astype(BF16)
    P['ple_w_proj'] = w['ple_w_proj'].astype(BF16)
    P['final_norm'] = w['final_norm'].reshape(1, D)
    return P


def _trunk(x, p, states, P, *, lvalid, bb, tb, c_dn, c_gla, c_ssd):
    B, L, D = x.shape
    depth = P['norm_g'].shape[0]
    st_dn, st_dnc, st_gla, st_ssd, st_ssdc, st_lru, st_lruc = states
    outs = [[] for _ in range(7)]
    T = B * L
    h = x.reshape(T, D)
    p_all = p.reshape(depth, T, PLE_DIM)
    lo = lvalid - (CONV_W - 1)
    for li in range(depth):
        j = li // 2
        if li % 2 == 0:
            W = P['ab'][j]
            um, us = _inproj(h, P['norm_g'][li], W['w_main'], W['w_small'])
            um = um.reshape(B, L, AB_MAIN)
            us = us.reshape(B, L, LANES)
            oa, dn_new = _dn_mixer(um, us, st_dnc[j], st_dn[j], W['dn_cw'], W['dn_prm'], W['dn_norm'],
                                   bb=bb, tb=tb, C=c_dn, lvalid=lvalid)
            ob, gla_new = _gla_mixer(um, us, st_gla[j], W['gla_wa'], W['gla_ba'], W['gla_norm'],
                                     bb=bb, tb=tb, C=c_gla, lvalid=lvalid)
            outs[0].append(dn_new)
            outs[1].append(um[:, lo:lvalid, AB_QKV:AB_QKV + DN_QKV])
            outs[2].append(gla_new)
        else:
            W = P['cd'][j]
            um, us = _inproj(h, P['norm_g'][li], W['w_main'], W['w_small'])
            um = um.reshape(B, L, CD_MAIN)
            us = us.reshape(B, L, LANES)
            oa, ssd_new = _ssd_mixer(um, us, st_ssdc[j], st_ssd[j], W['ssd_cw'], W['ssd_cb'], W['ssd_prm'],
                                     W['ssd_norm'], bb=bb, tb=tb, C=c_ssd, lvalid=lvalid)
            ob, lru_new = _lru_mixer(um, st_lruc[j], st_lru[j].reshape(B, 1, LRU_W), W['lru_cw'], W['lru_cb'],
                                     W['lru_wbd'], W['lru_prm'], bb=bb, tb=tb, lvalid=lvalid)
            outs[3].append(ssd_new)
            outs[4].append(um[:, lo:lvalid, CD_XBC:CD_XBC + SSD_XBC])
            outs[5].append(lru_new.reshape(B, LRU_W))
            outs[6].append(um[:, lo:lvalid, CD_XLRU:CD_XLRU + LRU_W])
        h = _outproj(oa.reshape(T, -1), ob.reshape(T, -1), h, p_all, li, W['w_out_a'], W['w_out_b'],
                     P['ple_norm'][li], P['ple_w_gate'][li], P['ple_w_proj'][li], P['final_norm'],
                     final=(li == depth - 1))
    return h.reshape(B, L, D), [jnp.stack(o) for o in outs]


def _pick_tb(L):
    tb = min(L, 512)
    assert L % tb == 0
    return tb


def kernel(x_prompt, x_sample, state_dn, state_dn_conv, state_gla, state_ssd, state_ssd_conv, state_lru, state_lru_conv, p_prompt, p_sample, norm_g, final_norm, ab_w_in, dn_conv_w, dn_a_log, dn_dt_bias, dn_norm, gla_wa2, gla_ba, gla_norm, ab_w_out, cd_w_in, ssd_conv_w, ssd_conv_b, ssd_a_log, ssd_dt_bias, ssd_d, ssd_norm, lru_conv_w, lru_conv_b, lru_wa, lru_ba, lru_wx, lru_bx, lru_lambda, cd_w_out, ple_w_proj, ple_norm, ple_w_gate):
    w = {
        'norm_g': norm_g, 'final_norm': final_norm,
        'ab_w_in': ab_w_in, 'dn_conv_w': dn_conv_w, 'dn_a_log': dn_a_log, 'dn_dt_bias': dn_dt_bias,
        'dn_norm': dn_norm, 'gla_wa2': gla_wa2, 'gla_ba': gla_ba, 'gla_norm': gla_norm, 'ab_w_out': ab_w_out,
        'cd_w_in': cd_w_in, 'ssd_conv_w': ssd_conv_w, 'ssd_conv_b': ssd_conv_b, 'ssd_a_log': ssd_a_log,
        'ssd_dt_bias': ssd_dt_bias, 'ssd_d': ssd_d, 'ssd_norm': ssd_norm,
        'lru_conv_w': lru_conv_w, 'lru_conv_b': lru_conv_b, 'lru_wa': lru_wa, 'lru_ba': lru_ba,
        'lru_wx': lru_wx, 'lru_bx': lru_bx, 'lru_lambda': lru_lambda, 'cd_w_out': cd_w_out,
        'ple_w_proj': ple_w_proj, 'ple_norm': ple_norm, 'ple_w_gate': ple_w_gate,
    }
    P = _prep_weights(w)
    n_even, n_odd = ab_w_in.shape[0], cd_w_in.shape[0]

    Bp, Lp, _ = x_prompt.shape
    assert Lp >= CONV_W - 1 and Lp % 64 == 0
    zero_states = (
        jnp.zeros((n_even, Bp, DN_HEADS, DN_DK, DN_DV), F32),
        jnp.zeros((n_even, Bp, CONV_W - 1, DN_QKV), F32),
        jnp.zeros((n_even, Bp, GLA_HEADS, GLA_DK, GLA_DV), F32),
        jnp.zeros((n_odd, Bp, SSD_HEADS, SSD_STATE, SSD_HEADDIM), F32),
        jnp.zeros((n_odd, Bp, CONV_W - 1, SSD_XBC), F32),
        jnp.zeros((n_odd, Bp, LRU_W), F32),
        jnp.zeros((n_odd, Bp, CONV_W - 1, LRU_W), F32),
    )
    y_p, st_p = _trunk(x_prompt, p_prompt, zero_states, P, lvalid=Lp, bb=1, tb=_pick_tb(Lp),
                       c_dn=64, c_gla=16, c_ssd=64)

    Bs, Ls, _ = x_sample.shape
    assert CONV_W - 1 <= Ls <= SUBLANES
    pad = SUBLANES - Ls
    xs = jnp.pad(x_sample, ((0, 0), (0, pad), (0, 0)))
    ps = jnp.pad(p_sample, ((0, 0), (0, 0), (0, pad), (0, 0)))
    sample_states = (state_dn, state_dn_conv, state_gla, state_ssd, state_ssd_conv, state_lru, state_lru_conv)
    bb_s = 8 if Bs % 8 == 0 else 1
    y_s, st_s = _trunk(xs, ps, sample_states, P, lvalid=Ls, bb=bb_s, tb=SUBLANES,
                       c_dn=SUBLANES, c_gla=SUBLANES, c_ssd=SUBLANES)
    return (y_p, y_s[:, :Ls], *st_p, *st_s)
```

```python
import functools

import jax
import jax.numpy as jnp
from jax import lax
from jax.experimental import pallas as pl
from jax.experimental.pallas import tpu as pltpu

F32 = jnp.float32
BF16 = jnp.bfloat16
HI = lax.Precision.HIGHEST

EPS = 1e-6
CONV_W = 4
PLE_DIM = 256
DN_HEADS, DN_DK, DN_DV = 8, 128, 128
DN_W = DN_HEADS * DN_DV
DN_QKV = 2 * DN_HEADS * DN_DK + DN_W
GLA_HEADS, GLA_DK, GLA_DV = 4, 128, 256
GLA_KW = GLA_HEADS * GLA_DK
GLA_W = GLA_HEADS * GLA_DV
GLA_RANK = 16
GLA_GATE_NORM = 16.0
AB_SIZES = (DN_QKV, DN_HEADS, DN_HEADS, DN_W, GLA_KW, GLA_KW, GLA_W, GLA_RANK, GLA_W)
SSD_HEADS, SSD_HEADDIM, SSD_GROUPS, SSD_STATE = 16, 64, 4, 64
SSD_W = SSD_HEADS * SSD_HEADDIM
SSD_XBC = SSD_W + 2 * SSD_GROUPS * SSD_STATE
LRU_W, LRU_BLOCKS = 1024, 16
LRU_BS = LRU_W // LRU_BLOCKS
LRU_C = 8.0
CD_SIZES = (SSD_W, SSD_XBC, SSD_HEADS, LRU_W, LRU_W)

LANES = 128
SUBLANES = 8
VMEM_LIMIT = 52 * 1024 * 1024

AB_QKV, AB_ZDN, AB_GQ, AB_GK, AB_GV, AB_ZGLA = 0, 3072, 4096, 4608, 5120, 6144
AB_MAIN = 7168
CD_Z, CD_ZLRU, CD_XLRU, CD_XBC = 0, 1024, 2048, 3072
CD_MAIN = 4608


def _silu(x):
    return x * jax.nn.sigmoid(x)


def _nt(a, b):
    return lax.dot_general(a, b, (((1,), (1,)), ((), ())), preferred_element_type=F32)


def _tn(a, b):
    return lax.dot_general(a, b, (((0,), (0,)), ((), ())), preferred_element_type=F32)


def _mm(a, b):
    return jnp.dot(a, b, preferred_element_type=F32)


def _mm_hi(a, b):
    return jnp.dot(a, b, preferred_element_type=F32, precision=HI)


def _iota2(shape, dim):
    return lax.broadcasted_iota(jnp.int32, shape, dim)


def _conv_rows(halo, x, w, bias=None):
    cat = jnp.concatenate([halo, x], axis=0)
    out = x * w[3:4]
    for s in (1, 2, 3):
        out = out + pltpu.roll(cat, s, axis=0)[SUBLANES:] * w[3 - s:4 - s]
    if bias is not None:
        out = out + bias
    return out


def _params(**kw):
    return pltpu.CompilerParams(vmem_limit_bytes=VMEM_LIMIT, **kw)


def _inproj_kernel(h_ref, g_ref, wm_ref, ws_ref, um_ref, us_ref, xn_ref):
    @pl.when(pl.program_id(1) == 0)
    def _():
        x = h_ref[...]
        r = lax.rsqrt(jnp.mean(x * x, axis=-1, keepdims=True) + EPS)
        xn = (x * r * g_ref[...]).astype(BF16)
        xn_ref[...] = xn
        us_ref[...] = _mm(xn, ws_ref[...])

    um_ref[...] = _mm(xn_ref[...], wm_ref[...])


def _inproj(h, g, wm, ws):
    T, D = h.shape
    Nm = wm.shape[1]
    tm = min(T, 1024)
    tn = 512
    return pl.pallas_call(
        _inproj_kernel,
        grid=(T // tm, Nm // tn),
        in_specs=[pl.BlockSpec((tm, D), lambda i, j: (i, 0)),
                  pl.BlockSpec((1, D), lambda i, j: (0, 0)),
                  pl.BlockSpec((D, tn), lambda i, j: (0, j)),
                  pl.BlockSpec((D, LANES), lambda i, j: (0, 0))],
        out_specs=[pl.BlockSpec((tm, tn), lambda i, j: (i, j)),
                   pl.BlockSpec((tm, LANES), lambda i, j: (i, 0))],
        out_shape=[jax.ShapeDtypeStruct((T, Nm), F32), jax.ShapeDtypeStruct((T, LANES), F32)],
        scratch_shapes=[pltpu.VMEM((tm, D), BF16)],
        compiler_params=_params(dimension_semantics=("parallel", "arbitrary")),
        name="inproj",
    )(h, g, wm, ws)


def _outproj_kernel(oa_ref, ob_ref, h_ref, p_ref, wa_ref, wb_ref, pg_ref, wg_ref, wp_ref, fg_ref, out_ref, *, final):
    h1 = h_ref[...] + _mm(oa_ref[...], wa_ref[...]) + _mm(ob_ref[...], wb_ref[...])
    r = lax.rsqrt(jnp.mean(h1 * h1, axis=-1, keepdims=True) + EPS)
    hn = (h1 * r * pg_ref[...]).astype(BF16)
    gate = jax.nn.sigmoid(_mm(hn, wg_ref[...]))
    pp = _mm(p_ref[...].astype(BF16), wp_ref[...])
    h2 = h1 + gate * pp
    if final:
        r2 = lax.rsqrt(jnp.mean(h2 * h2, axis=-1, keepdims=True) + EPS)
        h2 = h2 * r2 * fg_ref[...]
    out_ref[...] = h2


def _outproj(oa, ob, h, p_all, li, wa, wb, pg, wg, wp, fg, final):
    T, D = h.shape
    Wa, Wb = oa.shape[1], ob.shape[1]
    tm = min(T, 256)
    const = lambda i: (0, 0)
    return pl.pallas_call(
        functools.partial(_outproj_kernel, final=final),
        grid=(T // tm,),
        in_specs=[pl.BlockSpec((tm, Wa), lambda i: (i, 0)),
                  pl.BlockSpec((tm, Wb), lambda i: (i, 0)),
                  pl.BlockSpec((tm, D), lambda i: (i, 0)),
                  pl.BlockSpec((None, tm, PLE_DIM), lambda i: (li, i, 0)),
                  pl.BlockSpec((Wa, D), const),
                  pl.BlockSpec((Wb, D), const),
                  pl.BlockSpec((1, D), const),
                  pl.BlockSpec((D, D), const),
                  pl.BlockSpec((PLE_DIM, D), const),
                  pl.BlockSpec((1, D), const)],
        out_specs=pl.BlockSpec((tm, D), lambda i: (i, 0)),
        out_shape=jax.ShapeDtypeStruct((T, D), F32),
        compiler_params=_params(dimension_semantics=("parallel",)),
        name="outproj",
    )(oa, ob, h, p_all, wa, wb, pg, wg, wp, fg)


R_BD = 256


def _bd_masks(C):
    ii = _iota2((R_BD, R_BD), 0)
    jj = _iota2((R_BD, R_BD), 1)
    sh = C.bit_length() - 1
    lower = ((ii >> sh) == (jj >> sh)) & (jj <= ii)
    neg = jnp.where(lower, 0.0, -1e30)
    eye = (ii == jj).astype(F32)
    return neg, 1.0 - eye, eye


def _dn_groups(Q, K, V, beta, gc, gr, neg, offdiag, eye, C):
    n = range(len(Q))
    Kb = [K[g].astype(BF16) for g in n]
    KK = [_nt(Kb[g], Kb[g]) for g in n]
    QK = [_nt(Q[g].astype(BF16), Kb[g]) for g in n]
    dm = [jnp.exp((gc[g] - gr[g]) + neg) for g in n]
    A = [(beta[g] * KK[g]) * (dm[g] * offdiag) for g in n]
    T = [eye - A[g] for g in n]
    X = A
    p = 2
    while p < C:
        Xb = [X[g].astype(BF16) for g in n]
        X = [_mm(Xb[g], Xb[g]) for g in n]
        T = [T[g] + _mm(T[g].astype(BF16), X[g].astype(BF16)) for g in n]
        p *= 2
    eg = [jnp.exp(gc[g]) for g in n]
    rhs = [jnp.concatenate([beta[g] * V[g], (beta[g] * eg[g]) * K[g]], axis=1).astype(BF16) for g in n]
    sol = [_mm(T[g].astype(BF16), rhs[g]) for g in n]
    return ([sol[g][:, :DN_DV] for g in n], [sol[g][:, DN_DV:] for g in n],
            [QK[g] * dm[g] for g in n], [Q[g] * eg[g] for g in n])


def _l2n(x):
    return x * lax.rsqrt(jnp.sum(x * x, axis=-1, keepdims=True) + EPS)


def _dn_apply(S, u0, wk, qkd, qg, K, gl, gc, C):
    n = range(len(S))
    nu = len(S[0])
    blk = lambda x, i: x[i * C:(i + 1) * C]
    r = [[None] * nu for _ in n]
    for i in range(nu):
        for g in n:
            lhs = jnp.concatenate([blk(wk[g], i), blk(qg[g], i)], axis=0).astype(BF16)
            r[g][i] = _mm(lhs, S[g][i].astype(BF16))
    Ub = [jnp.concatenate([blk(u0[g], i) - r[g][i][:C] for i in range(nu)], axis=0).astype(BF16) for g in n]
    O = [jnp.concatenate([r[g][i][C:] for i in range(nu)], axis=0) + _mm(qkd[g].astype(BF16), Ub[g]) for g in n]
    kend = [(K[g] * jnp.exp(gl[g] - gc[g])).astype(BF16) for g in n]
    egl = [jnp.exp(gl[g]) for g in n]
    S_new = [[None] * nu for _ in n]
    for i in range(nu):
        for g in n:
            S_new[g][i] = egl[g][i * C:i * C + 1, :] * S[g][i] + _tn(blk(kend[g], i), blk(Ub[g], i))
    return O, S_new


def _dn_prompt_kernel(qkv_ref, z_ref, us_ref, cst_ref, st_ref, cw_ref, prm_ref, nrm_ref, o_ref, so_ref, halo_ref,
                      *, tb, C, nc):
    t = pl.program_id(1)
    hg = R_BD // C
    CR = nc * C

    @pl.when(t == 0)
    def _():
        so_ref[...] = st_ref[...]
        halo_ref[0:5, :] = jnp.zeros((5, DN_QKV), F32)
        halo_ref[5:8, :] = cst_ref[...]

    neg, offdiag, eye = _bd_masks(C)
    ii = _iota2((CR, CR), 0)
    jj = _iota2((CR, CR), 1)
    sh = C.bit_length() - 1
    tril = (((ii >> sh) == (jj >> sh)) & (jj <= ii)).astype(F32)
    lane = _iota2((1, LANES), 1)
    gmask = (lane >= DN_HEADS) & (lane < 2 * DN_HEADS)
    neg_a = -jnp.exp(prm_ref[0:1, :])
    bias = prm_ref[1:2, :]
    cw = cw_ref[...]
    nrm = nrm_ref[...]
    head_sets = [range(gi * hg, (gi + 1) * hg) for gi in range(DN_HEADS // hg)]
    groups = [(ci, hs) for ci in range(nc) for hs in head_sets]
    rws = lambda ci: slice(ci * C, (ci + 1) * C)

    def step(cp, carry):
        r0 = pl.multiple_of(cp * CR, CR)
        x = qkv_ref[pl.ds(r0, CR), :]
        prev = qkv_ref[pl.ds(pl.multiple_of(jnp.maximum(r0 - SUBLANES, 0), SUBLANES), SUBLANES), :]
        halo = jnp.where(cp == 0, halo_ref[...], prev)
        act = _silu(_conv_rows(halo, x, cw))
        us = us_ref[pl.ds(r0, CR), :]
        g = jnp.where(gmask, neg_a * jax.nn.softplus(us + bias), 0.0)
        beta_all = jax.nn.sigmoid(us)
        G = _mm_hi(tril, g)
        GT = G.T
        z = z_ref[pl.ds(r0, CR), :]
        cat0 = lambda f: [jnp.concatenate([f(ci, h) for h in hs], axis=0) for ci, hs in groups]
        Q = [_l2n(a) * (DN_DK ** -0.5) for a in cat0(lambda ci, h: act[rws(ci), h * DN_DK:(h + 1) * DN_DK])]
        K = [_l2n(a) for a in cat0(lambda ci, h: act[rws(ci), DN_W + h * DN_DK:DN_W + (h + 1) * DN_DK])]
        V = cat0(lambda ci, h: act[rws(ci), 2 * DN_W + h * DN_DV:2 * DN_W + (h + 1) * DN_DV])
        beta = cat0(lambda ci, h: beta_all[rws(ci), h:h + 1])
        gc = cat0(lambda ci, h: G[rws(ci), DN_HEADS + h:DN_HEADS + h + 1])
        gl = cat0(lambda ci, h: jnp.broadcast_to(
            G[(ci + 1) * C - 1:(ci + 1) * C, DN_HEADS + h:DN_HEADS + h + 1], (C, 1)))
        gr = [jnp.concatenate([GT[DN_HEADS + h:DN_HEADS + h + 1, rws(ci)] for h in hs], axis=1)
              for ci, hs in groups]
        u0, wk, qkd, qg = _dn_groups(Q, K, V, beta, gc, gr, neg, offdiag, eye, C)
        Z = cat0(lambda ci, h: z[rws(ci), h * DN_DV:(h + 1) * DN_DV])
        S = [[so_ref[h] for h in hs] for hs in head_sets]
        ng = len(head_sets)
        for ci in range(nc):
            pick = lambda xs: xs[ci * ng:(ci + 1) * ng]
            O, S = _dn_apply(S, pick(u0), pick(wk), pick(qkd), pick(qg), pick(K), pick(gl), pick(gc), C)
            for gi, hs in enumerate(head_sets):
                o = O[gi] * lax.rsqrt(jnp.mean(O[gi] * O[gi], axis=-1, keepdims=True) + EPS) * nrm
                out = (o * _silu(Z[ci * ng + gi])).astype(o_ref.dtype)
                for i, h in enumerate(hs):
                    o_ref[pl.ds(r0 + ci * C, C), h * DN_DV:(h + 1) * DN_DV] = out[i * C:(i + 1) * C]
        for gi, hs in enumerate(head_sets):
            for i, h in enumerate(hs):
                so_ref[h] = S[gi][i]
        return carry

    lax.fori_loop(0, tb // CR, step, 0)
    halo_ref[...] = qkv_ref[tb - SUBLANES:tb, :]


def _dn_sample_kernel(qkv_ref, z_ref, us_ref, cst_ref, st_ref, cw_ref, prm_ref, nrm_ref, o_ref, so_ref, halo_ref,
                      *, bb, lvalid):
    C = SUBLANES
    nb = R_BD // (DN_HEADS * C)
    rows_all = bb * C

    halo_ref[:, 0:5, :] = jnp.zeros((bb, 5, DN_QKV), F32)
    halo_ref[:, 5:8, :] = cst_ref[...]
    cw = cw_ref[...]
    nrm = nrm_ref[...]
    acts = [_silu(_conv_rows(halo_ref[b], qkv_ref[b], cw)) for b in range(bb)]

    neg, offdiag, eye = _bd_masks(C)
    lane = _iota2((1, LANES), 1)
    gmask = (lane >= DN_HEADS) & (lane < 2 * DN_HEADS)
    neg_a = -jnp.exp(prm_ref[0:1, :])
    bias = prm_ref[1:2, :]
    us = us_ref[...].reshape(rows_all, LANES)
    valid = (_iota2((rows_all, 1), 0) & (C - 1)) < lvalid
    g = jnp.where(gmask & valid, neg_a * jax.nn.softplus(us + bias), 0.0)
    beta_all = jnp.where(valid, jax.nn.sigmoid(us), 0.0)
    ii = _iota2((rows_all, rows_all), 0)
    jj = _iota2((rows_all, rows_all), 1)
    same = (ii >> 3) == (jj >> 3)
    G_all = _mm_hi((same & (jj <= ii)).astype(F32), g)
    GL_all = _mm_hi(same.astype(F32), g)
    GT_all = G_all.T

    groups = [[(h, gi * nb + bl) for h in range(DN_HEADS) for bl in range(nb)]
              for gi in range(bb // nb)]
    rows = [slice(gi * nb * C, (gi + 1) * nb * C) for gi in range(bb // nb)]
    cat0 = lambda f: [jnp.concatenate([f(h, b) for h, b in us_], axis=0) for us_ in groups]
    cath = lambda f: [jnp.concatenate([f(h, rs) for h in range(DN_HEADS)], axis=0) for rs in rows]
    Q = [_l2n(x) * (DN_DK ** -0.5) for x in cat0(lambda h, b: acts[b][:, h * DN_DK:(h + 1) * DN_DK])]
    K = [_l2n(x) for x in cat0(lambda h, b: acts[b][:, DN_W + h * DN_DK:DN_W + (h + 1) * DN_DK])]
    V = cat0(lambda h, b: acts[b][:, 2 * DN_W + h * DN_DV:2 * DN_W + (h + 1) * DN_DV])
    beta = cath(lambda h, rs: beta_all[rs, h:h + 1])
    gc = cath(lambda h, rs: G_all[rs, DN_HEADS + h:DN_HEADS + h + 1])
    gl = cath(lambda h, rs: GL_all[rs, DN_HEADS + h:DN_HEADS + h + 1])
    gr = [jnp.concatenate([GT_all[DN_HEADS + h:DN_HEADS + h + 1, rs] for h in range(DN_HEADS)], axis=1)
          for rs in rows]
    u0, wk, qkd, qg = _dn_groups(Q, K, V, beta, gc, gr, neg, offdiag, eye, C)
    S = [[st_ref[b, h] for h, b in us_] for us_ in groups]
    O, S_new = _dn_apply(S, u0, wk, qkd, qg, K, gl, gc, C)
    Z = cat0(lambda h, b: z_ref[b][:, h * DN_DV:(h + 1) * DN_DV])
    for g, us_ in enumerate(groups):
        o = O[g] * lax.rsqrt(jnp.mean(O[g] * O[g], axis=-1, keepdims=True) + EPS) * nrm
        out = (o * _silu(Z[g])).astype(o_ref.dtype)
        for i, (h, b) in enumerate(us_):
            so_ref[b, h] = S_new[g][i]
            o_ref[b, :, h * DN_DV:(h + 1) * DN_DV] = out[i * C:(i + 1) * C]


def _dn_mixer(um, us, cst_all, st_all, j, cw, prm, nrm, *, bb, tb, C, lvalid):
    B, L, _ = um.shape
    const = lambda *_: (0, 0)
    out_shape = [jax.ShapeDtypeStruct((B, L, DN_W), BF16), jax.ShapeDtypeStruct(st_all.shape, F32)]
    if lvalid == L:
        assert bb == 1
        kern = functools.partial(_dn_prompt_kernel, tb=tb, C=C, nc=2)
        grid = (B, L // tb)
        in_specs = [pl.BlockSpec((None, tb, DN_QKV), lambda b, t: (b, t, AB_QKV // DN_QKV)),
                    pl.BlockSpec((None, tb, DN_W), lambda b, t: (b, t, AB_ZDN // DN_W)),
                    pl.BlockSpec((None, tb, LANES), lambda b, t: (b, t, 0)),
                    pl.BlockSpec((None, None, CONV_W - 1, DN_QKV), lambda b, t: (j, b, 0, 0)),
                    pl.BlockSpec((None, None, DN_HEADS, DN_DK, DN_DV), lambda b, t: (j, b, 0, 0, 0)),
                    pl.BlockSpec((CONV_W, DN_QKV), const),
                    pl.BlockSpec((2, LANES), const),
                    pl.BlockSpec((1, DN_DV), const)]
        out_specs = [pl.BlockSpec((None, tb, DN_W), lambda b, t: (b, t, 0)),
                     pl.BlockSpec((None, None, DN_HEADS, DN_DK, DN_DV), lambda b, t: (j, b, 0, 0, 0))]
        scratch = [pltpu.VMEM((SUBLANES, DN_QKV), F32)]
        sem = ("parallel", "arbitrary")
    else:
        assert L == SUBLANES and C == SUBLANES
        kern = functools.partial(_dn_sample_kernel, bb=bb, lvalid=lvalid)
        grid = (B // bb,)
        in_specs = [pl.BlockSpec((bb, L, DN_QKV), lambda b: (b, 0, AB_QKV // DN_QKV)),
                    pl.BlockSpec((bb, L, DN_W), lambda b: (b, 0, AB_ZDN // DN_W)),
                    pl.BlockSpec((bb, L, LANES), lambda b: (b, 0, 0)),
                    pl.BlockSpec((None, bb, CONV_W - 1, DN_QKV), lambda b: (j, b, 0, 0)),
                    pl.BlockSpec((None, bb, DN_HEADS, DN_DK, DN_DV), lambda b: (j, b, 0, 0, 0)),
                    pl.BlockSpec((CONV_W, DN_QKV), const),
                    pl.BlockSpec((2, LANES), const),
                    pl.BlockSpec((1, DN_DV), const)]
        out_specs = [pl.BlockSpec((bb, L, DN_W), lambda b: (b, 0, 0)),
                     pl.BlockSpec((None, bb, DN_HEADS, DN_DK, DN_DV), lambda b: (j, b, 0, 0, 0))]
        scratch = [pltpu.VMEM((bb, SUBLANES, DN_QKV), F32)]
        sem = ("parallel",)
    return pl.pallas_call(
        kern, grid=grid, in_specs=in_specs, out_specs=out_specs, out_shape=out_shape,
        scratch_shapes=scratch, input_output_aliases={4: 1},
        compiler_params=_params(dimension_semantics=sem), name="dn_mixer",
    )(um, um, us, cst_all, st_all, cw, prm, nrm)


def _gla_kernel(q_ref, k_ref, v_ref, z_ref, us_ref, st_ref, wa_ref, ba_ref, nrm_ref, o_ref, so_ref,
                *, bb, tb, C, lvalid, ltotal):
    t = pl.program_id(1)

    @pl.when(t == 0)
    def _():
        so_ref[...] = st_ref[...]

    ii = _iota2((C, C), 0)
    jj = _iota2((C, C), 1)
    tril = (jj <= ii).astype(F32)
    wa = wa_ref[...]
    ba = ba_ref[...]
    nrm = nrm_ref[...]
    masked = lvalid < ltotal

    def chunk(bi, c):
        r0 = pl.multiple_of(c * C, C)
        q = q_ref[bi, pl.ds(r0, C), :] * (GLA_DK ** -0.5)
        k = k_ref[bi, pl.ds(r0, C), :]
        v = v_ref[bi, pl.ds(r0, C), :]
        z = z_ref[bi, pl.ds(r0, C), :]
        us = us_ref[bi, pl.ds(r0, C), :]
        glog = jax.nn.log_sigmoid(_mm(us.astype(BF16), wa) + ba) * (1.0 / GLA_GATE_NORM)
        if masked:
            valid = (t * tb + r0 + _iota2((C, 1), 0)) < lvalid
            glog = jnp.where(valid, glog, 0.0)
            k = jnp.where(valid, k, 0.0)
            v = jnp.where(valid, v, 0.0)
        G = _mm_hi(tril, glog)
        for h in range(GLA_HEADS):
            Gh = G[:, h * GLA_DK:(h + 1) * GLA_DK]
            qh = q[:, h * GLA_DK:(h + 1) * GLA_DK]
            kh = k[:, h * GLA_DK:(h + 1) * GLA_DK]
            vh = v[:, h * GLA_DV:(h + 1) * GLA_DV].astype(BF16)
            AT = jnp.zeros((C, C), F32)
            for i in range(C):
                e = jnp.exp(jnp.minimum(Gh[i:i + 1, :] - Gh, 0.0))
                col = jnp.sum(kh * e * qh[i:i + 1, :], axis=1, keepdims=True)
                AT = AT + jnp.where(jj == i, col, 0.0)
            AT = jnp.where(ii <= jj, AT, 0.0)
            S = so_ref[bi, h]
            o = _mm((qh * jnp.exp(Gh)).astype(BF16), S.astype(BF16)) + _tn(AT.astype(BF16), vh)
            glast = Gh[C - 1:C, :]
            kend = (kh * jnp.exp(glast - Gh)).astype(BF16)
            gend_col = jnp.exp(Gh.T[:, C - 1:C])
            so_ref[bi, h] = gend_col * S + _tn(kend, vh)
            o = o * lax.rsqrt(jnp.mean(o * o, axis=-1, keepdims=True) + EPS) * nrm
            o = o * _silu(z[:, h * GLA_DV:(h + 1) * GLA_DV])
            o_ref[bi, pl.ds(r0, C), h * GLA_DV:(h + 1) * GLA_DV] = o.astype(o_ref.dtype)

    def per_batch(bi, carry):
        def per_chunk(c, carry2):
            chunk(bi, c)
            return carry2
        lax.fori_loop(0, tb // C, per_chunk, 0)
        return carry

    lax.fori_loop(0, bb, per_batch, 0)


def _gla_mixer(um, us, st_all, j, wa, ba, nrm, *, bb, tb, C, lvalid):
    B, L, _ = um.shape
    kern = functools.partial(_gla_kernel, bb=bb, tb=tb, C=C, lvalid=lvalid, ltotal=L)
    const = lambda b, t: (0, 0)
    return pl.pallas_call(
        kern,
        grid=(B // bb, L // tb),
        in_specs=[pl.BlockSpec((bb, tb, GLA_KW), lambda b, t: (b, t, AB_GQ // GLA_KW)),
                  pl.BlockSpec((bb, tb, GLA_KW), lambda b, t: (b, t, AB_GK // GLA_KW)),
                  pl.BlockSpec((bb, tb, GLA_W), lambda b, t: (b, t, AB_GV // GLA_W)),
                  pl.BlockSpec((bb, tb, GLA_W), lambda b, t: (b, t, AB_ZGLA // GLA_W)),
                  pl.BlockSpec((bb, tb, LANES), lambda b, t: (b, t, 0)),
                  pl.BlockSpec((None, bb, GLA_HEADS, GLA_DK, GLA_DV), lambda b, t: (j, b, 0, 0, 0)),
                  pl.BlockSpec((LANES, GLA_KW), const),
                  pl.BlockSpec((1, GLA_KW), const),
                  pl.BlockSpec((1, GLA_DV), const)],
        out_specs=[pl.BlockSpec((bb, tb, GLA_W), lambda b, t: (b, t, 0)),
                   pl.BlockSpec((None, bb, GLA_HEADS, GLA_DK, GLA_DV), lambda b, t: (j, b, 0, 0, 0))],
        out_shape=[jax.ShapeDtypeStruct((B, L, GLA_W), BF16),
                   jax.ShapeDtypeStruct(st_all.shape, F32)],
        input_output_aliases={5: 1},
        compiler_params=_params(dimension_semantics=("parallel", "arbitrary")),
        name="gla_mixer",
    )(um, um, um, um, us, st_all, wa, ba, nrm)


SSD_GW = SSD_W // SSD_GROUPS
SSD_REP = SSD_HEADS // SSD_GROUPS
SSD_ROWS = 64


def _lane_expand(x, h0, width):
    return jnp.concatenate([jnp.broadcast_to(x[:, h0 + j:h0 + j + 1], (x.shape[0], width)) for j in range(SSD_REP)],
                           axis=1)


def _ssd_masks(C):
    ii = _iota2((SSD_ROWS, SSD_GW), 0)
    jj = _iota2((SSD_ROWS, SSD_GW), 1) & (SSD_ROWS - 1)
    sh = C.bit_length() - 1
    negc = jnp.where(((ii >> sh) == (jj >> sh)) & (jj <= ii), 0.0, -1e30)
    ri = _iota2((SSD_GW, SSD_GW), 0)
    ci = _iota2((SSD_GW, SSD_GW), 1)
    bd = ((ri >> 6) == (ci >> 6)).astype(F32)
    return negc, bd


def _ssd_intra(act, G, GT, dt, gi, negc, bd):
    h0 = gi * SSD_REP
    xs = act[:, gi * SSD_GW:(gi + 1) * SSD_GW]
    bm = act[:, SSD_W + gi * SSD_STATE:SSD_W + (gi + 1) * SSD_STATE].astype(BF16)
    cm = act[:, SSD_W + (SSD_GROUPS + gi) * SSD_STATE:SSD_W + (SSD_GROUPS + gi + 1) * SSD_STATE].astype(BF16)
    gcc = _lane_expand(G, h0, SSD_HEADDIM)
    grc = jnp.concatenate([GT[h0 + j:h0 + j + 1, :] for j in range(SSD_REP)], axis=1)
    vcat = xs * _lane_expand(dt, h0, SSD_HEADDIM)
    cbm = _nt(cm, bm)
    acat = (jnp.concatenate([cbm] * SSD_REP, axis=1) * jnp.exp((gcc - grc) + negc)).astype(BF16)
    vbd = (jnp.concatenate([vcat] * SSD_REP, axis=0) * bd).astype(BF16)
    return _mm(acat, vbd), xs, vcat, gcc, bm, cm


def _ssd_finish(o, xs, z, dsk, nrm):
    y = (o + dsk * xs) * _silu(z)
    return y * lax.rsqrt(jnp.mean(y * y, axis=-1, keepdims=True) + EPS) * nrm


def _ssd_prompt_kernel(z_ref, xbc_ref, us_ref, cst_ref, st_ref, cw_ref, cb_ref, prm_ref, nrm_ref, o_ref, so_ref,
                       halo_ref, scat_ref, *, tb):
    t = pl.program_id(1)
    C = SSD_ROWS
    cat_heads = lambda f, gi: jnp.concatenate([f(gi * SSD_REP + j) for j in range(SSD_REP)], axis=1)

    @pl.when(t == 0)
    def _():
        for gi in range(SSD_GROUPS):
            scat_ref[gi] = cat_heads(lambda h: st_ref[h], gi)
        halo_ref[0:5, :] = jnp.zeros((5, SSD_XBC), F32)
        halo_ref[5:8, :] = cst_ref[...]

    negc, bd = _ssd_masks(C)
    tril = (_iota2((C, C), 1) <= _iota2((C, C), 0)).astype(F32)
    lane = _iota2((1, LANES), 1)
    hmask = lane < SSD_HEADS
    neg_a = -jnp.exp(prm_ref[0:1, :])
    bias = prm_ref[1:2, :]
    dsk = [_lane_expand(prm_ref[2:3, :], gi * SSD_REP, SSD_HEADDIM) for gi in range(SSD_GROUPS)]
    cw = cw_ref[...]
    cb = cb_ref[...]
    nrm = nrm_ref[...]
    groups = range(SSD_GROUPS)

    def chunk(c, carry):
        r0 = pl.multiple_of(c * C, C)
        x = xbc_ref[pl.ds(r0, C), :]
        prev = xbc_ref[pl.ds(pl.multiple_of(jnp.maximum(r0 - SUBLANES, 0), SUBLANES), SUBLANES), :]
        halo = jnp.where(c == 0, halo_ref[...], prev)
        act = _silu(_conv_rows(halo, x, cw, cb))
        us = us_ref[pl.ds(r0, C), :]
        dt = jnp.where(hmask, jax.nn.softplus(us + bias), 0.0)
        G = _mm_hi(tril, neg_a * dt)
        GT = G.T
        z = z_ref[pl.ds(r0, C), :]
        parts = [_ssd_intra(act, G, GT, dt, gi, negc, bd) for gi in groups]
        S = [scat_ref[gi] for gi in groups]
        inter = [_mm(parts[gi][5], S[gi].astype(BF16)) for gi in groups]
        for gi in groups:
            o_in, xs, vcat, gcc, bm, cm = parts[gi]
            glc = gcc[C - 1:C, :]
            w = (vcat * jnp.exp(glc - gcc)).astype(BF16)
            scat_ref[gi] = jnp.exp(glc) * S[gi] + _tn(bm, w)
            y = _ssd_finish(jnp.exp(gcc) * inter[gi] + o_in, xs, z[:, gi * SSD_GW:(gi + 1) * SSD_GW], dsk[gi],
                            nrm[:, gi * SSD_GW:(gi + 1) * SSD_GW])
            o_ref[pl.ds(r0, C), gi * SSD_GW:(gi + 1) * SSD_GW] = y.astype(o_ref.dtype)
        return carry

    lax.fori_loop(0, tb // C, chunk, 0)
    halo_ref[...] = xbc_ref[tb - SUBLANES:tb, :]

    @pl.when(t == pl.num_programs(1) - 1)
    def _():
        for gi in groups:
            s = scat_ref[gi]
            for j in range(SSD_REP):
                so_ref[gi * SSD_REP + j] = s[:, j * SSD_HEADDIM:(j + 1) * SSD_HEADDIM]


def _ssd_sample_kernel(z_ref, xbc_ref, us_ref, cst_ref, st_ref, cw_ref, cb_ref, prm_ref, nrm_ref, o_ref, so_ref,
                       halo_ref, *, bb, lvalid):
    C = SUBLANES
    assert bb * C == SSD_ROWS
    halo_ref[:, 0:5, :] = jnp.zeros((bb, 5, SSD_XBC), F32)
    halo_ref[:, 5:8, :] = cst_ref[...]
    cw = cw_ref[...]
    cb = cb_ref[...]
    nrm = nrm_ref[...]
    act = jnp.concatenate([_silu(_conv_rows(halo_ref[b], xbc_ref[b], cw, cb)) for b in range(bb)], axis=0)

    negc, bd = _ssd_masks(C)
    lane = _iota2((1, LANES), 1)
    neg_a = -jnp.exp(prm_ref[0:1, :])
    bias = prm_ref[1:2, :]
    us = us_ref[...].reshape(SSD_ROWS, LANES)
    valid = (_iota2((SSD_ROWS, 1), 0) & (C - 1)) < lvalid
    dt = jnp.where((lane < SSD_HEADS) & valid, jax.nn.softplus(us + bias), 0.0)
    ii = _iota2((SSD_ROWS, SSD_ROWS), 0)
    jj = _iota2((SSD_ROWS, SSD_ROWS), 1)
    same = (ii >> 3) == (jj >> 3)
    g = neg_a * dt
    G = _mm_hi((same & (jj <= ii)).astype(F32), g)
    GL = _mm_hi(same.astype(F32), g)
    GT = G.T
    z = z_ref[...].reshape(SSD_ROWS, SSD_W)
    groups = range(SSD_GROUPS)
    parts = [_ssd_intra(act, G, GT, dt, gi, negc, bd) for gi in groups]
    outs = []
    for gi in groups:
        o_in, xs, vcat, gcc, bm, cm = parts[gi]
        glc = _lane_expand(GL, gi * SSD_REP, SSD_HEADDIM)
        w = (vcat * jnp.exp(glc - gcc)).astype(BF16)
        egl = jnp.exp(glc)
        inter = []
        for b in range(bb):
            rows = slice(b * C, (b + 1) * C)
            S = jnp.concatenate([st_ref[b, gi * SSD_REP + j] for j in range(SSD_REP)], axis=1)
            inter.append(_mm(cm[rows], S.astype(BF16)))
            s_new = egl[b * C:b * C + 1, :] * S + _tn(bm[rows], w[rows])
            for j in range(SSD_REP):
                so_ref[b, gi * SSD_REP + j] = s_new[:, j * SSD_HEADDIM:(j + 1) * SSD_HEADDIM]
        o = jnp.exp(gcc) * jnp.concatenate(inter, axis=0) + o_in
        dsk = _lane_expand(prm_ref[2:3, :], gi * SSD_REP, SSD_HEADDIM)
        outs.append(_ssd_finish(o, xs, z[:, gi * SSD_GW:(gi + 1) * SSD_GW], dsk, nrm[:, gi * SSD_GW:(gi + 1) * SSD_GW]))
    o_ref[...] = jnp.concatenate(outs, axis=1).astype(o_ref.dtype).reshape(bb, C, SSD_W)


def _ssd_mixer(um, us, cst_all, st_all, j, cw, cb, prm, nrm, *, bb, tb, lvalid):
    B, L, _ = um.shape
    const = lambda *_: (0, 0)
    out_shape = [jax.ShapeDtypeStruct((B, L, SSD_W), BF16), jax.ShapeDtypeStruct(st_all.shape, F32)]
    st_blk = (SSD_HEADS, SSD_STATE, SSD_HEADDIM)
    w_specs = [pl.BlockSpec((CONV_W, SSD_XBC), const), pl.BlockSpec((1, SSD_XBC), const),
               pl.BlockSpec((3, LANES), const), pl.BlockSpec((1, SSD_W), const)]
    if lvalid == L:
        assert bb == 1 and tb % SSD_ROWS == 0
        kern = functools.partial(_ssd_prompt_kernel, tb=tb)
        grid = (B, L // tb)
        in_specs = [pl.BlockSpec((None, tb, SSD_W), lambda b, t: (b, t, CD_Z // SSD_W)),
                    pl.BlockSpec((None, tb, SSD_XBC), lambda b, t: (b, t, CD_XBC // SSD_XBC)),
                    pl.BlockSpec((None, tb, LANES), lambda b, t: (b, t, 0)),
                    pl.BlockSpec((None, None, CONV_W - 1, SSD_XBC), lambda b, t: (j, b, 0, 0)),
                    pl.BlockSpec((None, None) + st_blk, lambda b, t: (j, b, 0, 0, 0))] + w_specs
        out_specs = [pl.BlockSpec((None, tb, SSD_W), lambda b, t: (b, t, 0)),
                     pl.BlockSpec((None, None) + st_blk, lambda b, t: (j, b, 0, 0, 0))]
        scratch = [pltpu.VMEM((SUBLANES, SSD_XBC), F32), pltpu.VMEM((SSD_GROUPS, SSD_STATE, SSD_GW), F32)]
        sem = ("parallel", "arbitrary")
    else:
        assert L == SUBLANES
        kern = functools.partial(_ssd_sample_kernel, bb=bb, lvalid=lvalid)
        grid = (B // bb,)
        in_specs = [pl.BlockSpec((bb, L, SSD_W), lambda b: (b, 0, CD_Z // SSD_W)),
                    pl.BlockSpec((bb, L, SSD_XBC), lambda b: (b, 0, CD_XBC // SSD_XBC)),
                    pl.BlockSpec((bb, L, LANES), lambda b: (b, 0, 0)),
                    pl.BlockSpec((None, bb, CONV_W - 1, SSD_XBC), lambda b: (j, b, 0, 0)),
                    pl.BlockSpec((None, bb) + st_blk, lambda b: (j, b, 0, 0, 0))] + w_specs
        out_specs = [pl.BlockSpec((bb, L, SSD_W), lambda b: (b, 0, 0)),
                     pl.BlockSpec((None, bb) + st_blk, lambda b: (j, b, 0, 0, 0))]
        scratch = [pltpu.VMEM((bb, SUBLANES, SSD_XBC), F32)]
        sem = ("parallel",)
    return pl.pallas_call(
        kern, grid=grid, in_specs=in_specs, out_specs=out_specs, out_shape=out_shape,
        scratch_shapes=scratch, input_output_aliases={4: 1},
        compiler_params=_params(dimension_semantics=sem), name="ssd_mixer",
    )(um, um, us, cst_all, st_all, cw, cb, prm, nrm)


def _lru_kernel(z_ref, x_ref, cst_ref, st_ref, cw_ref, cb_ref, wbd_ref, prm_ref, o_ref, so_ref, halo_ref, carry_ref,
                *, bb, tb, lvalid, ltotal):
    t = pl.program_id(1)

    @pl.when(t == 0)
    def _():
        carry_ref[...] = st_ref[...]
        halo_ref[:, 0:5, :] = jnp.zeros((bb, 5, LRU_W), F32)
        halo_ref[:, 5:8, :] = cst_ref[...]

    cw = cw_ref[...]
    cb = cb_ref[...]
    b_a = prm_ref[0:1, :]
    b_x = prm_ref[1:2, :]
    sp = jax.nn.softplus(-prm_ref[2:3, :])
    row = _iota2((tb, 1), 0)
    last_t, last_r = (lvalid - 1) // tb, (lvalid - 1) % tb

    def per_batch(bi, carry):
        x = x_ref[bi]
        xc = _conv_rows(halo_ref[bi], x, cw, cb)
        halo_ref[bi] = x[tb - SUBLANES:tb, :]
        xb = xc.astype(BF16)
        rs, is_ = [], []
        for cg in range(LRU_W // LANES):
            ri = _mm(xb[:, cg * LANES:(cg + 1) * LANES], wbd_ref[cg])
            rs.append(ri[:, :LANES])
            is_.append(ri[:, LANES:])
        r = jax.nn.sigmoid(jnp.concatenate(rs, axis=1) + b_a)
        i = jax.nn.sigmoid(jnp.concatenate(is_, axis=1) + b_x)
        log_a = (-LRU_C) * r * sp
        a = jnp.exp(log_a)
        b = jnp.sqrt(1.0 - jnp.exp(2.0 * log_a)) * (i * xc)
        d = 1
        while d < tb:
            m = row >= d
            a_s = pltpu.roll(a, d, axis=0)
            b_s = pltpu.roll(b, d, axis=0)
            b = jnp.where(m, a * b_s + b, b)
            a = jnp.where(m, a * a_s, a)
            d *= 2
        hs = a * carry_ref[bi] + b
        carry_ref[bi] = hs[tb - 1:tb, :]

        @pl.when(t == last_t)
        def _():
            so_ref[bi] = hs[last_r:last_r + 1, :]

        o_ref[bi] = (hs * _silu(z_ref[bi])).astype(o_ref.dtype)
        return carry

    lax.fori_loop(0, bb, per_batch, 0)


def _lru_mixer(um, cst_all, st_all, j, cw, cb, wbd, prm, *, bb, tb, lvalid):
    B, L, _ = um.shape
    kern = functools.partial(_lru_kernel, bb=bb, tb=tb, lvalid=lvalid, ltotal=L)
    const = lambda b, t: (0, 0)
    return pl.pallas_call(
        kern,
        grid=(B // bb, L // tb),
        in_specs=[pl.BlockSpec((bb, tb, LRU_W), lambda b, t: (b, t, CD_ZLRU // LRU_W)),
                  pl.BlockSpec((bb, tb, LRU_W), lambda b, t: (b, t, CD_XLRU // LRU_W)),
                  pl.BlockSpec((None, bb, CONV_W - 1, LRU_W), lambda b, t: (j, b, 0, 0)),
                  pl.BlockSpec((None, bb, 1, LRU_W), lambda b, t: (j, b, 0, 0)),
                  pl.BlockSpec((CONV_W, LRU_W), const),
                  pl.BlockSpec((1, LRU_W), const),
                  pl.BlockSpec((LRU_W // LANES, LANES, 2 * LANES), lambda b, t: (0, 0, 0)),
                  pl.BlockSpec((3, LRU_W), const)],
        out_specs=[pl.BlockSpec((bb, tb, LRU_W), lambda b, t: (b, t, 0)),
                   pl.BlockSpec((bb, 1, LRU_W), lambda b, t: (b, 0, 0))],
        out_shape=[jax.ShapeDtypeStruct((B, L, LRU_W), BF16),
                   jax.ShapeDtypeStruct((B, 1, LRU_W), F32)],
        scratch_shapes=[pltpu.VMEM((bb, SUBLANES, LRU_W), F32), pltpu.VMEM((bb, 1, LRU_W), F32)],
        compiler_params=_params(dimension_semantics=("parallel", "arbitrary")),
        name="lru_mixer",
    )(um, um, cst_all, st_all, cw, cb, wbd, prm)


def _split_cols(w, sizes):
    out, o = [], 0
    for s in sizes:
        out.append(w[:, o:o + s])
        o += s
    return out


def _lane_row(*pieces):
    v = jnp.concatenate([p.reshape(-1) for p in pieces])
    return jnp.pad(v, (0, LANES - v.shape[0])).reshape(1, LANES)


def _prep_weights(w):
    n_even = w['ab_w_in'].shape[0]
    n_odd = w['cd_w_in'].shape[0]
    D = w['ab_w_in'].shape[1]
    P = {'ab': [], 'cd': []}
    for j in range(n_even):
        qkv, beta, a, zdn, gq, gk, gv, glr, zgla = _split_cols(w['ab_w_in'][j], AB_SIZES)
        small = jnp.concatenate([beta, a, glr], axis=1)
        P['ab'].append(dict(
            w_main=jnp.concatenate([qkv, zdn, gq, gk, gv, zgla], axis=1).astype(BF16),
            w_small=jnp.pad(small, ((0, 0), (0, LANES - small.shape[1]))).astype(BF16),
            dn_cw=w['dn_conv_w'][j],
            dn_prm=jnp.concatenate([_lane_row(jnp.zeros((DN_HEADS,), F32), w['dn_a_log'][j]),
                                    _lane_row(jnp.zeros((DN_HEADS,), F32), w['dn_dt_bias'][j])], axis=0),
            dn_norm=w['dn_norm'][j].reshape(1, DN_DV),
            gla_wa=jnp.pad(w['gla_wa2'][j], ((2 * DN_HEADS, LANES - 2 * DN_HEADS - GLA_RANK), (0, 0))).astype(BF16),
            gla_ba=w['gla_ba'][j].reshape(1, GLA_KW),
            gla_norm=w['gla_norm'][j].reshape(1, GLA_DV),
            w_out_a=w['ab_w_out'][j][:DN_W].astype(BF16),
            w_out_b=w['ab_w_out'][j][DN_W:].astype(BF16),
        ))
    eye2 = jnp.eye(2, dtype=F32)
    for j in range(n_odd):
        z, xbc, dt, zlru, xlru = _split_cols(w['cd_w_in'][j], CD_SIZES)
        wa = w['lru_wa'][j].reshape(LRU_W // LANES, 2, LRU_BS, LRU_BS)
        wx = w['lru_wx'][j].reshape(LRU_W // LANES, 2, LRU_BS, LRU_BS)
        bd = lambda m: jnp.einsum('gbcd,be->gbced', m, eye2).reshape(LRU_W // LANES, LANES, LANES)
        P['cd'].append(dict(
            w_main=jnp.concatenate([z, zlru, xlru, xbc], axis=1).astype(BF16),
            w_small=jnp.pad(dt, ((0, 0), (0, LANES - dt.shape[1]))).astype(BF16),
            ssd_cw=w['ssd_conv_w'][j],
            ssd_cb=w['ssd_conv_b'][j].reshape(1, SSD_XBC),
            ssd_prm=jnp.concatenate([_lane_row(w['ssd_a_log'][j]), _lane_row(w['ssd_dt_bias'][j]),
                                     _lane_row(w['ssd_d'][j])], axis=0),
            ssd_norm=w['ssd_norm'][j].reshape(1, SSD_W),
            lru_cw=w['lru_conv_w'][j],
            lru_cb=w['lru_conv_b'][j].reshape(1, LRU_W),
            lru_wbd=jnp.concatenate([bd(wa), bd(wx)], axis=2).astype(BF16),
            lru_prm=jnp.stack([w['lru_ba'][j], w['lru_bx'][j], w['lru_lambda'][j]], axis=0),
            w_out_a=w['cd_w_out'][j][:SSD_W].astype(BF16),
            w_out_b=w['cd_w_out'][j][SSD_W:].astype(BF16),
        ))
    P['norm_g'] = w['norm_g'].reshape(-1, 1, D)
    P['ple_norm'] = w['ple_norm'].reshape(-1, 1, D)
    P['ple_w_gate'] = w['ple_w_gate'].astype(BF16)
    P['ple_w_proj'] = w['ple_w_proj'].astype(BF16)
    P['final_norm'] = w['final_norm'].reshape(1, D)
    return P


def _trunk(x, p, states, P, *, lvalid, bb, tb, c_dn, c_gla, c_ssd):
    B, L, D = x.shape
    depth = P['norm_g'].shape[0]
    st_dn, st_dnc, st_gla, st_ssd, st_ssdc, st_lru, st_lruc = states
    st_lru = st_lru.reshape(st_lru.shape[0], B, 1, LRU_W)
    dnc_new, ssdc_new, lru_new, lruc_new = [], [], [], []
    T = B * L
    h = x.reshape(T, D)
    p_all = p.reshape(depth, T, PLE_DIM)
    lo = lvalid - (CONV_W - 1)
    for li in range(depth):
        j = li // 2
        if li % 2 == 0:
            W = P['ab'][j]
            um, us = _inproj(h, P['norm_g'][li], W['w_main'], W['w_small'])
            um = um.reshape(B, L, AB_MAIN)
            us = us.reshape(B, L, LANES)
            oa, st_dn = _dn_mixer(um, us, st_dnc, st_dn, j, W['dn_cw'], W['dn_prm'], W['dn_norm'],
                                  bb=bb, tb=tb, C=c_dn, lvalid=lvalid)
            ob, st_gla = _gla_mixer(um, us, st_gla, j, W['gla_wa'], W['gla_ba'], W['gla_norm'],
                                    bb=bb, tb=tb, C=c_gla, lvalid=lvalid)
            dnc_new.append(um[:, lo:lvalid, AB_QKV:AB_QKV + DN_QKV])
        else:
            W = P['cd'][j]
            um, us = _inproj(h, P['norm_g'][li], W['w_main'], W['w_small'])
            um = um.reshape(B, L, CD_MAIN)
            us = us.reshape(B, L, LANES)
            oa, st_ssd = _ssd_mixer(um, us, st_ssdc, st_ssd, j, W['ssd_cw'], W['ssd_cb'], W['ssd_prm'],
                                    W['ssd_norm'], bb=bb, tb=tb, lvalid=lvalid)
            ob, lru_j = _lru_mixer(um, st_lruc, st_lru, j, W['lru_cw'], W['lru_cb'],
                                   W['lru_wbd'], W['lru_prm'], bb=bb, tb=tb, lvalid=lvalid)
            ssdc_new.append(um[:, lo:lvalid, CD_XBC:CD_XBC + SSD_XBC])
            lru_new.append(lru_j.reshape(B, LRU_W))
            lruc_new.append(um[:, lo:lvalid, CD_XLRU:CD_XLRU + LRU_W])
        h = _outproj(oa.reshape(T, -1), ob.reshape(T, -1), h, p_all, li, W['w_out_a'], W['w_out_b'],
                     P['ple_norm'][li], P['ple_w_gate'][li], P['ple_w_proj'][li], P['final_norm'],
                     final=(li == depth - 1))
    new_states = [st_dn, jnp.stack(dnc_new), st_gla, st_ssd, jnp.stack(ssdc_new), jnp.stack(lru_new),
                  jnp.stack(lruc_new)]
    return h.reshape(B, L, D), new_states


def _pick_tb(L):
    tb = min(L, 512)
    assert L % tb == 0
    return tb


def kernel(x_prompt, x_sample, state_dn, state_dn_conv, state_gla, state_ssd, state_ssd_conv, state_lru, state_lru_conv, p_prompt, p_sample, norm_g, final_norm, ab_w_in, dn_conv_w, dn_a_log, dn_dt_bias, dn_norm, gla_wa2, gla_ba, gla_norm, ab_w_out, cd_w_in, ssd_conv_w, ssd_conv_b, ssd_a_log, ssd_dt_bias, ssd_d, ssd_norm, lru_conv_w, lru_conv_b, lru_wa, lru_ba, lru_wx, lru_bx, lru_lambda, cd_w_out, ple_w_proj, ple_norm, ple_w_gate):
    w = {
        'norm_g': norm_g, 'final_norm': final_norm,
        'ab_w_in': ab_w_in, 'dn_conv_w': dn_conv_w, 'dn_a_log': dn_a_log, 'dn_dt_bias': dn_dt_bias,
        'dn_norm': dn_norm, 'gla_wa2': gla_wa2, 'gla_ba': gla_ba, 'gla_norm': gla_norm, 'ab_w_out': ab_w_out,
        'cd_w_in': cd_w_in, 'ssd_conv_w': ssd_conv_w, 'ssd_conv_b': ssd_conv_b, 'ssd_a_log': ssd_a_log,
        'ssd_dt_bias': ssd_dt_bias, 'ssd_d': ssd_d, 'ssd_norm': ssd_norm,
        'lru_conv_w': lru_conv_w, 'lru_conv_b': lru_conv_b, 'lru_wa': lru_wa, 'lru_ba': lru_ba,
        'lru_wx': lru_wx, 'lru_bx': lru_bx, 'lru_lambda': lru_lambda, 'cd_w_out': cd_w_out,
        'ple_w_proj': ple_w_proj, 'ple_norm': ple_norm, 'ple_w_gate': ple_w_gate,
    }
    P = _prep_weights(w)
    n_even, n_odd = ab_w_in.shape[0], cd_w_in.shape[0]

    Bp, Lp, _ = x_prompt.shape
    assert Lp >= CONV_W - 1 and Lp % 64 == 0
    zero_states = (
        jnp.zeros((n_even, Bp, DN_HEADS, DN_DK, DN_DV), F32),
        jnp.zeros((n_even, Bp, CONV_W - 1, DN_QKV), F32),
        jnp.zeros((n_even, Bp, GLA_HEADS, GLA_DK, GLA_DV), F32),
        jnp.zeros((n_odd, Bp, SSD_HEADS, SSD_STATE, SSD_HEADDIM), F32),
        jnp.zeros((n_odd, Bp, CONV_W - 1, SSD_XBC), F32),
        jnp.zeros((n_odd, Bp, LRU_W), F32),
        jnp.zeros((n_odd, Bp, CONV_W - 1, LRU_W), F32),
    )
    y_p, st_p = _trunk(x_prompt, p_prompt, zero_states, P, lvalid=Lp, bb=1, tb=_pick_tb(Lp),
                       c_dn=64, c_gla=16, c_ssd=64)

    Bs, Ls, _ = x_sample.shape
    assert CONV_W - 1 <= Ls <= SUBLANES
    pad = SUBLANES - Ls
    xs = jnp.pad(x_sample, ((0, 0), (0, pad), (0, 0)))
    ps = jnp.pad(p_sample, ((0, 0), (0, 0), (0, pad), (0, 0)))
    sample_states = (state_dn, state_dn_conv, state_gla, state_ssd, state_ssd_conv, state_lru, state_lru_conv)
    bb_s = 8 if Bs % 8 == 0 else 1
    y_s, st_s = _trunk(xs, ps, sample_states, P, lvalid=Ls, bb=bb_s, tb=SUBLANES,
                       c_dn=SUBLANES, c_gla=SUBLANES, c_ssd=SUBLANES)
    return (y_p, y_s[:, :Ls], *st_p, *st_s)
```

```python
import functools

import jax
import jax.numpy as jnp
from jax import lax
from jax.experimental import pallas as pl
from jax.experimental.pallas import tpu as pltpu

F32 = jnp.float32
BF16 = jnp.bfloat16
HI = lax.Precision.HIGHEST

EPS = 1e-6
CONV_W = 4
PLE_DIM = 256
DN_HEADS, DN_DK, DN_DV = 8, 128, 128
DN_W = DN_HEADS * DN_DV
DN_QKV = 2 * DN_HEADS * DN_DK + DN_W
GLA_HEADS, GLA_DK, GLA_DV = 4, 128, 256
GLA_KW = GLA_HEADS * GLA_DK
GLA_W = GLA_HEADS * GLA_DV
GLA_RANK = 16
GLA_GATE_NORM = 16.0
AB_SIZES = (DN_QKV, DN_HEADS, DN_HEADS, DN_W, GLA_KW, GLA_KW, GLA_W, GLA_RANK, GLA_W)
SSD_HEADS, SSD_HEADDIM, SSD_GROUPS, SSD_STATE = 16, 64, 4, 64
SSD_W = SSD_HEADS * SSD_HEADDIM
SSD_XBC = SSD_W + 2 * SSD_GROUPS * SSD_STATE
LRU_W, LRU_BLOCKS = 1024, 16
LRU_BS = LRU_W // LRU_BLOCKS
LRU_C = 8.0
CD_SIZES = (SSD_W, SSD_XBC, SSD_HEADS, LRU_W, LRU_W)

LANES = 128
SUBLANES = 8
VMEM_LIMIT = 52 * 1024 * 1024

AB_QKV, AB_ZDN, AB_GQ, AB_GK, AB_GV, AB_ZGLA = 0, 3072, 4096, 4608, 5120, 6144
AB_MAIN = 7168
CD_Z, CD_ZLRU, CD_XLRU, CD_XBC = 0, 1024, 2048, 3072
CD_MAIN = 4608


def _silu(x):
    return x * jax.nn.sigmoid(x)


def _nt(a, b):
    return lax.dot_general(a, b, (((1,), (1,)), ((), ())), preferred_element_type=F32)


def _tn(a, b):
    return lax.dot_general(a, b, (((0,), (0,)), ((), ())), preferred_element_type=F32)


def _mm(a, b):
    return jnp.dot(a, b, preferred_element_type=F32)


def _mm_hi(a, b):
    return jnp.dot(a, b, preferred_element_type=F32, precision=HI)


def _iota2(shape, dim):
    return lax.broadcasted_iota(jnp.int32, shape, dim)


def _conv_rows(halo, x, w, bias=None):
    cat = jnp.concatenate([halo, x], axis=0)
    out = x * w[3:4]
    for s in (1, 2, 3):
        out = out + pltpu.roll(cat, s, axis=0)[SUBLANES:] * w[3 - s:4 - s]
    if bias is not None:
        out = out + bias
    return out


def _params(**kw):
    return pltpu.CompilerParams(vmem_limit_bytes=VMEM_LIMIT, **kw)


def _inproj_kernel(h_ref, g_ref, wm_ref, ws_ref, um_ref, us_ref, xn_ref):
    @pl.when(pl.program_id(1) == 0)
    def _():
        x = h_ref[...]
        r = lax.rsqrt(jnp.mean(x * x, axis=-1, keepdims=True) + EPS)
        xn = (x * r * g_ref[...]).astype(BF16)
        xn_ref[...] = xn
        us_ref[...] = _mm(xn, ws_ref[...])

    um_ref[...] = _mm(xn_ref[...], wm_ref[...])


def _inproj(h, g, wm, ws):
    T, D = h.shape
    Nm = wm.shape[1]
    tm = min(T, 1024)
    tn = 1024 if Nm % 1024 == 0 else 1536
    return pl.pallas_call(
        _inproj_kernel,
        grid=(T // tm, Nm // tn),
        in_specs=[pl.BlockSpec((tm, D), lambda i, j: (i, 0)),
                  pl.BlockSpec((1, D), lambda i, j: (0, 0)),
                  pl.BlockSpec((D, tn), lambda i, j: (0, j)),
                  pl.BlockSpec((D, LANES), lambda i, j: (0, 0))],
        out_specs=[pl.BlockSpec((tm, tn), lambda i, j: (i, j)),
                   pl.BlockSpec((tm, LANES), lambda i, j: (i, 0))],
        out_shape=[jax.ShapeDtypeStruct((T, Nm), F32), jax.ShapeDtypeStruct((T, LANES), F32)],
        scratch_shapes=[pltpu.VMEM((tm, D), BF16)],
        compiler_params=_params(dimension_semantics=("parallel", "arbitrary")),
        name="inproj",
    )(h, g, wm, ws)


def _outproj_kernel(oa_ref, ob_ref, h_ref, p_ref, wa_ref, wb_ref, pg_ref, wg_ref, wp_ref, fg_ref, out_ref, *, final):
    h1 = h_ref[...] + _mm(oa_ref[...], wa_ref[...]) + _mm(ob_ref[...], wb_ref[...])
    r = lax.rsqrt(jnp.mean(h1 * h1, axis=-1, keepdims=True) + EPS)
    hn = (h1 * r * pg_ref[...]).astype(BF16)
    gate = jax.nn.sigmoid(_mm(hn, wg_ref[...]))
    pp = _mm(p_ref[...].astype(BF16), wp_ref[...])
    h2 = h1 + gate * pp
    if final:
        r2 = lax.rsqrt(jnp.mean(h2 * h2, axis=-1, keepdims=True) + EPS)
        h2 = h2 * r2 * fg_ref[...]
    out_ref[...] = h2


def _outproj(oa, ob, h, p_all, li, wa, wb, pg, wg, wp, fg, final):
    T, D = h.shape
    Wa, Wb = oa.shape[1], ob.shape[1]
    tm = min(T, 512)
    const = lambda i: (0, 0)
    resident = lambda shape: pl.BlockSpec(shape, const, pipeline_mode=pl.Buffered(1))
    return pl.pallas_call(
        functools.partial(_outproj_kernel, final=final),
        grid=(T // tm,),
        in_specs=[pl.BlockSpec((tm, Wa), lambda i: (i, 0)),
                  pl.BlockSpec((tm, Wb), lambda i: (i, 0)),
                  pl.BlockSpec((tm, D), lambda i: (i, 0)),
                  pl.BlockSpec((None, tm, PLE_DIM), lambda i: (li, i, 0)),
                  resident((Wa, D)),
                  resident((Wb, D)),
                  resident((1, D)),
                  resident((D, D)),
                  resident((PLE_DIM, D)),
                  resident((1, D))],
        out_specs=pl.BlockSpec((tm, D), lambda i: (i, 0)),
        out_shape=jax.ShapeDtypeStruct((T, D), F32),
        compiler_params=_params(dimension_semantics=("parallel",)),
        name="outproj",
    )(oa, ob, h, p_all, wa, wb, pg, wg, wp, fg)


R_BD = 128


def _bd_masks(C):
    ii = _iota2((R_BD, R_BD), 0)
    jj = _iota2((R_BD, R_BD), 1)
    sh = C.bit_length() - 1
    lower = ((ii >> sh) == (jj >> sh)) & (jj <= ii)
    neg = jnp.where(lower, 0.0, -1e30)
    eye = (ii == jj).astype(F32)
    return neg, 1.0 - eye, eye


def _dn_groups(Q, K, V, beta, gc, gr, neg, offdiag, eye, C):
    n = range(len(Q))
    Kb = [K[g].astype(BF16) for g in n]
    KK = [_nt(Kb[g], Kb[g]) for g in n]
    QK = [_nt(Q[g].astype(BF16), Kb[g]) for g in n]
    dm = [jnp.exp((gc[g] - gr[g]) + neg) for g in n]
    A = [(beta[g] * KK[g]) * (dm[g] * offdiag) for g in n]
    T = [eye - A[g] for g in n]
    X = A
    p = 2
    while p < C:
        Xb = [X[g].astype(BF16) for g in n]
        X = [_mm(Xb[g], Xb[g]) for g in n]
        T = [T[g] + _mm(T[g].astype(BF16), X[g].astype(BF16)) for g in n]
        p *= 2
    eg = [jnp.exp(gc[g]) for g in n]
    rhs = [jnp.concatenate([beta[g] * V[g], (beta[g] * eg[g]) * K[g]], axis=1).astype(BF16) for g in n]
    sol = [_mm(T[g].astype(BF16), rhs[g]) for g in n]
    return ([sol[g][:, :DN_DV] for g in n], [sol[g][:, DN_DV:] for g in n],
            [QK[g] * dm[g] for g in n], [Q[g] * eg[g] for g in n])


def _l2n(x):
    return x * lax.rsqrt(jnp.sum(x * x, axis=-1, keepdims=True) + EPS)


def _dn_apply(S, u0, wk, qkd, qg, K, gl, gc, C):
    n = range(len(S))
    nu = len(S[0])
    blk = lambda x, i: x[i * C:(i + 1) * C]
    r = [[None] * nu for _ in n]
    for i in range(nu):
        for g in n:
            lhs = jnp.concatenate([blk(wk[g], i), blk(qg[g], i)], axis=0).astype(BF16)
            r[g][i] = _mm(lhs, S[g][i].astype(BF16))
    Ub = [jnp.concatenate([blk(u0[g], i) - r[g][i][:C] for i in range(nu)], axis=0).astype(BF16) for g in n]
    O = [jnp.concatenate([r[g][i][C:] for i in range(nu)], axis=0) + _mm(qkd[g].astype(BF16), Ub[g]) for g in n]
    kend = [(K[g] * jnp.exp(gl[g] - gc[g])).astype(BF16) for g in n]
    egl = [jnp.exp(gl[g]) for g in n]
    S_new = [[None] * nu for _ in n]
    for i in range(nu):
        for g in n:
            S_new[g][i] = egl[g][i * C:i * C + 1, :] * S[g][i] + _tn(blk(kend[g], i), blk(Ub[g], i))
    return O, S_new


def _dn_prompt_kernel(qkv_ref, z_ref, us_ref, cst_ref, st_ref, cw_ref, prm_ref, nrm_ref, o_ref, so_ref, halo_ref,
                      *, tb, C, nc):
    t = pl.program_id(1)
    hg = R_BD // C
    CR = nc * C

    @pl.when(t == 0)
    def _():
        so_ref[...] = st_ref[...]
        halo_ref[0:5, :] = jnp.zeros((5, DN_QKV), F32)
        halo_ref[5:8, :] = cst_ref[...]

    neg, offdiag, eye = _bd_masks(C)
    ii = _iota2((CR, CR), 0)
    jj = _iota2((CR, CR), 1)
    sh = C.bit_length() - 1
    tril = (((ii >> sh) == (jj >> sh)) & (jj <= ii)).astype(F32)
    lane = _iota2((1, LANES), 1)
    gmask = (lane >= DN_HEADS) & (lane < 2 * DN_HEADS)
    neg_a = -jnp.exp(prm_ref[0:1, :])
    bias = prm_ref[1:2, :]
    cw = cw_ref[...]
    nrm = nrm_ref[...]
    head_sets = [range(gi * hg, (gi + 1) * hg) for gi in range(DN_HEADS // hg)]
    groups = [(ci, hs) for ci in range(nc) for hs in head_sets]
    rws = lambda ci: slice(ci * C, (ci + 1) * C)

    def step(cp, carry):
        r0 = pl.multiple_of(cp * CR, CR)
        x = qkv_ref[pl.ds(r0, CR), :]
        prev = qkv_ref[pl.ds(pl.multiple_of(jnp.maximum(r0 - SUBLANES, 0), SUBLANES), SUBLANES), :]
        halo = jnp.where(cp == 0, halo_ref[...], prev)
        act = _silu(_conv_rows(halo, x, cw))
        us = us_ref[pl.ds(r0, CR), :]
        g = jnp.where(gmask, neg_a * jax.nn.softplus(us + bias), 0.0)
        beta_all = jax.nn.sigmoid(us)
        G = _mm_hi(tril, g)
        GT = G.T
        z = z_ref[pl.ds(r0, CR), :]
        cat0 = lambda f: [jnp.concatenate([f(ci, h) for h in hs], axis=0) for ci, hs in groups]
        Q = [_l2n(a) * (DN_DK ** -0.5) for a in cat0(lambda ci, h: act[rws(ci), h * DN_DK:(h + 1) * DN_DK])]
        K = [_l2n(a) for a in cat0(lambda ci, h: act[rws(ci), DN_W + h * DN_DK:DN_W + (h + 1) * DN_DK])]
        V = cat0(lambda ci, h: act[rws(ci), 2 * DN_W + h * DN_DV:2 * DN_W + (h + 1) * DN_DV])
        beta = cat0(lambda ci, h: beta_all[rws(ci), h:h + 1])
        gc = cat0(lambda ci, h: G[rws(ci), DN_HEADS + h:DN_HEADS + h + 1])
        gl = cat0(lambda ci, h: jnp.broadcast_to(
            G[(ci + 1) * C - 1:(ci + 1) * C, DN_HEADS + h:DN_HEADS + h + 1], (C, 1)))
        gr = [jnp.concatenate([GT[DN_HEADS + h:DN_HEADS + h + 1, rws(ci)] for h in hs], axis=1)
              for ci, hs in groups]
        u0, wk, qkd, qg = _dn_groups(Q, K, V, beta, gc, gr, neg, offdiag, eye, C)
        Z = cat0(lambda ci, h: z[rws(ci), h * DN_DV:(h + 1) * DN_DV])
        S = [[so_ref[h] for h in hs] for hs in head_sets]
        ng = len(head_sets)
        for ci in range(nc):
            pick = lambda xs: xs[ci * ng:(ci + 1) * ng]
            O, S = _dn_apply(S, pick(u0), pick(wk), pick(qkd), pick(qg), pick(K), pick(gl), pick(gc), C)
            for gi, hs in enumerate(head_sets):
                o = O[gi] * lax.rsqrt(jnp.mean(O[gi] * O[gi], axis=-1, keepdims=True) + EPS) * nrm
                out = (o * _silu(Z[ci * ng + gi])).astype(o_ref.dtype)
                for i, h in enumerate(hs):
                    o_ref[pl.ds(r0 + ci * C, C), h * DN_DV:(h + 1) * DN_DV] = out[i * C:(i + 1) * C]
        for gi, hs in enumerate(head_sets):
            for i, h in enumerate(hs):
                so_ref[h] = S[gi][i]
        return carry

    lax.fori_loop(0, tb // CR, step, 0)
    halo_ref[...] = qkv_ref[tb - SUBLANES:tb, :]


def _dn_sample_kernel(qkv_ref, z_ref, us_ref, cst_ref, st_ref, cw_ref, prm_ref, nrm_ref, o_ref, so_ref, halo_ref,
                      *, bb, lvalid):
    C = SUBLANES
    nb = R_BD // (DN_HEADS * C)
    rows_all = bb * C

    halo_ref[:, 0:5, :] = jnp.zeros((bb, 5, DN_QKV), F32)
    halo_ref[:, 5:8, :] = cst_ref[...]
    cw = cw_ref[...]
    nrm = nrm_ref[...]
    acts = [_silu(_conv_rows(halo_ref[b], qkv_ref[b], cw)) for b in range(bb)]

    neg, offdiag, eye = _bd_masks(C)
    lane = _iota2((1, LANES), 1)
    gmask = (lane >= DN_HEADS) & (lane < 2 * DN_HEADS)
    neg_a = -jnp.exp(prm_ref[0:1, :])
    bias = prm_ref[1:2, :]
    us = us_ref[...].reshape(rows_all, LANES)
    valid = (_iota2((rows_all, 1), 0) & (C - 1)) < lvalid
    g = jnp.where(gmask & valid, neg_a * jax.nn.softplus(us + bias), 0.0)
    beta_all = jnp.where(valid, jax.nn.sigmoid(us), 0.0)
    ii = _iota2((rows_all, rows_all), 0)
    jj = _iota2((rows_all, rows_all), 1)
    same = (ii >> 3) == (jj >> 3)
    G_all = _mm_hi((same & (jj <= ii)).astype(F32), g)
    GL_all = _mm_hi(same.astype(F32), g)
    GT_all = G_all.T

    groups = [[(h, gi * nb + bl) for h in range(DN_HEADS) for bl in range(nb)]
              for gi in range(bb // nb)]
    rows = [slice(gi * nb * C, (gi + 1) * nb * C) for gi in range(bb // nb)]
    cat0 = lambda f: [jnp.concatenate([f(h, b) for h, b in us_], axis=0) for us_ in groups]
    cath = lambda f: [jnp.concatenate([f(h, rs) for h in range(DN_HEADS)], axis=0) for rs in rows]
    Q = [_l2n(x) * (DN_DK ** -0.5) for x in cat0(lambda h, b: acts[b][:, h * DN_DK:(h + 1) * DN_DK])]
    K = [_l2n(x) for x in cat0(lambda h, b: acts[b][:, DN_W + h * DN_DK:DN_W + (h + 1) * DN_DK])]
    V = cat0(lambda h, b: acts[b][:, 2 * DN_W + h * DN_DV:2 * DN_W + (h + 1) * DN_DV])
    beta = cath(lambda h, rs: beta_all[rs, h:h + 1])
    gc = cath(lambda h, rs: G_all[rs, DN_HEADS + h:DN_HEADS + h + 1])
    gl = cath(lambda h, rs: GL_all[rs, DN_HEADS + h:DN_HEADS + h + 1])
    gr = [jnp.concatenate([GT_all[DN_HEADS + h:DN_HEADS + h + 1, rs] for h in range(DN_HEADS)], axis=1)
          for rs in rows]
    u0, wk, qkd, qg = _dn_groups(Q, K, V, beta, gc, gr, neg, offdiag, eye, C)
    S = [[st_ref[b, h] for h, b in us_] for us_ in groups]
    O, S_new = _dn_apply(S, u0, wk, qkd, qg, K, gl, gc, C)
    Z = cat0(lambda h, b: z_ref[b][:, h * DN_DV:(h + 1) * DN_DV])
    for g, us_ in enumerate(groups):
        o = O[g] * lax.rsqrt(jnp.mean(O[g] * O[g], axis=-1, keepdims=True) + EPS) * nrm
        out = (o * _silu(Z[g])).astype(o_ref.dtype)
        for i, (h, b) in enumerate(us_):
            so_ref[b, h] = S_new[g][i]
            o_ref[b, :, h * DN_DV:(h + 1) * DN_DV] = out[i * C:(i + 1) * C]


def _dn_mixer(um, us, cst_all, st_all, j, cw, prm, nrm, *, bb, tb, C, lvalid):
    B, L, _ = um.shape
    const = lambda *_: (0, 0)
    out_shape = [jax.ShapeDtypeStruct((B, L, DN_W), BF16), jax.ShapeDtypeStruct(st_all.shape, F32)]
    if lvalid == L:
        assert bb == 1
        kern = functools.partial(_dn_prompt_kernel, tb=tb, C=C, nc=4)
        grid = (B, L // tb)
        in_specs = [pl.BlockSpec((None, tb, DN_QKV), lambda b, t: (b, t, AB_QKV // DN_QKV)),
                    pl.BlockSpec((None, tb, DN_W), lambda b, t: (b, t, AB_ZDN // DN_W)),
                    pl.BlockSpec((None, tb, LANES), lambda b, t: (b, t, 0)),
                    pl.BlockSpec((None, None, CONV_W - 1, DN_QKV), lambda b, t: (j, b, 0, 0)),
                    pl.BlockSpec((None, None, DN_HEADS, DN_DK, DN_DV), lambda b, t: (j, b, 0, 0, 0)),
                    pl.BlockSpec((CONV_W, DN_QKV), const),
                    pl.BlockSpec((2, LANES), const),
                    pl.BlockSpec((1, DN_DV), const)]
        out_specs = [pl.BlockSpec((None, tb, DN_W), lambda b, t: (b, t, 0)),
                     pl.BlockSpec((None, None, DN_HEADS, DN_DK, DN_DV), lambda b, t: (j, b, 0, 0, 0))]
        scratch = [pltpu.VMEM((SUBLANES, DN_QKV), F32)]
        sem = ("parallel", "arbitrary")
    else:
        assert L == SUBLANES and C == SUBLANES
        kern = functools.partial(_dn_sample_kernel, bb=bb, lvalid=lvalid)
        grid = (B // bb,)
        in_specs = [pl.BlockSpec((bb, L, DN_QKV), lambda b: (b, 0, AB_QKV // DN_QKV)),
                    pl.BlockSpec((bb, L, DN_W), lambda b: (b, 0, AB_ZDN // DN_W)),
                    pl.BlockSpec((bb, L, LANES), lambda b: (b, 0, 0)),
                    pl.BlockSpec((None, bb, CONV_W - 1, DN_QKV), lambda b: (j, b, 0, 0)),
                    pl.BlockSpec((None, bb, DN_HEADS, DN_DK, DN_DV), lambda b: (j, b, 0, 0, 0)),
                    pl.BlockSpec((CONV_W, DN_QKV), const),
                    pl.BlockSpec((2, LANES), const),
                    pl.BlockSpec((1, DN_DV), const)]
        out_specs = [pl.BlockSpec((bb, L, DN_W), lambda b: (b, 0, 0)),
                     pl.BlockSpec((None, bb, DN_HEADS, DN_DK, DN_DV), lambda b: (j, b, 0, 0, 0))]
        scratch = [pltpu.VMEM((bb, SUBLANES, DN_QKV), F32)]
        sem = ("parallel",)
    return pl.pallas_call(
        kern, grid=grid, in_specs=in_specs, out_specs=out_specs, out_shape=out_shape,
        scratch_shapes=scratch, input_output_aliases={4: 1},
        compiler_params=_params(dimension_semantics=sem), name="dn_mixer",
    )(um, um, us, cst_all, st_all, cw, prm, nrm)


def _gla_sample_kernel(q_ref, k_ref, v_ref, z_ref, us_ref, st_ref, wa_ref, ba_ref, nrm_ref, o_ref, so_ref,
                       *, bb, lvalid):
    C = SUBLANES
    R = bb * C
    ii = _iota2((C, C), 0)
    jj = _iota2((C, C), 1)
    nrm = nrm_ref[...]
    valid = (_iota2((R, 1), 0) & (C - 1)) < lvalid
    us = us_ref[...].reshape(R, LANES)
    glog = jax.nn.log_sigmoid(_mm(us.astype(BF16), wa_ref[...]) + ba_ref[...]) * (1.0 / GLA_GATE_NORM)
    glog = jnp.where(valid, glog, 0.0)
    ri = _iota2((R, R), 0)
    ci = _iota2((R, R), 1)
    G = _mm_hi((((ri >> 3) == (ci >> 3)) & (ci <= ri)).astype(F32), glog)
    q = q_ref[...].reshape(R, GLA_KW) * (GLA_DK ** -0.5)
    k = jnp.where(valid, k_ref[...].reshape(R, GLA_KW), 0.0)
    v = jnp.where(valid, v_ref[...].reshape(R, GLA_W), 0.0)
    z = z_ref[...].reshape(R, GLA_W)
    units = [(b, h) for b in range(bb) for h in range(GLA_HEADS)]
    U = range(len(units))
    tile = lambda x, b, h, w: x[b * C:(b + 1) * C, h * w:(h + 1) * w]
    Gu = [tile(G, b, h, GLA_DK) for b, h in units]
    qu = [tile(q, b, h, GLA_DK) for b, h in units]
    ku = [tile(k, b, h, GLA_DK) for b, h in units]
    vu = [tile(v, b, h, GLA_DV).astype(BF16) for b, h in units]
    AT = [jnp.zeros((C, C), F32) for _ in U]
    for i in range(C):
        for u in U:
            e = jnp.exp(jnp.minimum(Gu[u][i:i + 1, :] - Gu[u], 0.0))
            col = jnp.sum(ku[u] * e * qu[u][i:i + 1, :], axis=1, keepdims=True)
            AT[u] = AT[u] + jnp.where(jj == i, col, 0.0)
    AT = [jnp.where(ii <= jj, AT[u], 0.0).astype(BF16) for u in U]
    S = [st_ref[b, h] for b, h in units]
    o = [_mm((qu[u] * jnp.exp(Gu[u])).astype(BF16), S[u].astype(BF16)) + _tn(AT[u], vu[u]) for u in U]
    kend = [(ku[u] * jnp.exp(Gu[u][C - 1:C, :] - Gu[u])).astype(BF16) for u in U]
    gend = [jnp.exp(Gu[u].T[:, C - 1:C]) for u in U]
    for u, (b, h) in enumerate(units):
        so_ref[b, h] = gend[u] * S[u] + _tn(kend[u], vu[u])
    o = [o[u] * lax.rsqrt(jnp.mean(o[u] * o[u], axis=-1, keepdims=True) + EPS) * nrm for u in U]
    o = [o[u] * _silu(tile(z, b, h, GLA_DV)) for u, (b, h) in enumerate(units)]
    rows = [jnp.concatenate(o[b * GLA_HEADS:(b + 1) * GLA_HEADS], axis=1) for b in range(bb)]
    o_ref[...] = jnp.concatenate(rows, axis=0).astype(o_ref.dtype).reshape(bb, C, GLA_W)


def _gla_prompt_kernel(q_ref, k_ref, v_ref, z_ref, us_ref, st_ref, wa_ref, ba_ref, nrm_ref, o_ref, so_ref, g_ref,
                       *, tb, C):
    t = pl.program_id(1)
    H = range(GLA_HEADS)
    nch = tb // C

    @pl.when(t == 0)
    def _():
        so_ref[...] = st_ref[...]

    glog = jax.nn.log_sigmoid(_mm(us_ref[...].astype(BF16), wa_ref[...]) + ba_ref[...]) * (1.0 / GLA_GATE_NORM)
    step_in_chunk = _iota2((tb, 1), 0) & (C - 1)
    d = 1
    while d < C:
        glog = glog + jnp.where(step_in_chunk >= d, pltpu.roll(glog, d, axis=0), 0.0)
        d *= 2
    g_ref[...] = glog

    ii = _iota2((C, C), 0)
    jj = _iota2((C, C), 1)
    nrm = nrm_ref[...]
    hs = lambda x, h, w: x[:, h * w:(h + 1) * w]

    def scores(c):
        rows = pl.ds(pl.multiple_of(c * C, C), C)
        G = g_ref[rows, :]
        q = q_ref[rows, :] * (GLA_DK ** -0.5)
        k = k_ref[rows, :]
        AT = [jnp.zeros((C, C), F32) for _ in H]
        for i in range(C):
            for h in H:
                Gh = hs(G, h, GLA_DK)
                e = jnp.exp(jnp.minimum(Gh[i:i + 1, :] - Gh, 0.0))
                col = jnp.sum(hs(k, h, GLA_DK) * e * hs(q, h, GLA_DK)[i:i + 1, :], axis=1, keepdims=True)
                AT[h] = AT[h] + jnp.where(jj == i, col, 0.0)
        AT = [jnp.where(ii <= jj, AT[h], 0.0).astype(BF16) for h in H]
        gend = [jnp.exp(hs(G, h, GLA_DK).T[:, C - 1:C]) for h in H]
        return AT, gend

    def body(c, carry):
        AT, gend = carry
        rows = pl.ds(pl.multiple_of(c * C, C), C)
        G = g_ref[rows, :]
        q = q_ref[rows, :] * (GLA_DK ** -0.5)
        k = k_ref[rows, :]
        v = v_ref[rows, :]
        z = z_ref[rows, :]
        S = [so_ref[h] for h in H]
        vh = [hs(v, h, GLA_DV).astype(BF16) for h in H]
        o = [_mm((hs(q, h, GLA_DK) * jnp.exp(hs(G, h, GLA_DK))).astype(BF16), S[h].astype(BF16)) + _tn(AT[h], vh[h])
             for h in H]
        kend = [(hs(k, h, GLA_DK) * jnp.exp(hs(G, h, GLA_DK)[C - 1:C, :] - hs(G, h, GLA_DK))).astype(BF16) for h in H]
        for h in H:
            so_ref[h] = gend[h] * S[h] + _tn(kend[h], vh[h])
        nxt = scores(jnp.minimum(c + 1, nch - 1))
        o = [o[h] * lax.rsqrt(jnp.mean(o[h] * o[h], axis=-1, keepdims=True) + EPS) * nrm for h in H]
        o = [o[h] * _silu(hs(z, h, GLA_DV)) for h in H]
        o_ref[rows, :] = jnp.concatenate(o, axis=1).astype(o_ref.dtype)
        return nxt

    lax.fori_loop(0, nch, body, scores(0))


def _gla_mixer(um, us, st_all, j, wa, ba, nrm, *, bb, tb, C, lvalid):
    B, L, _ = um.shape
    const = lambda *_: (0, 0)
    st_blk = (GLA_HEADS, GLA_DK, GLA_DV)
    if lvalid == L:
        assert bb == 1
        kern = functools.partial(_gla_prompt_kernel, tb=tb, C=C)
        sq = None
        scratch = [pltpu.VMEM((tb, GLA_KW), F32)]
    else:
        assert L == SUBLANES and tb == L
        kern = functools.partial(_gla_sample_kernel, bb=bb, lvalid=lvalid)
        sq = bb
        scratch = []
    return pl.pallas_call(
        kern,
        grid=(B // bb, L // tb),
        in_specs=[pl.BlockSpec((sq, tb, GLA_KW), lambda b, t: (b, t, AB_GQ // GLA_KW)),
                  pl.BlockSpec((sq, tb, GLA_KW), lambda b, t: (b, t, AB_GK // GLA_KW)),
                  pl.BlockSpec((sq, tb, GLA_W), lambda b, t: (b, t, AB_GV // GLA_W)),
                  pl.BlockSpec((sq, tb, GLA_W), lambda b, t: (b, t, AB_ZGLA // GLA_W)),
                  pl.BlockSpec((sq, tb, LANES), lambda b, t: (b, t, 0)),
                  pl.BlockSpec((None, sq) + st_blk, lambda b, t: (j, b, 0, 0, 0)),
                  pl.BlockSpec((LANES, GLA_KW), const),
                  pl.BlockSpec((1, GLA_KW), const),
                  pl.BlockSpec((1, GLA_DV), const)],
        out_specs=[pl.BlockSpec((sq, tb, GLA_W), lambda b, t: (b, t, 0)),
                   pl.BlockSpec((None, sq) + st_blk, lambda b, t: (j, b, 0, 0, 0))],
        out_shape=[jax.ShapeDtypeStruct((B, L, GLA_W), BF16), jax.ShapeDtypeStruct(st_all.shape, F32)],
        scratch_shapes=scratch,
        input_output_aliases={5: 1},
        compiler_params=_params(dimension_semantics=("parallel", "arbitrary")),
        name="gla_mixer",
    )(um, um, um, um, us, st_all, wa, ba, nrm)


SSD_GW = SSD_W // SSD_GROUPS
SSD_REP = SSD_HEADS // SSD_GROUPS
SSD_ROWS = 64


def _lane_expand(x, h0, width):
    return jnp.concatenate([jnp.broadcast_to(x[:, h0 + j:h0 + j + 1], (x.shape[0], width)) for j in range(SSD_REP)],
                           axis=1)


def _ssd_masks(C):
    ii = _iota2((SSD_ROWS, SSD_GW), 0)
    jj = _iota2((SSD_ROWS, SSD_GW), 1) & (SSD_ROWS - 1)
    sh = C.bit_length() - 1
    negc = jnp.where(((ii >> sh) == (jj >> sh)) & (jj <= ii), 0.0, -1e30)
    ri = _iota2((SSD_GW, SSD_GW), 0)
    ci = _iota2((SSD_GW, SSD_GW), 1)
    bd = ((ri >> 6) == (ci >> 6)).astype(F32)
    return negc, bd


def _ssd_intra(act, G, GT, dt, gi, negc, bd):
    h0 = gi * SSD_REP
    xs = act[:, gi * SSD_GW:(gi + 1) * SSD_GW]
    bm = act[:, SSD_W + gi * SSD_STATE:SSD_W + (gi + 1) * SSD_STATE].astype(BF16)
    cm = act[:, SSD_W + (SSD_GROUPS + gi) * SSD_STATE:SSD_W + (SSD_GROUPS + gi + 1) * SSD_STATE].astype(BF16)
    gcc = _lane_expand(G, h0, SSD_HEADDIM)
    grc = jnp.concatenate([GT[h0 + j:h0 + j + 1, :] for j in range(SSD_REP)], axis=1)
    vcat = xs * _lane_expand(dt, h0, SSD_HEADDIM)
    cbm = _nt(cm, bm)
    acat = (jnp.concatenate([cbm] * SSD_REP, axis=1) * jnp.exp((gcc - grc) + negc)).astype(BF16)
    vbd = (jnp.concatenate([vcat] * SSD_REP, axis=0) * bd).astype(BF16)
    return _mm(acat, vbd), xs, vcat, gcc, bm, cm


def _ssd_finish(o, xs, z, dsk, nrm):
    y = (o + dsk * xs) * _silu(z)
    return y * lax.rsqrt(jnp.mean(y * y, axis=-1, keepdims=True) + EPS) * nrm


def _ssd_prompt_kernel(z_ref, xbc_ref, us_ref, cst_ref, st_ref, cw_ref, cb_ref, prm_ref, nrm_ref, o_ref, so_ref,
                       halo_ref, scat_ref, *, tb):
    t = pl.program_id(1)
    C = SSD_ROWS
    cat_heads = lambda f, gi: jnp.concatenate([f(gi * SSD_REP + j) for j in range(SSD_REP)], axis=1)

    @pl.when(t == 0)
    def _():
        for gi in range(SSD_GROUPS):
            scat_ref[gi] = cat_heads(lambda h: st_ref[h], gi)
        halo_ref[0:5, :] = jnp.zeros((5, SSD_XBC), F32)
        halo_ref[5:8, :] = cst_ref[...]

    negc, bd = _ssd_masks(C)
    tril = (_iota2((C, C), 1) <= _iota2((C, C), 0)).astype(F32)
    lane = _iota2((1, LANES), 1)
    hmask = lane < SSD_HEADS
    neg_a = -jnp.exp(prm_ref[0:1, :])
    bias = prm_ref[1:2, :]
    dsk = [_lane_expand(prm_ref[2:3, :], gi * SSD_REP, SSD_HEADDIM) for gi in range(SSD_GROUPS)]
    cw = cw_ref[...]
    cb = cb_ref[...]
    nrm = nrm_ref[...]
    groups = range(SSD_GROUPS)

    def chunk(c, carry):
        r0 = pl.multiple_of(c * C, C)
        x = xbc_ref[pl.ds(r0, C), :]
        prev = xbc_ref[pl.ds(pl.multiple_of(jnp.maximum(r0 - SUBLANES, 0), SUBLANES), SUBLANES), :]
        halo = jnp.where(c == 0, halo_ref[...], prev)
        act = _silu(_conv_rows(halo, x, cw, cb))
        us = us_ref[pl.ds(r0, C), :]
        dt = jnp.where(hmask, jax.nn.softplus(us + bias), 0.0)
        G = _mm_hi(tril, neg_a * dt)
        GT = G.T
        z = z_ref[pl.ds(r0, C), :]
        parts = [_ssd_intra(act, G, GT, dt, gi, negc, bd) for gi in groups]
        S = [scat_ref[gi] for gi in groups]
        inter = [_mm(parts[gi][5], S[gi].astype(BF16)) for gi in groups]
        for gi in groups:
            o_in, xs, vcat, gcc, bm, cm = parts[gi]
            glc = gcc[C - 1:C, :]
            w = (vcat * jnp.exp(glc - gcc)).astype(BF16)
            scat_ref[gi] = jnp.exp(glc) * S[gi] + _tn(bm, w)
            y = _ssd_finish(jnp.exp(gcc) * inter[gi] + o_in, xs, z[:, gi * SSD_GW:(gi + 1) * SSD_GW], dsk[gi],
                            nrm[:, gi * SSD_GW:(gi + 1) * SSD_GW])
            o_ref[pl.ds(r0, C), gi * SSD_GW:(gi + 1) * SSD_GW] = y.astype(o_ref.dtype)
        return carry

    lax.fori_loop(0, tb // C, chunk, 0)
    halo_ref[...] = xbc_ref[tb - SUBLANES:tb, :]

    @pl.when(t == pl.num_programs(1) - 1)
    def _():
        for gi in groups:
            s = scat_ref[gi]
            for j in range(SSD_REP):
                so_ref[gi * SSD_REP + j] = s[:, j * SSD_HEADDIM:(j + 1) * SSD_HEADDIM]


def _ssd_sample_kernel(z_ref, xbc_ref, us_ref, cst_ref, st_ref, cw_ref, cb_ref, prm_ref, nrm_ref, o_ref, so_ref,
                       halo_ref, *, bb, lvalid):
    C = SUBLANES
    assert bb * C == SSD_ROWS
    halo_ref[:, 0:5, :] = jnp.zeros((bb, 5, SSD_XBC), F32)
    halo_ref[:, 5:8, :] = cst_ref[...]
    cw = cw_ref[...]
    cb = cb_ref[...]
    nrm = nrm_ref[...]
    act = jnp.concatenate([_silu(_conv_rows(halo_ref[b], xbc_ref[b], cw, cb)) for b in range(bb)], axis=0)

    negc, bd = _ssd_masks(C)
    lane = _iota2((1, LANES), 1)
    neg_a = -jnp.exp(prm_ref[0:1, :])
    bias = prm_ref[1:2, :]
    us = us_ref[...].reshape(SSD_ROWS, LANES)
    valid = (_iota2((SSD_ROWS, 1), 0) & (C - 1)) < lvalid
    dt = jnp.where((lane < SSD_HEADS) & valid, jax.nn.softplus(us + bias), 0.0)
    ii = _iota2((SSD_ROWS, SSD_ROWS), 0)
    jj = _iota2((SSD_ROWS, SSD_ROWS), 1)
    same = (ii >> 3) == (jj >> 3)
    g = neg_a * dt
    G = _mm_hi((same & (jj <= ii)).astype(F32), g)
    GL = _mm_hi(same.astype(F32), g)
    GT = G.T
    z = z_ref[...].reshape(SSD_ROWS, SSD_W)
    groups = range(SSD_GROUPS)
    parts = [_ssd_intra(act, G, GT, dt, gi, negc, bd) for gi in groups]
    units = [(gi, b) for gi in groups for b in range(bb)]
    rows = lambda b: slice(b * C, (b + 1) * C)
    glc = [_lane_expand(GL, gi * SSD_REP, SSD_HEADDIM) for gi in groups]
    w = [(parts[gi][2] * jnp.exp(glc[gi] - parts[gi][3])).astype(BF16) for gi in groups]
    egl = [jnp.exp(glc[gi]) for gi in groups]
    S = [jnp.concatenate([st_ref[b, gi * SSD_REP + j] for j in range(SSD_REP)], axis=1) for gi, b in units]
    inter = [_mm(parts[gi][5][rows(b)], S[u].astype(BF16)) for u, (gi, b) in enumerate(units)]
    s_new = [egl[gi][b * C:b * C + 1, :] * S[u] + _tn(parts[gi][4][rows(b)], w[gi][rows(b)])
             for u, (gi, b) in enumerate(units)]
    for u, (gi, b) in enumerate(units):
        for j in range(SSD_REP):
            so_ref[b, gi * SSD_REP + j] = s_new[u][:, j * SSD_HEADDIM:(j + 1) * SSD_HEADDIM]
    outs = []
    for gi in groups:
        o_in, xs, vcat, gcc, bm, cm = parts[gi]
        o = jnp.exp(gcc) * jnp.concatenate(inter[gi * bb:(gi + 1) * bb], axis=0) + o_in
        dsk = _lane_expand(prm_ref[2:3, :], gi * SSD_REP, SSD_HEADDIM)
        outs.append(_ssd_finish(o, xs, z[:, gi * SSD_GW:(gi + 1) * SSD_GW], dsk, nrm[:, gi * SSD_GW:(gi + 1) * SSD_GW]))
    o_ref[...] = jnp.concatenate(outs, axis=1).astype(o_ref.dtype).reshape(bb, C, SSD_W)


def _ssd_mixer(um, us, cst_all, st_all, j, cw, cb, prm, nrm, *, bb, tb, lvalid):
    B, L, _ = um.shape
    const = lambda *_: (0, 0)
    out_shape = [jax.ShapeDtypeStruct((B, L, SSD_W), BF16), jax.ShapeDtypeStruct(st_all.shape, F32)]
    st_blk = (SSD_HEADS, SSD_STATE, SSD_HEADDIM)
    w_specs = [pl.BlockSpec((CONV_W, SSD_XBC), const), pl.BlockSpec((1, SSD_XBC), const),
               pl.BlockSpec((3, LANES), const), pl.BlockSpec((1, SSD_W), const)]
    if lvalid == L:
        assert bb == 1 and tb % SSD_ROWS == 0
        kern = functools.partial(_ssd_prompt_kernel, tb=tb)
        grid = (B, L // tb)
        in_specs = [pl.BlockSpec((None, tb, SSD_W), lambda b, t: (b, t, CD_Z // SSD_W)),
                    pl.BlockSpec((None, tb, SSD_XBC), lambda b, t: (b, t, CD_XBC // SSD_XBC)),
                    pl.BlockSpec((None, tb, LANES), lambda b, t: (b, t, 0)),
                    pl.BlockSpec((None, None, CONV_W - 1, SSD_XBC), lambda b, t: (j, b, 0, 0)),
                    pl.BlockSpec((None, None) + st_blk, lambda b, t: (j, b, 0, 0, 0))] + w_specs
        out_specs = [pl.BlockSpec((None, tb, SSD_W), lambda b, t: (b, t, 0)),
                     pl.BlockSpec((None, None) + st_blk, lambda b, t: (j, b, 0, 0, 0))]
        scratch = [pltpu.VMEM((SUBLANES, SSD_XBC), F32), pltpu.VMEM((SSD_GROUPS, SSD_STATE, SSD_GW), F32)]
        sem = ("parallel", "arbitrary")
    else:
        assert L == SUBLANES
        kern = functools.partial(_ssd_sample_kernel, bb=bb, lvalid=lvalid)
        grid = (B // bb,)
        in_specs = [pl.BlockSpec((bb, L, SSD_W), lambda b: (b, 0, CD_Z // SSD_W)),
                    pl.BlockSpec((bb, L, SSD_XBC), lambda b: (b, 0, CD_XBC // SSD_XBC)),
                    pl.BlockSpec((bb, L, LANES), lambda b: (b, 0, 0)),
                    pl.BlockSpec((None, bb, CONV_W - 1, SSD_XBC), lambda b: (j, b, 0, 0)),
                    pl.BlockSpec((None, bb) + st_blk, lambda b: (j, b, 0, 0, 0))] + w_specs
        out_specs = [pl.BlockSpec((bb, L, SSD_W), lambda b: (b, 0, 0)),
                     pl.BlockSpec((None, bb) + st_blk, lambda b: (j, b, 0, 0, 0))]
        scratch = [pltpu.VMEM((bb, SUBLANES, SSD_XBC), F32)]
        sem = ("parallel",)
    return pl.pallas_call(
        kern, grid=grid, in_specs=in_specs, out_specs=out_specs, out_shape=out_shape,
        scratch_shapes=scratch, input_output_aliases={4: 1},
        compiler_params=_params(dimension_semantics=sem), name="ssd_mixer",
    )(um, um, us, cst_all, st_all, cw, cb, prm, nrm)


def _lru_kernel(z_ref, x_ref, cst_ref, st_ref, cw_ref, cb_ref, wbd_ref, prm_ref, o_ref, so_ref, halo_ref, carry_ref,
                *, bb, tb, lvalid, ltotal):
    t = pl.program_id(1)

    @pl.when(t == 0)
    def _():
        carry_ref[...] = st_ref[...]
        halo_ref[:, 0:5, :] = jnp.zeros((bb, 5, LRU_W), F32)
        halo_ref[:, 5:8, :] = cst_ref[...]

    cw = cw_ref[...]
    cb = cb_ref[...]
    b_a = prm_ref[0:1, :]
    b_x = prm_ref[1:2, :]
    sp = jax.nn.softplus(-prm_ref[2:3, :])
    row = _iota2((tb, 1), 0)
    last_t, last_r = (lvalid - 1) // tb, (lvalid - 1) % tb

    def per_batch(bi, carry):
        x = x_ref[bi]
        xc = _conv_rows(halo_ref[bi], x, cw, cb)
        halo_ref[bi] = x[tb - SUBLANES:tb, :]
        xb = xc.astype(BF16)
        rs, is_ = [], []
        for cg in range(LRU_W // LANES):
            ri = _mm(xb[:, cg * LANES:(cg + 1) * LANES], wbd_ref[cg])
            rs.append(ri[:, :LANES])
            is_.append(ri[:, LANES:])
        r = jax.nn.sigmoid(jnp.concatenate(rs, axis=1) + b_a)
        i = jax.nn.sigmoid(jnp.concatenate(is_, axis=1) + b_x)
        log_a = (-LRU_C) * r * sp
        a = jnp.exp(log_a)
        b = jnp.sqrt(1.0 - jnp.exp(2.0 * log_a)) * (i * xc)
        d = 1
        while d < tb:
            m = row >= d
            a_s = pltpu.roll(a, d, axis=0)
            b_s = pltpu.roll(b, d, axis=0)
            b = jnp.where(m, a * b_s + b, b)
            a = jnp.where(m, a * a_s, a)
            d *= 2
        hs = a * carry_ref[bi] + b
        carry_ref[bi] = hs[tb - 1:tb, :]

        @pl.when(t == last_t)
        def _():
            so_ref[bi] = hs[last_r:last_r + 1, :]

        o_ref[bi] = (hs * _silu(z_ref[bi])).astype(o_ref.dtype)
        return carry

    lax.fori_loop(0, bb, per_batch, 0)


def _lru_mixer(um, cst_all, st_all, j, cw, cb, wbd, prm, *, bb, tb, lvalid):
    B, L, _ = um.shape
    kern = functools.partial(_lru_kernel, bb=bb, tb=tb, lvalid=lvalid, ltotal=L)
    const = lambda b, t: (0, 0)
    return pl.pallas_call(
        kern,
        grid=(B // bb, L // tb),
        in_specs=[pl.BlockSpec((bb, tb, LRU_W), lambda b, t: (b, t, CD_ZLRU // LRU_W)),
                  pl.BlockSpec((bb, tb, LRU_W), lambda b, t: (b, t, CD_XLRU // LRU_W)),
                  pl.BlockSpec((None, bb, CONV_W - 1, LRU_W), lambda b, t: (j, b, 0, 0)),
                  pl.BlockSpec((None, bb, 1, LRU_W), lambda b, t: (j, b, 0, 0)),
                  pl.BlockSpec((CONV_W, LRU_W), const),
                  pl.BlockSpec((1, LRU_W), const),
                  pl.BlockSpec((LRU_W // LANES, LANES, 2 * LANES), lambda b, t: (0, 0, 0)),
                  pl.BlockSpec((3, LRU_W), const)],
        out_specs=[pl.BlockSpec((bb, tb, LRU_W), lambda b, t: (b, t, 0)),
                   pl.BlockSpec((bb, 1, LRU_W), lambda b, t: (b, 0, 0))],
        out_shape=[jax.ShapeDtypeStruct((B, L, LRU_W), BF16),
                   jax.ShapeDtypeStruct((B, 1, LRU_W), F32)],
        scratch_shapes=[pltpu.VMEM((bb, SUBLANES, LRU_W), F32), pltpu.VMEM((bb, 1, LRU_W), F32)],
        compiler_params=_params(dimension_semantics=("parallel", "arbitrary")),
        name="lru_mixer",
    )(um, um, cst_all, st_all, cw, cb, wbd, prm)


def _split_cols(w, sizes):
    out, o = [], 0
    for s in sizes:
        out.append(w[:, o:o + s])
        o += s
    return out


def _lane_row(*pieces):
    v = jnp.concatenate([p.reshape(-1) for p in pieces])
    return jnp.pad(v, (0, LANES - v.shape[0])).reshape(1, LANES)


def _prep_weights(w):
    n_even = w['ab_w_in'].shape[0]
    n_odd = w['cd_w_in'].shape[0]
    D = w['ab_w_in'].shape[1]
    P = {'ab': [], 'cd': []}
    for j in range(n_even):
        qkv, beta, a, zdn, gq, gk, gv, glr, zgla = _split_cols(w['ab_w_in'][j], AB_SIZES)
        small = jnp.concatenate([beta, a, glr], axis=1)
        P['ab'].append(dict(
            w_main=jnp.concatenate([qkv, zdn, gq, gk, gv, zgla], axis=1).astype(BF16),
            w_small=jnp.pad(small, ((0, 0), (0, LANES - small.shape[1]))).astype(BF16),
            dn_cw=w['dn_conv_w'][j],
            dn_prm=jnp.concatenate([_lane_row(jnp.zeros((DN_HEADS,), F32), w['dn_a_log'][j]),
                                    _lane_row(jnp.zeros((DN_HEADS,), F32), w['dn_dt_bias'][j])], axis=0),
            dn_norm=w['dn_norm'][j].reshape(1, DN_DV),
            gla_wa=jnp.pad(w['gla_wa2'][j], ((2 * DN_HEADS, LANES - 2 * DN_HEADS - GLA_RANK), (0, 0))).astype(BF16),
            gla_ba=w['gla_ba'][j].reshape(1, GLA_KW),
            gla_norm=w['gla_norm'][j].reshape(1, GLA_DV),
            w_out_a=w['ab_w_out'][j][:DN_W].astype(BF16),
            w_out_b=w['ab_w_out'][j][DN_W:].astype(BF16),
        ))
    eye2 = jnp.eye(2, dtype=F32)
    for j in range(n_odd):
        z, xbc, dt, zlru, xlru = _split_cols(w['cd_w_in'][j], CD_SIZES)
        wa = w['lru_wa'][j].reshape(LRU_W // LANES, 2, LRU_BS, LRU_BS)
        wx = w['lru_wx'][j].reshape(LRU_W // LANES, 2, LRU_BS, LRU_BS)
        bd = lambda m: jnp.einsum('gbcd,be->gbced', m, eye2).reshape(LRU_W // LANES, LANES, LANES)
        P['cd'].append(dict(
            w_main=jnp.concatenate([z, zlru, xlru, xbc], axis=1).astype(BF16),
            w_small=jnp.pad(dt, ((0, 0), (0, LANES - dt.shape[1]))).astype(BF16),
            ssd_cw=w['ssd_conv_w'][j],
            ssd_cb=w['ssd_conv_b'][j].reshape(1, SSD_XBC),
            ssd_prm=jnp.concatenate([_lane_row(w['ssd_a_log'][j]), _lane_row(w['ssd_dt_bias'][j]),
                                     _lane_row(w['ssd_d'][j])], axis=0),
            ssd_norm=w['ssd_norm'][j].reshape(1, SSD_W),
            lru_cw=w['lru_conv_w'][j],
            lru_cb=w['lru_conv_b'][j].reshape(1, LRU_W),
            lru_wbd=jnp.concatenate([bd(wa), bd(wx)], axis=2).astype(BF16),
            lru_prm=jnp.stack([w['lru_ba'][j], w['lru_bx'][j], w['lru_lambda'][j]], axis=0),
            w_out_a=w['cd_w_out'][j][:SSD_W].astype(BF16),
            w_out_b=w['cd_w_out'][j][SSD_W:].astype(BF16),
        ))
    P['norm_g'] = w['norm_g'].reshape(-1, 1, D)
    P['ple_norm'] = w['ple_norm'].reshape(-1, 1, D)
    P['ple_w_gate'] = w['ple_w_gate'].astype(BF16)
    P['ple_w_proj'] = w['ple_w_proj'].astype(BF16)
    P['final_norm'] = w['final_norm'].reshape(1, D)
    return P


def _trunk(x, p, states, P, *, lvalid, bb, tb, c_dn, c_gla, c_ssd):
    B, L, D = x.shape
    depth = P['norm_g'].shape[0]
    st_dn, st_dnc, st_gla, st_ssd, st_ssdc, st_lru, st_lruc = states
    st_lru = st_lru.reshape(st_lru.shape[0], B, 1, LRU_W)
    dnc_new, ssdc_new, lru_new, lruc_new = [], [], [], []
    T = B * L
    h = x.reshape(T, D)
    p_all = p.reshape(depth, T, PLE_DIM)
    lo = lvalid - (CONV_W - 1)
    for li in range(depth):
        j = li // 2
        if li % 2 == 0:
            W = P['ab'][j]
            um, us = _inproj(h, P['norm_g'][li], W['w_main'], W['w_small'])
            um = um.reshape(B, L, AB_MAIN)
            us = us.reshape(B, L, LANES)
            oa, st_dn = _dn_mixer(um, us, st_dnc, st_dn, j, W['dn_cw'], W['dn_prm'], W['dn_norm'],
                                  bb=bb, tb=tb, C=c_dn, lvalid=lvalid)
            ob, st_gla = _gla_mixer(um, us, st_gla, j, W['gla_wa'], W['gla_ba'], W['gla_norm'],
                                    bb=bb, tb=tb, C=c_gla, lvalid=lvalid)
            dnc_new.append(um[:, lo:lvalid, AB_QKV:AB_QKV + DN_QKV])
        else:
            W = P['cd'][j]
            um, us = _inproj(h, P['norm_g'][li], W['w_main'], W['w_small'])
            um = um.reshape(B, L, CD_MAIN)
            us = us.reshape(B, L, LANES)
            oa, st_ssd = _ssd_mixer(um, us, st_ssdc, st_ssd, j, W['ssd_cw'], W['ssd_cb'], W['ssd_prm'],
                                    W['ssd_norm'], bb=bb, tb=tb, lvalid=lvalid)
            ob, lru_j = _lru_mixer(um, st_lruc, st_lru, j, W['lru_cw'], W['lru_cb'],
                                   W['lru_wbd'], W['lru_prm'], bb=bb, tb=tb, lvalid=lvalid)
            ssdc_new.append(um[:, lo:lvalid, CD_XBC:CD_XBC + SSD_XBC])
            lru_new.append(lru_j.reshape(B, LRU_W))
            lruc_new.append(um[:, lo:lvalid, CD_XLRU:CD_XLRU + LRU_W])
        h = _outproj(oa.reshape(T, -1), ob.reshape(T, -1), h, p_all, li, W['w_out_a'], W['w_out_b'],
                     P['ple_norm'][li], P['ple_w_gate'][li], P['ple_w_proj'][li], P['final_norm'],
                     final=(li == depth - 1))
    new_states = [st_dn, jnp.stack(dnc_new), st_gla, st_ssd, jnp.stack(ssdc_new), jnp.stack(lru_new),
                  jnp.stack(lruc_new)]
    return h.reshape(B, L, D), new_states


def _pick_tb(L):
    tb = min(L, 512)
    assert L % tb == 0
    return tb


def kernel(x_prompt, x_sample, state_dn, state_dn_conv, state_gla, state_ssd, state_ssd_conv, state_lru, state_lru_conv, p_prompt, p_sample, norm_g, final_norm, ab_w_in, dn_conv_w, dn_a_log, dn_dt_bias, dn_norm, gla_wa2, gla_ba, gla_norm, ab_w_out, cd_w_in, ssd_conv_w, ssd_conv_b, ssd_a_log, ssd_dt_bias, ssd_d, ssd_norm, lru_conv_w, lru_conv_b, lru_wa, lru_ba, lru_wx, lru_bx, lru_lambda, cd_w_out, ple_w_proj, ple_norm, ple_w_gate):
    w = {
        'norm_g': norm_g, 'final_norm': final_norm,
        'ab_w_in': ab_w_in, 'dn_conv_w': dn_conv_w, 'dn_a_log': dn_a_log, 'dn_dt_bias': dn_dt_bias,
        'dn_norm': dn_norm, 'gla_wa2': gla_wa2, 'gla_ba': gla_ba, 'gla_norm': gla_norm, 'ab_w_out': ab_w_out,
        'cd_w_in': cd_w_in, 'ssd_conv_w': ssd_conv_w, 'ssd_conv_b': ssd_conv_b, 'ssd_a_log': ssd_a_log,
        'ssd_dt_bias': ssd_dt_bias, 'ssd_d': ssd_d, 'ssd_norm': ssd_norm,
        'lru_conv_w': lru_conv_w, 'lru_conv_b': lru_conv_b, 'lru_wa': lru_wa, 'lru_ba': lru_ba,
        'lru_wx': lru_wx, 'lru_bx': lru_bx, 'lru_lambda': lru_lambda, 'cd_w_out': cd_w_out,
        'ple_w_proj': ple_w_proj, 'ple_norm': ple_norm, 'ple_w_gate': ple_w_gate,
    }
    P = _prep_weights(w)
    n_even, n_odd = ab_w_in.shape[0], cd_w_in.shape[0]

    Bp, Lp, _ = x_prompt.shape
    assert Lp >= CONV_W - 1 and Lp % 64 == 0
    zero_states = (
        jnp.zeros((n_even, Bp, DN_HEADS, DN_DK, DN_DV), F32),
        jnp.zeros((n_even, Bp, CONV_W - 1, DN_QKV), F32),
        jnp.zeros((n_even, Bp, GLA_HEADS, GLA_DK, GLA_DV), F32),
        jnp.zeros((n_odd, Bp, SSD_HEADS, SSD_STATE, SSD_HEADDIM), F32),
        jnp.zeros((n_odd, Bp, CONV_W - 1, SSD_XBC), F32),
        jnp.zeros((n_odd, Bp, LRU_W), F32),
        jnp.zeros((n_odd, Bp, CONV_W - 1, LRU_W), F32),
    )
    y_p, st_p = _trunk(x_prompt, p_prompt, zero_states, P, lvalid=Lp, bb=1, tb=_pick_tb(Lp),
                       c_dn=64, c_gla=16, c_ssd=64)

    Bs, Ls, _ = x_sample.shape
    assert CONV_W - 1 <= Ls <= SUBLANES
    pad = SUBLANES - Ls
    xs = jnp.pad(x_sample, ((0, 0), (0, pad), (0, 0)))
    ps = jnp.pad(p_sample, ((0, 0), (0, 0), (0, pad), (0, 0)))
    sample_states = (state_dn, state_dn_conv, state_gla, state_ssd, state_ssd_conv, state_lru, state_lru_conv)
    bb_s = 8 if Bs % 8 == 0 else 1
    y_s, st_s = _trunk(xs, ps, sample_states, P, lvalid=Ls, bb=bb_s, tb=SUBLANES,
                       c_dn=SUBLANES, c_gla=SUBLANES, c_ssd=SUBLANES)
    return (y_p, y_s[:, :Ls], *st_p, *st_s)
```

```python
import functools

import jax
import jax.numpy as jnp
from jax import lax
from jax.experimental import pallas as pl
from jax.experimental.pallas import tpu as pltpu

F32 = jnp.float32
BF16 = jnp.bfloat16
HI = lax.Precision.HIGHEST

EPS = 1e-6
LOG2E = 1.4426950408889634
CONV_W = 4
PLE_DIM = 256
DN_HEADS, DN_DK, DN_DV = 8, 128, 128
DN_W = DN_HEADS * DN_DV
DN_QKV = 2 * DN_HEADS * DN_DK + DN_W
GLA_HEADS, GLA_DK, GLA_DV = 4, 128, 256
GLA_KW = GLA_HEADS * GLA_DK
GLA_W = GLA_HEADS * GLA_DV
GLA_RANK = 16
GLA_GATE_NORM = 16.0
AB_SIZES = (DN_QKV, DN_HEADS, DN_HEADS, DN_W, GLA_KW, GLA_KW, GLA_W, GLA_RANK, GLA_W)
SSD_HEADS, SSD_HEADDIM, SSD_GROUPS, SSD_STATE = 16, 64, 4, 64
SSD_W = SSD_HEADS * SSD_HEADDIM
SSD_XBC = SSD_W + 2 * SSD_GROUPS * SSD_STATE
LRU_W, LRU_BLOCKS = 1024, 16
LRU_BS = LRU_W // LRU_BLOCKS
LRU_C = 8.0
CD_SIZES = (SSD_W, SSD_XBC, SSD_HEADS, LRU_W, LRU_W)

LANES = 128
SUBLANES = 8
VMEM_LIMIT = 52 * 1024 * 1024

AB_QKV, AB_ZDN, AB_GQ, AB_GK, AB_GV, AB_ZGLA = 0, 3072, 4096, 4608, 5120, 6144
AB_MAIN = 7168
CD_Z, CD_ZLRU, CD_XLRU, CD_XBC = 0, 1024, 2048, 3072
CD_MAIN = 4608


def _silu(x):
    return x * jax.nn.sigmoid(x)


def _nt(a, b):
    return lax.dot_general(a, b, (((1,), (1,)), ((), ())), preferred_element_type=F32)


def _tn(a, b):
    return lax.dot_general(a, b, (((0,), (0,)), ((), ())), preferred_element_type=F32)


def _mm(a, b):
    return jnp.dot(a, b, preferred_element_type=F32)


def _mm_hi(a, b):
    return jnp.dot(a, b, preferred_element_type=F32, precision=HI)


def _iota2(shape, dim):
    return lax.broadcasted_iota(jnp.int32, shape, dim)


def _conv_rows(halo, x, w, bias=None):
    cat = jnp.concatenate([halo, x], axis=0)
    out = x * w[3:4]
    for s in (1, 2, 3):
        out = out + pltpu.roll(cat, s, axis=0)[SUBLANES:] * w[3 - s:4 - s]
    if bias is not None:
        out = out + bias
    return out


def _params(**kw):
    return pltpu.CompilerParams(vmem_limit_bytes=VMEM_LIMIT, **kw)


def _own_layer(st_ref, so_ref, carry):
    if carry is None:
        return st_ref, so_ref
    n, jl = carry
    for l in range(n):
        if l != jl:
            so_ref[l] = st_ref[l]
    return st_ref.at[jl], so_ref.at[jl]


def _state_specs(st_all, j, bb, carry):
    tail = st_all.shape[2:]
    zeros = (0,) * len(tail)
    if carry:
        return pl.BlockSpec((st_all.shape[0], bb) + tail, lambda b, *_: (0, b) + zeros)
    return pl.BlockSpec((None, bb) + tail, lambda b, *_: (j, b) + zeros)


def _inproj_kernel(h_ref, g_ref, wm_ref, ws_ref, um_ref, us_ref, xn_ref):
    @pl.when(pl.program_id(1) == 0)
    def _():
        x = h_ref[...]
        r = lax.rsqrt(jnp.mean(x * x, axis=-1, keepdims=True) + EPS)
        xn = (x * r * g_ref[...]).astype(BF16)
        xn_ref[...] = xn
        us_ref[...] = _mm(xn, ws_ref[...])

    um_ref[...] = _mm(xn_ref[...], wm_ref[...])


def _inproj(h, g_all, li, wm_all, ws_all, j):
    T, D = h.shape
    Nm = wm_all.shape[2]
    tm = min(T, 1024)
    tn = 1024 if Nm % 1024 == 0 else 1536
    return pl.pallas_call(
        _inproj_kernel,
        grid=(T // tm, Nm // tn),
        in_specs=[pl.BlockSpec((tm, D), lambda i, c: (i, 0)),
                  pl.BlockSpec((None, 1, D), lambda i, c: (li, 0, 0)),
                  pl.BlockSpec((None, D, tn), lambda i, c: (j, 0, c)),
                  pl.BlockSpec((None, D, LANES), lambda i, c: (j, 0, 0))],
        out_specs=[pl.BlockSpec((tm, tn), lambda i, c: (i, c)),
                   pl.BlockSpec((tm, LANES), lambda i, c: (i, 0))],
        out_shape=[jax.ShapeDtypeStruct((T, Nm), F32), jax.ShapeDtypeStruct((T, LANES), F32)],
        scratch_shapes=[pltpu.VMEM((tm, D), BF16)],
        compiler_params=_params(dimension_semantics=("parallel", "arbitrary")),
        name="inproj",
    )(h, g_all, wm_all, ws_all)


def _outproj_kernel(oa_ref, ob_ref, h_ref, p_ref, wa_ref, wb_ref, pg_ref, wg_ref, wp_ref, fg_ref, out_ref, *, final):
    h1 = h_ref[...] + _mm(oa_ref[...], wa_ref[...]) + _mm(ob_ref[...], wb_ref[...])
    r = lax.rsqrt(jnp.mean(h1 * h1, axis=-1, keepdims=True) + EPS)
    hn = (h1 * r * pg_ref[...]).astype(BF16)
    gate = jax.nn.sigmoid(_mm(hn, wg_ref[...]))
    pp = _mm(p_ref[...].astype(BF16), wp_ref[...])
    h2 = h1 + gate * pp
    if final:
        r2 = lax.rsqrt(jnp.mean(h2 * h2, axis=-1, keepdims=True) + EPS)
        h2 = h2 * r2 * fg_ref[...]
    out_ref[...] = h2


def _outproj(oa, ob, h, p_all, li, w_out_all, j, pg_all, wg_all, wp_all, fg, final):
    T, D = h.shape
    Wa, Wb = oa.shape[1], ob.shape[1]
    assert Wa == Wb
    tm = min(T, 512)
    resident = lambda shape, idx: pl.BlockSpec(shape, lambda i: idx, pipeline_mode=pl.Buffered(1))
    return pl.pallas_call(
        functools.partial(_outproj_kernel, final=final),
        grid=(T // tm,),
        in_specs=[pl.BlockSpec((tm, Wa), lambda i: (i, 0)),
                  pl.BlockSpec((tm, Wb), lambda i: (i, 0)),
                  pl.BlockSpec((tm, D), lambda i: (i, 0)),
                  pl.BlockSpec((None, tm, PLE_DIM), lambda i: (li, i, 0)),
                  resident((None, Wa, D), (j, 0, 0)),
                  resident((None, Wb, D), (j, 1, 0)),
                  resident((None, 1, D), (li, 0, 0)),
                  resident((None, D, D), (li, 0, 0)),
                  resident((None, PLE_DIM, D), (li, 0, 0)),
                  resident((1, D), (0, 0))],
        out_specs=pl.BlockSpec((tm, D), lambda i: (i, 0)),
        out_shape=jax.ShapeDtypeStruct((T, D), F32),
        compiler_params=_params(dimension_semantics=("parallel",)),
        name="outproj",
    )(oa, ob, h, p_all, w_out_all, w_out_all, pg_all, wg_all, wp_all, fg)


R_BD = 128


def _bd_masks(C):
    ii = _iota2((R_BD, R_BD), 0)
    jj = _iota2((R_BD, R_BD), 1)
    sh = C.bit_length() - 1
    lower = ((ii >> sh) == (jj >> sh)) & (jj <= ii)
    neg = jnp.where(lower, 0.0, -1e30)
    eye = (ii == jj).astype(F32)
    return neg, 1.0 - eye, eye


def _dn_groups(Q, K, V, beta, gc, gr, neg, offdiag, eye, C):
    n = range(len(Q))
    Kb = [K[g].astype(BF16) for g in n]
    KK = [_nt(Kb[g], Kb[g]) for g in n]
    QK = [_nt(Q[g].astype(BF16), Kb[g]) for g in n]
    dm = [jnp.exp((gc[g] - gr[g]) + neg) for g in n]
    A = [(beta[g] * KK[g]) * (dm[g] * offdiag) for g in n]
    T = [eye - A[g] for g in n]
    X = A
    p = 2
    while p < C:
        Xb = [X[g].astype(BF16) for g in n]
        X = [_mm(Xb[g], Xb[g]) for g in n]
        T = [T[g] + _mm(T[g].astype(BF16), X[g].astype(BF16)) for g in n]
        p *= 2
    eg = [jnp.exp(gc[g]) for g in n]
    rhs = [jnp.concatenate([beta[g] * V[g], (beta[g] * eg[g]) * K[g]], axis=1).astype(BF16) for g in n]
    sol = [_mm(T[g].astype(BF16), rhs[g]) for g in n]
    return ([sol[g][:, :DN_DV] for g in n], [sol[g][:, DN_DV:] for g in n],
            [QK[g] * dm[g] for g in n], [Q[g] * eg[g] for g in n])


def _l2n(x):
    return x * lax.rsqrt(jnp.sum(x * x, axis=-1, keepdims=True) + EPS)


def _dn_apply(S, u0, wk, qkd, qg, K, gl, gc, C):
    n = range(len(S))
    nu = len(S[0])
    blk = lambda x, i: x[i * C:(i + 1) * C]
    r = [[None] * nu for _ in n]
    for i in range(nu):
        for g in n:
            lhs = jnp.concatenate([blk(wk[g], i), blk(qg[g], i)], axis=0).astype(BF16)
            r[g][i] = _mm(lhs, S[g][i].astype(BF16))
    Ub = [jnp.concatenate([blk(u0[g], i) - r[g][i][:C] for i in range(nu)], axis=0).astype(BF16) for g in n]
    O = [jnp.concatenate([r[g][i][C:] for i in range(nu)], axis=0) + _mm(qkd[g].astype(BF16), Ub[g]) for g in n]
    kend = [(K[g] * jnp.exp(gl[g] - gc[g])).astype(BF16) for g in n]
    egl = [jnp.exp(gl[g]) for g in n]
    S_new = [[None] * nu for _ in n]
    for i in range(nu):
        for g in n:
            S_new[g][i] = egl[g][i * C:i * C + 1, :] * S[g][i] + _tn(blk(kend[g], i), blk(Ub[g], i))
    return O, S_new


def _dn_prompt_kernel(qkv_ref, z_ref, us_ref, cst_ref, st_ref, cw_ref, prm_ref, nrm_ref, o_ref, so_ref, halo_ref,
                      *, tb, C, nc):
    t = pl.program_id(1)
    hg = R_BD // C
    CR = nc * C

    @pl.when(t == 0)
    def _():
        so_ref[...] = st_ref[...]
        halo_ref[0:5, :] = jnp.zeros((5, DN_QKV), F32)
        halo_ref[5:8, :] = cst_ref[...]

    neg, offdiag, eye = _bd_masks(C)
    ii = _iota2((CR, CR), 0)
    jj = _iota2((CR, CR), 1)
    sh = C.bit_length() - 1
    tril = (((ii >> sh) == (jj >> sh)) & (jj <= ii)).astype(F32)
    lane = _iota2((1, LANES), 1)
    gmask = (lane >= DN_HEADS) & (lane < 2 * DN_HEADS)
    neg_a = -jnp.exp(prm_ref[0:1, :])
    bias = prm_ref[1:2, :]
    cw = cw_ref[...]
    nrm = nrm_ref[...]
    head_sets = [range(gi * hg, (gi + 1) * hg) for gi in range(DN_HEADS // hg)]
    groups = [(ci, hs) for ci in range(nc) for hs in head_sets]
    rws = lambda ci: slice(ci * C, (ci + 1) * C)

    def step(cp, carry):
        r0 = pl.multiple_of(cp * CR, CR)
        x = qkv_ref[pl.ds(r0, CR), :]
        prev = qkv_ref[pl.ds(pl.multiple_of(jnp.maximum(r0 - SUBLANES, 0), SUBLANES), SUBLANES), :]
        halo = jnp.where(cp == 0, halo_ref[...], prev)
        act = _silu(_conv_rows(halo, x, cw))
        us = us_ref[pl.ds(r0, CR), :]
        g = jnp.where(gmask, neg_a * jax.nn.softplus(us + bias), 0.0)
        beta_all = jax.nn.sigmoid(us)
        G = _mm_hi(tril, g)
        GT = G.T
        z = z_ref[pl.ds(r0, CR), :]
        cat0 = lambda f: [jnp.concatenate([f(ci, h) for h in hs], axis=0) for ci, hs in groups]
        Q = [_l2n(a) * (DN_DK ** -0.5) for a in cat0(lambda ci, h: act[rws(ci), h * DN_DK:(h + 1) * DN_DK])]
        K = [_l2n(a) for a in cat0(lambda ci, h: act[rws(ci), DN_W + h * DN_DK:DN_W + (h + 1) * DN_DK])]
        V = cat0(lambda ci, h: act[rws(ci), 2 * DN_W + h * DN_DV:2 * DN_W + (h + 1) * DN_DV])
        beta = cat0(lambda ci, h: beta_all[rws(ci), h:h + 1])
        gc = cat0(lambda ci, h: G[rws(ci), DN_HEADS + h:DN_HEADS + h + 1])
        gl = cat0(lambda ci, h: jnp.broadcast_to(
            G[(ci + 1) * C - 1:(ci + 1) * C, DN_HEADS + h:DN_HEADS + h + 1], (C, 1)))
        gr = [jnp.concatenate([GT[DN_HEADS + h:DN_HEADS + h + 1, rws(ci)] for h in hs], axis=1)
              for ci, hs in groups]
        u0, wk, qkd, qg = _dn_groups(Q, K, V, beta, gc, gr, neg, offdiag, eye, C)
        Z = cat0(lambda ci, h: z[rws(ci), h * DN_DV:(h + 1) * DN_DV])
        S = [[so_ref[h] for h in hs] for hs in head_sets]
        ng = len(head_sets)
        for ci in range(nc):
            pick = lambda xs: xs[ci * ng:(ci + 1) * ng]
            O, S = _dn_apply(S, pick(u0), pick(wk), pick(qkd), pick(qg), pick(K), pick(gl), pick(gc), C)
            for gi, hs in enumerate(head_sets):
                o = O[gi] * lax.rsqrt(jnp.mean(O[gi] * O[gi], axis=-1, keepdims=True) + EPS) * nrm
                out = (o * _silu(Z[ci * ng + gi])).astype(o_ref.dtype)
                for i, h in enumerate(hs):
                    o_ref[pl.ds(r0 + ci * C, C), h * DN_DV:(h + 1) * DN_DV] = out[i * C:(i + 1) * C]
        for gi, hs in enumerate(head_sets):
            for i, h in enumerate(hs):
                so_ref[h] = S[gi][i]
        return carry

    lax.fori_loop(0, tb // CR, step, 0)
    halo_ref[...] = qkv_ref[tb - SUBLANES:tb, :]


def _dn_sample_kernel(qkv_ref, z_ref, us_ref, cst_ref, st_ref, cw_ref, prm_ref, nrm_ref, o_ref, so_ref, halo_ref,
                      *, bb, lvalid, carry):
    st_ref, so_ref = _own_layer(st_ref, so_ref, carry)
    C = SUBLANES
    nb = R_BD // (DN_HEADS * C)
    rows_all = bb * C

    halo_ref[:, 0:5, :] = jnp.zeros((bb, 5, DN_QKV), F32)
    halo_ref[:, 5:8, :] = cst_ref[...]
    cw = cw_ref[...]
    nrm = nrm_ref[...]
    acts = [_silu(_conv_rows(halo_ref[b], qkv_ref[b], cw)) for b in range(bb)]

    neg, offdiag, eye = _bd_masks(C)
    lane = _iota2((1, LANES), 1)
    gmask = (lane >= DN_HEADS) & (lane < 2 * DN_HEADS)
    neg_a = -jnp.exp(prm_ref[0:1, :])
    bias = prm_ref[1:2, :]
    us = us_ref[...].reshape(rows_all, LANES)
    valid = (_iota2((rows_all, 1), 0) & (C - 1)) < lvalid
    g = jnp.where(gmask & valid, neg_a * jax.nn.softplus(us + bias), 0.0)
    beta_all = jnp.where(valid, jax.nn.sigmoid(us), 0.0)
    ii = _iota2((rows_all, rows_all), 0)
    jj = _iota2((rows_all, rows_all), 1)
    same = (ii >> 3) == (jj >> 3)
    G_all = _mm_hi((same & (jj <= ii)).astype(F32), g)
    GL_all = _mm_hi(same.astype(F32), g)
    GT_all = G_all.T

    groups = [[(h, gi * nb + bl) for h in range(DN_HEADS) for bl in range(nb)]
              for gi in range(bb // nb)]
    rows = [slice(gi * nb * C, (gi + 1) * nb * C) for gi in range(bb // nb)]
    cat0 = lambda f: [jnp.concatenate([f(h, b) for h, b in us_], axis=0) for us_ in groups]
    cath = lambda f: [jnp.concatenate([f(h, rs) for h in range(DN_HEADS)], axis=0) for rs in rows]
    Q = [_l2n(x) * (DN_DK ** -0.5) for x in cat0(lambda h, b: acts[b][:, h * DN_DK:(h + 1) * DN_DK])]
    K = [_l2n(x) for x in cat0(lambda h, b: acts[b][:, DN_W + h * DN_DK:DN_W + (h + 1) * DN_DK])]
    V = cat0(lambda h, b: acts[b][:, 2 * DN_W + h * DN_DV:2 * DN_W + (h + 1) * DN_DV])
    beta = cath(lambda h, rs: beta_all[rs, h:h + 1])
    gc = cath(lambda h, rs: G_all[rs, DN_HEADS + h:DN_HEADS + h + 1])
    gl = cath(lambda h, rs: GL_all[rs, DN_HEADS + h:DN_HEADS + h + 1])
    gr = [jnp.concatenate([GT_all[DN_HEADS + h:DN_HEADS + h + 1, rs] for h in range(DN_HEADS)], axis=1)
          for rs in rows]
    u0, wk, qkd, qg = _dn_groups(Q, K, V, beta, gc, gr, neg, offdiag, eye, C)
    S = [[st_ref[b, h] for h, b in us_] for us_ in groups]
    O, S_new = _dn_apply(S, u0, wk, qkd, qg, K, gl, gc, C)
    Z = cat0(lambda h, b: z_ref[b][:, h * DN_DV:(h + 1) * DN_DV])
    for g, us_ in enumerate(groups):
        o = O[g] * lax.rsqrt(jnp.mean(O[g] * O[g], axis=-1, keepdims=True) + EPS) * nrm
        out = (o * _silu(Z[g])).astype(o_ref.dtype)
        for i, (h, b) in enumerate(us_):
            so_ref[b, h] = S_new[g][i]
            o_ref[b, :, h * DN_DV:(h + 1) * DN_DV] = out[i * C:(i + 1) * C]


def _dn_mixer(um, us, cst_all, st_all, j, cw, prm, nrm, *, bb, tb, C, lvalid):
    B, L, _ = um.shape
    const = lambda *_: (0, 0)
    out_shape = [jax.ShapeDtypeStruct((B, L, DN_W), BF16), jax.ShapeDtypeStruct(st_all.shape, F32)]
    carry = None
    if lvalid == L:
        assert bb == 1
        kern = functools.partial(_dn_prompt_kernel, tb=tb, C=C, nc=4)
        grid = (B, L // tb)
        in_specs = [pl.BlockSpec((None, tb, DN_QKV), lambda b, t: (b, t, AB_QKV // DN_QKV)),
                    pl.BlockSpec((None, tb, DN_W), lambda b, t: (b, t, AB_ZDN // DN_W)),
                    pl.BlockSpec((None, tb, LANES), lambda b, t: (b, t, 0)),
                    pl.BlockSpec((None, None, CONV_W - 1, DN_QKV), lambda b, t: (j, b, 0, 0)),
                    pl.BlockSpec((None, None, DN_HEADS, DN_DK, DN_DV), lambda b, t: (j, b, 0, 0, 0)),
                    pl.BlockSpec((CONV_W, DN_QKV), const),
                    pl.BlockSpec((2, LANES), const),
                    pl.BlockSpec((1, DN_DV), const)]
        out_specs = [pl.BlockSpec((None, tb, DN_W), lambda b, t: (b, t, 0)),
                     pl.BlockSpec((None, None, DN_HEADS, DN_DK, DN_DV), lambda b, t: (j, b, 0, 0, 0))]
        scratch = [pltpu.VMEM((SUBLANES, DN_QKV), F32)]
        sem = ("parallel", "arbitrary")
    else:
        assert L == SUBLANES and C == SUBLANES
        carry = (st_all.shape[0], j) if j == 0 else None
        kern = functools.partial(_dn_sample_kernel, bb=bb, lvalid=lvalid, carry=carry)
        grid = (B // bb,)
        in_specs = [pl.BlockSpec((bb, L, DN_QKV), lambda b: (b, 0, AB_QKV // DN_QKV)),
                    pl.BlockSpec((bb, L, DN_W), lambda b: (b, 0, AB_ZDN // DN_W)),
                    pl.BlockSpec((bb, L, LANES), lambda b: (b, 0, 0)),
                    pl.BlockSpec((None, bb, CONV_W - 1, DN_QKV), lambda b: (j, b, 0, 0)),
                    _state_specs(st_all, j, bb, carry),
                    pl.BlockSpec((CONV_W, DN_QKV), const),
                    pl.BlockSpec((2, LANES), const),
                    pl.BlockSpec((1, DN_DV), const)]
        out_specs = [pl.BlockSpec((bb, L, DN_W), lambda b: (b, 0, 0)), _state_specs(st_all, j, bb, carry)]
        scratch = [pltpu.VMEM((bb, SUBLANES, DN_QKV), F32)]
        sem = ("parallel",)
    return pl.pallas_call(
        kern, grid=grid, in_specs=in_specs, out_specs=out_specs, out_shape=out_shape,
        scratch_shapes=scratch, input_output_aliases={} if carry else {4: 1},
        compiler_params=_params(dimension_semantics=sem), name="dn_mixer",
    )(um, um, us, cst_all, st_all, cw, prm, nrm)


def _gla_sample_kernel(q_ref, k_ref, v_ref, z_ref, us_ref, st_ref, wa_ref, ba_ref, nrm_ref, o_ref, so_ref,
                       *, bb, lvalid, carry):
    st_ref, so_ref = _own_layer(st_ref, so_ref, carry)
    C = SUBLANES
    R = bb * C
    ii = _iota2((C, C), 0)
    jj = _iota2((C, C), 1)
    nrm = nrm_ref[...]
    valid = (_iota2((R, 1), 0) & (C - 1)) < lvalid
    us = us_ref[...].reshape(R, LANES)
    glog = jax.nn.log_sigmoid(_mm(us.astype(BF16), wa_ref[...]) + ba_ref[...]) * (1.0 / GLA_GATE_NORM)
    glog = jnp.where(valid, glog, 0.0)
    ri = _iota2((R, R), 0)
    ci = _iota2((R, R), 1)
    G = _mm_hi((((ri >> 3) == (ci >> 3)) & (ci <= ri)).astype(F32), glog)
    q = q_ref[...].reshape(R, GLA_KW) * (GLA_DK ** -0.5)
    k = jnp.where(valid, k_ref[...].reshape(R, GLA_KW), 0.0)
    v = jnp.where(valid, v_ref[...].reshape(R, GLA_W), 0.0)
    z = z_ref[...].reshape(R, GLA_W)
    units = [(b, h) for b in range(bb) for h in range(GLA_HEADS)]
    U = range(len(units))
    tile = lambda x, b, h, w: x[b * C:(b + 1) * C, h * w:(h + 1) * w]
    Gu = [tile(G, b, h, GLA_DK) for b, h in units]
    qu = [tile(q, b, h, GLA_DK) for b, h in units]
    ku = [tile(k, b, h, GLA_DK) for b, h in units]
    vu = [tile(v, b, h, GLA_DV).astype(BF16) for b, h in units]
    AT = [jnp.zeros((C, C), F32) for _ in U]
    for i in range(C):
        for u in U:
            e = jnp.exp(jnp.minimum(Gu[u][i:i + 1, :] - Gu[u], 0.0))
            col = jnp.sum(ku[u] * e * qu[u][i:i + 1, :], axis=1, keepdims=True)
            AT[u] = AT[u] + jnp.where(jj == i, col, 0.0)
    AT = [jnp.where(ii <= jj, AT[u], 0.0).astype(BF16) for u in U]
    S = [st_ref[b, h] for b, h in units]
    o = [_mm((qu[u] * jnp.exp(Gu[u])).astype(BF16), S[u].astype(BF16)) + _tn(AT[u], vu[u]) for u in U]
    kend = [(ku[u] * jnp.exp(Gu[u][C - 1:C, :] - Gu[u])).astype(BF16) for u in U]
    gend = [jnp.exp(Gu[u].T[:, C - 1:C]) for u in U]
    for u, (b, h) in enumerate(units):
        so_ref[b, h] = gend[u] * S[u] + _tn(kend[u], vu[u])
    o = [o[u] * lax.rsqrt(jnp.mean(o[u] * o[u], axis=-1, keepdims=True) + EPS) * nrm for u in U]
    o = [o[u] * _silu(tile(z, b, h, GLA_DV)) for u, (b, h) in enumerate(units)]
    rows = [jnp.concatenate(o[b * GLA_HEADS:(b + 1) * GLA_HEADS], axis=1) for b in range(bb)]
    o_ref[...] = jnp.concatenate(rows, axis=0).astype(o_ref.dtype).reshape(bb, C, GLA_W)


def _gla_prompt_kernel(q_ref, k_ref, v_ref, z_ref, us_ref, st_ref, wa_ref, ba_ref, nrm_ref, o_ref, so_ref, g_ref,
                       *, tb, C):
    t = pl.program_id(1)
    H = range(GLA_HEADS)
    nch = tb // C

    @pl.when(t == 0)
    def _():
        so_ref[...] = st_ref[...]

    glog = jax.nn.log_sigmoid(_mm(us_ref[...].astype(BF16), wa_ref[...]) + ba_ref[...]) * (1.0 / GLA_GATE_NORM)
    step_in_chunk = _iota2((tb, 1), 0) & (C - 1)
    d = 1
    while d < C:
        glog = glog + jnp.where(step_in_chunk >= d, pltpu.roll(glog, d, axis=0), 0.0)
        d *= 2
    g_ref[...] = glog * LOG2E

    ii = _iota2((C, C), 0)
    jj = _iota2((C, C), 1)
    nrm = nrm_ref[...]
    hs = lambda x, h, w: x[:, h * w:(h + 1) * w]

    def scores(c):
        rows = pl.ds(pl.multiple_of(c * C, C), C)
        G = g_ref[rows, :]
        q = q_ref[rows, :] * (GLA_DK ** -0.5)
        k = k_ref[rows, :]
        nt = C // SUBLANES
        Gt = [[hs(G, h, GLA_DK)[t * SUBLANES:(t + 1) * SUBLANES] for t in range(nt)] for h in H]
        kt = [[hs(k, h, GLA_DK)[t * SUBLANES:(t + 1) * SUBLANES] for t in range(nt)] for h in H]
        acc = [[jnp.zeros((SUBLANES, C), F32) for _ in range(nt)] for _ in H]
        for i in range(C):
            for t in range(i // SUBLANES + 1):
                for h in H:
                    e = jnp.exp2(jnp.minimum(hs(G, h, GLA_DK)[i:i + 1, :] - Gt[h][t], 0.0))
                    col = jnp.sum(kt[h][t] * e * hs(q, h, GLA_DK)[i:i + 1, :], axis=1, keepdims=True)
                    acc[h][t] = acc[h][t] + jnp.where(jj[:SUBLANES] == i, col, 0.0)
        AT = [jnp.where(ii <= jj, jnp.concatenate(acc[h], axis=0), 0.0).astype(BF16) for h in H]
        gend = [jnp.exp2(hs(G, h, GLA_DK).T[:, C - 1:C]) for h in H]
        return AT, gend

    def finish(o, c):
        rows = pl.ds(pl.multiple_of(c * C, C), C)
        z = z_ref[rows, :]
        o = [o[h] * lax.rsqrt(jnp.mean(o[h] * o[h], axis=-1, keepdims=True) + EPS) * nrm for h in H]
        o = [o[h] * _silu(hs(z, h, GLA_DV)) for h in H]
        o_ref[rows, :] = jnp.concatenate(o, axis=1).astype(o_ref.dtype)

    def body(c, carry):
        AT, gend, o_prev = carry
        nxt = scores(jnp.minimum(c + 1, nch - 1))
        rows = pl.ds(pl.multiple_of(c * C, C), C)
        G = g_ref[rows, :]
        q = q_ref[rows, :] * (GLA_DK ** -0.5)
        k = k_ref[rows, :]
        v = v_ref[rows, :]
        S = [so_ref[h] for h in H]
        vh = [hs(v, h, GLA_DV).astype(BF16) for h in H]
        o = [_mm((hs(q, h, GLA_DK) * jnp.exp2(hs(G, h, GLA_DK))).astype(BF16), S[h].astype(BF16)) + _tn(AT[h], vh[h])
             for h in H]
        kend = [(hs(k, h, GLA_DK) * jnp.exp2(hs(G, h, GLA_DK)[C - 1:C, :] - hs(G, h, GLA_DK))).astype(BF16) for h in H]
        for h in H:
            so_ref[h] = gend[h] * S[h] + _tn(kend[h], vh[h])
        finish(o_prev, jnp.maximum(c - 1, 0))
        return nxt + (o,)

    AT0, gend0 = scores(0)
    _, _, o_last = lax.fori_loop(0, nch, body, (AT0, gend0, [jnp.zeros((C, GLA_DV), F32) for _ in H]))
    finish(o_last, nch - 1)


def _gla_mixer(um, us, st_all, j, wa, ba, nrm, *, bb, tb, C, lvalid):
    B, L, _ = um.shape
    const = lambda *_: (0, 0)
    carry = None
    if lvalid == L:
        assert bb == 1
        kern = functools.partial(_gla_prompt_kernel, tb=tb, C=C)
        sq = None
        scratch = [pltpu.VMEM((tb, GLA_KW), F32)]
        st_spec = pl.BlockSpec((None, None, GLA_HEADS, GLA_DK, GLA_DV), lambda b, t: (j, b, 0, 0, 0))
    else:
        assert L == SUBLANES and tb == L
        carry = (st_all.shape[0], j) if j == 0 else None
        kern = functools.partial(_gla_sample_kernel, bb=bb, lvalid=lvalid, carry=carry)
        sq = bb
        scratch = []
        st_spec = _state_specs(st_all, j, bb, carry)
    return pl.pallas_call(
        kern,
        grid=(B // bb, L // tb),
        in_specs=[pl.BlockSpec((sq, tb, GLA_KW), lambda b, t: (b, t, AB_GQ // GLA_KW)),
                  pl.BlockSpec((sq, tb, GLA_KW), lambda b, t: (b, t, AB_GK // GLA_KW)),
                  pl.BlockSpec((sq, tb, GLA_W), lambda b, t: (b, t, AB_GV // GLA_W)),
                  pl.BlockSpec((sq, tb, GLA_W), lambda b, t: (b, t, AB_ZGLA // GLA_W)),
                  pl.BlockSpec((sq, tb, LANES), lambda b, t: (b, t, 0)),
                  st_spec,
                  pl.BlockSpec((LANES, GLA_KW), const),
                  pl.BlockSpec((1, GLA_KW), const),
                  pl.BlockSpec((1, GLA_DV), const)],
        out_specs=[pl.BlockSpec((sq, tb, GLA_W), lambda b, t: (b, t, 0)), st_spec],
        out_shape=[jax.ShapeDtypeStruct((B, L, GLA_W), BF16), jax.ShapeDtypeStruct(st_all.shape, F32)],
        scratch_shapes=scratch,
        input_output_aliases={} if carry else {5: 1},
        compiler_params=_params(dimension_semantics=("parallel", "arbitrary")),
        name="gla_mixer",
    )(um, um, um, um, us, st_all, wa, ba, nrm)


SSD_GW = SSD_W // SSD_GROUPS
SSD_REP = SSD_HEADS // SSD_GROUPS
SSD_ROWS = 64


def _lane_expand(x, h0, width):
    return jnp.concatenate([jnp.broadcast_to(x[:, h0 + j:h0 + j + 1], (x.shape[0], width)) for j in range(SSD_REP)],
                           axis=1)


def _ssd_masks(C):
    ii = _iota2((SSD_ROWS, SSD_GW), 0)
    jj = _iota2((SSD_ROWS, SSD_GW), 1) & (SSD_ROWS - 1)
    sh = C.bit_length() - 1
    negc = jnp.where(((ii >> sh) == (jj >> sh)) & (jj <= ii), 0.0, -1e30)
    ri = _iota2((SSD_GW, SSD_GW), 0)
    ci = _iota2((SSD_GW, SSD_GW), 1)
    bd = ((ri >> 6) == (ci >> 6)).astype(F32)
    return negc, bd


def _ssd_intra(act, G, GT, dt, gi, negc, bd):
    h0 = gi * SSD_REP
    xs = act[:, gi * SSD_GW:(gi + 1) * SSD_GW]
    bm = act[:, SSD_W + gi * SSD_STATE:SSD_W + (gi + 1) * SSD_STATE].astype(BF16)
    cm = act[:, SSD_W + (SSD_GROUPS + gi) * SSD_STATE:SSD_W + (SSD_GROUPS + gi + 1) * SSD_STATE].astype(BF16)
    gcc = _lane_expand(G, h0, SSD_HEADDIM)
    grc = jnp.concatenate([GT[h0 + j:h0 + j + 1, :] for j in range(SSD_REP)], axis=1)
    vcat = xs * _lane_expand(dt, h0, SSD_HEADDIM)
    cbm = _nt(cm, bm)
    acat = (jnp.concatenate([cbm] * SSD_REP, axis=1) * jnp.exp((gcc - grc) + negc)).astype(BF16)
    vbd = (jnp.concatenate([vcat] * SSD_REP, axis=0) * bd).astype(BF16)
    return _mm(acat, vbd), xs, vcat, gcc, bm, cm


def _ssd_finish(o, xs, z, dsk, nrm):
    y = (o + dsk * xs) * _silu(z)
    return y * lax.rsqrt(jnp.mean(y * y, axis=-1, keepdims=True) + EPS) * nrm


def _ssd_prompt_kernel(z_ref, xbc_ref, us_ref, cst_ref, st_ref, cw_ref, cb_ref, prm_ref, nrm_ref, o_ref, so_ref,
                       halo_ref, scat_ref, *, tb):
    t = pl.program_id(1)
    C = SSD_ROWS
    cat_heads = lambda f, gi: jnp.concatenate([f(gi * SSD_REP + j) for j in range(SSD_REP)], axis=1)

    @pl.when(t == 0)
    def _():
        for gi in range(SSD_GROUPS):
            scat_ref[gi] = cat_heads(lambda h: st_ref[h], gi)
        halo_ref[0:5, :] = jnp.zeros((5, SSD_XBC), F32)
        halo_ref[5:8, :] = cst_ref[...]

    negc, bd = _ssd_masks(C)
    tril = (_iota2((C, C), 1) <= _iota2((C, C), 0)).astype(F32)
    lane = _iota2((1, LANES), 1)
    hmask = lane < SSD_HEADS
    neg_a = -jnp.exp(prm_ref[0:1, :])
    bias = prm_ref[1:2, :]
    dsk = [_lane_expand(prm_ref[2:3, :], gi * SSD_REP, SSD_HEADDIM) for gi in range(SSD_GROUPS)]
    cw = cw_ref[...]
    cb = cb_ref[...]
    nrm = nrm_ref[...]
    groups = range(SSD_GROUPS)

    def chunk(c, carry):
        r0 = pl.multiple_of(c * C, C)
        x = xbc_ref[pl.ds(r0, C), :]
        prev = xbc_ref[pl.ds(pl.multiple_of(jnp.maximum(r0 - SUBLANES, 0), SUBLANES), SUBLANES), :]
        halo = jnp.where(c == 0, halo_ref[...], prev)
        act = _silu(_conv_rows(halo, x, cw, cb))
        us = us_ref[pl.ds(r0, C), :]
        dt = jnp.where(hmask, jax.nn.softplus(us + bias), 0.0)
        G = _mm_hi(tril, neg_a * dt)
        GT = G.T
        z = z_ref[pl.ds(r0, C), :]
        parts = [_ssd_intra(act, G, GT, dt, gi, negc, bd) for gi in groups]
        S = [scat_ref[gi] for gi in groups]
        inter = [_mm(parts[gi][5], S[gi].astype(BF16)) for gi in groups]
        for gi in groups:
            o_in, xs, vcat, gcc, bm, cm = parts[gi]
            glc = gcc[C - 1:C, :]
            w = (vcat * jnp.exp(glc - gcc)).astype(BF16)
            scat_ref[gi] = jnp.exp(glc) * S[gi] + _tn(bm, w)
            y = _ssd_finish(jnp.exp(gcc) * inter[gi] + o_in, xs, z[:, gi * SSD_GW:(gi + 1) * SSD_GW], dsk[gi],
                            nrm[:, gi * SSD_GW:(gi + 1) * SSD_GW])
            o_ref[pl.ds(r0, C), gi * SSD_GW:(gi + 1) * SSD_GW] = y.astype(o_ref.dtype)
        return carry

    lax.fori_loop(0, tb // C, chunk, 0)
    halo_ref[...] = xbc_ref[tb - SUBLANES:tb, :]

    @pl.when(t == pl.num_programs(1) - 1)
    def _():
        for gi in groups:
            s = scat_ref[gi]
            for j in range(SSD_REP):
                so_ref[gi * SSD_REP + j] = s[:, j * SSD_HEADDIM:(j + 1) * SSD_HEADDIM]


def _ssd_sample_kernel(z_ref, xbc_ref, us_ref, cst_ref, st_ref, cw_ref, cb_ref, prm_ref, nrm_ref, o_ref, so_ref,
                       halo_ref, *, bb, lvalid, carry):
    st_ref, so_ref = _own_layer(st_ref, so_ref, carry)
    C = SUBLANES
    assert bb * C == SSD_ROWS
    halo_ref[:, 0:5, :] = jnp.zeros((bb, 5, SSD_XBC), F32)
    halo_ref[:, 5:8, :] = cst_ref[...]
    cw = cw_ref[...]
    cb = cb_ref[...]
    nrm = nrm_ref[...]
    act = jnp.concatenate([_silu(_conv_rows(halo_ref[b], xbc_ref[b], cw, cb)) for b in range(bb)], axis=0)

    negc, bd = _ssd_masks(C)
    lane = _iota2((1, LANES), 1)
    neg_a = -jnp.exp(prm_ref[0:1, :])
    bias = prm_ref[1:2, :]
    us = us_ref[...].reshape(SSD_ROWS, LANES)
    valid = (_iota2((SSD_ROWS, 1), 0) & (C - 1)) < lvalid
    dt = jnp.where((lane < SSD_HEADS) & valid, jax.nn.softplus(us + bias), 0.0)
    ii = _iota2((SSD_ROWS, SSD_ROWS), 0)
    jj = _iota2((SSD_ROWS, SSD_ROWS), 1)
    same = (ii >> 3) == (jj >> 3)
    g = neg_a * dt
    G = _mm_hi((same & (jj <= ii)).astype(F32), g)
    GL = _mm_hi(same.astype(F32), g)
    GT = G.T
    z = z_ref[...].reshape(SSD_ROWS, SSD_W)
    groups = range(SSD_GROUPS)
    parts = [_ssd_intra(act, G, GT, dt, gi, negc, bd) for gi in groups]
    units = [(gi, b) for gi in groups for b in range(bb)]
    rows = lambda b: slice(b * C, (b + 1) * C)
    glc = [_lane_expand(GL, gi * SSD_REP, SSD_HEADDIM) for gi in groups]
    w = [(parts[gi][2] * jnp.exp(glc[gi] - parts[gi][3])).astype(BF16) for gi in groups]
    egl = [jnp.exp(glc[gi]) for gi in groups]
    S = [jnp.concatenate([st_ref[b, gi * SSD_REP + j] for j in range(SSD_REP)], axis=1) for gi, b in units]
    inter = [_mm(parts[gi][5][rows(b)], S[u].astype(BF16)) for u, (gi, b) in enumerate(units)]
    s_new = [egl[gi][b * C:b * C + 1, :] * S[u] + _tn(parts[gi][4][rows(b)], w[gi][rows(b)])
             for u, (gi, b) in enumerate(units)]
    for u, (gi, b) in enumerate(units):
        for j in range(SSD_REP):
            so_ref[b, gi * SSD_REP + j] = s_new[u][:, j * SSD_HEADDIM:(j + 1) * SSD_HEADDIM]
    outs = []
    for gi in groups:
        o_in, xs, vcat, gcc, bm, cm = parts[gi]
        o = jnp.exp(gcc) * jnp.concatenate(inter[gi * bb:(gi + 1) * bb], axis=0) + o_in
        dsk = _lane_expand(prm_ref[2:3, :], gi * SSD_REP, SSD_HEADDIM)
        outs.append(_ssd_finish(o, xs, z[:, gi * SSD_GW:(gi + 1) * SSD_GW], dsk, nrm[:, gi * SSD_GW:(gi + 1) * SSD_GW]))
    o_ref[...] = jnp.concatenate(outs, axis=1).astype(o_ref.dtype).reshape(bb, C, SSD_W)


def _ssd_mixer(um, us, cst_all, st_all, j, cw, cb, prm, nrm, *, bb, tb, lvalid):
    B, L, _ = um.shape
    const = lambda *_: (0, 0)
    out_shape = [jax.ShapeDtypeStruct((B, L, SSD_W), BF16), jax.ShapeDtypeStruct(st_all.shape, F32)]
    st_blk = (SSD_HEADS, SSD_STATE, SSD_HEADDIM)
    w_specs = [pl.BlockSpec((CONV_W, SSD_XBC), const), pl.BlockSpec((1, SSD_XBC), const),
               pl.BlockSpec((3, LANES), const), pl.BlockSpec((1, SSD_W), const)]
    carry = None
    if lvalid == L:
        assert bb == 1 and tb % SSD_ROWS == 0
        kern = functools.partial(_ssd_prompt_kernel, tb=tb)
        grid = (B, L // tb)
        in_specs = [pl.BlockSpec((None, tb, SSD_W), lambda b, t: (b, t, CD_Z // SSD_W)),
                    pl.BlockSpec((None, tb, SSD_XBC), lambda b, t: (b, t, CD_XBC // SSD_XBC)),
                    pl.BlockSpec((None, tb, LANES), lambda b, t: (b, t, 0)),
                    pl.BlockSpec((None, None, CONV_W - 1, SSD_XBC), lambda b, t: (j, b, 0, 0)),
                    pl.BlockSpec((None, None) + st_blk, lambda b, t: (j, b, 0, 0, 0))] + w_specs
        out_specs = [pl.BlockSpec((None, tb, SSD_W), lambda b, t: (b, t, 0)),
                     pl.BlockSpec((None, None) + st_blk, lambda b, t: (j, b, 0, 0, 0))]
        scratch = [pltpu.VMEM((SUBLANES, SSD_XBC), F32), pltpu.VMEM((SSD_GROUPS, SSD_STATE, SSD_GW), F32)]
        sem = ("parallel", "arbitrary")
    else:
        assert L == SUBLANES
        carry = (st_all.shape[0], j) if j == 0 else None
        kern = functools.partial(_ssd_sample_kernel, bb=bb, lvalid=lvalid, carry=carry)
        grid = (B // bb,)
        in_specs = [pl.BlockSpec((bb, L, SSD_W), lambda b: (b, 0, CD_Z // SSD_W)),
                    pl.BlockSpec((bb, L, SSD_XBC), lambda b: (b, 0, CD_XBC // SSD_XBC)),
                    pl.BlockSpec((bb, L, LANES), lambda b: (b, 0, 0)),
                    pl.BlockSpec((None, bb, CONV_W - 1, SSD_XBC), lambda b: (j, b, 0, 0)),
                    _state_specs(st_all, j, bb, carry)] + w_specs
        out_specs = [pl.BlockSpec((bb, L, SSD_W), lambda b: (b, 0, 0)), _state_specs(st_all, j, bb, carry)]
        scratch = [pltpu.VMEM((bb, SUBLANES, SSD_XBC), F32)]
        sem = ("parallel",)
    return pl.pallas_call(
        kern, grid=grid, in_specs=in_specs, out_specs=out_specs, out_shape=out_shape,
        scratch_shapes=scratch, input_output_aliases={} if carry else {4: 1},
        compiler_params=_params(dimension_semantics=sem), name="ssd_mixer",
    )(um, um, us, cst_all, st_all, cw, cb, prm, nrm)


def _lru_kernel(z_ref, x_ref, cst_ref, st_ref, cw_ref, cb_ref, wbd_ref, prm_ref, o_ref, so_ref, halo_ref, carry_ref,
                *, bb, tb, lvalid, ltotal):
    t = pl.program_id(1)

    @pl.when(t == 0)
    def _():
        carry_ref[...] = st_ref[...]
        halo_ref[:, 0:5, :] = jnp.zeros((bb, 5, LRU_W), F32)
        halo_ref[:, 5:8, :] = cst_ref[...]

    cw = cw_ref[...]
    cb = cb_ref[...]
    b_a = prm_ref[0:1, :]
    b_x = prm_ref[1:2, :]
    sp = jax.nn.softplus(-prm_ref[2:3, :])
    row = _iota2((tb, 1), 0)
    last_t, last_r = (lvalid - 1) // tb, (lvalid - 1) % tb

    def per_batch(bi, carry):
        x = x_ref[bi]
        xc = _conv_rows(halo_ref[bi], x, cw, cb)
        halo_ref[bi] = x[tb - SUBLANES:tb, :]
        xb = xc.astype(BF16)
        rs, is_ = [], []
        for cg in range(LRU_W // LANES):
            ri = _mm(xb[:, cg * LANES:(cg + 1) * LANES], wbd_ref[cg])
            rs.append(ri[:, :LANES])
            is_.append(ri[:, LANES:])
        r = jax.nn.sigmoid(jnp.concatenate(rs, axis=1) + b_a)
        i = jax.nn.sigmoid(jnp.concatenate(is_, axis=1) + b_x)
        log_a = (-LRU_C) * r * sp
        a = jnp.exp(log_a)
        b = jnp.sqrt(1.0 - jnp.exp(2.0 * log_a)) * (i * xc)
        d = 1
        while d < tb:
            m = row >= d
            a_s = pltpu.roll(a, d, axis=0)
            b_s = pltpu.roll(b, d, axis=0)
            b = jnp.where(m, a * b_s + b, b)
            a = jnp.where(m, a * a_s, a)
            d *= 2
        hs = a * carry_ref[bi] + b
        carry_ref[bi] = hs[tb - 1:tb, :]

        @pl.when(t == last_t)
        def _():
            so_ref[bi] = hs[last_r:last_r + 1, :]

        o_ref[bi] = (hs * _silu(z_ref[bi])).astype(o_ref.dtype)
        return carry

    lax.fori_loop(0, bb, per_batch, 0)


def _lru_mixer(um, cst_all, st_all, j, cw, cb, wbd, prm, *, bb, tb, lvalid):
    B, L, _ = um.shape
    kern = functools.partial(_lru_kernel, bb=bb, tb=tb, lvalid=lvalid, ltotal=L)
    const = lambda b, t: (0, 0)
    return pl.pallas_call(
        kern,
        grid=(B // bb, L // tb),
        in_specs=[pl.BlockSpec((bb, tb, LRU_W), lambda b, t: (b, t, CD_ZLRU // LRU_W)),
                  pl.BlockSpec((bb, tb, LRU_W), lambda b, t: (b, t, CD_XLRU // LRU_W)),
                  pl.BlockSpec((None, bb, CONV_W - 1, LRU_W), lambda b, t: (j, b, 0, 0)),
                  pl.BlockSpec((None, bb, 1, LRU_W), lambda b, t: (j, b, 0, 0)),
                  pl.BlockSpec((CONV_W, LRU_W), const),
                  pl.BlockSpec((1, LRU_W), const),
                  pl.BlockSpec((LRU_W // LANES, LANES, 2 * LANES), lambda b, t: (0, 0, 0)),
                  pl.BlockSpec((3, LRU_W), const)],
        out_specs=[pl.BlockSpec((bb, tb, LRU_W), lambda b, t: (b, t, 0)),
                   pl.BlockSpec((bb, 1, LRU_W), lambda b, t: (b, 0, 0))],
        out_shape=[jax.ShapeDtypeStruct((B, L, LRU_W), BF16),
                   jax.ShapeDtypeStruct((B, 1, LRU_W), F32)],
        scratch_shapes=[pltpu.VMEM((bb, SUBLANES, LRU_W), F32), pltpu.VMEM((bb, 1, LRU_W), F32)],
        compiler_params=_params(dimension_semantics=("parallel", "arbitrary")),
        name="lru_mixer",
    )(um, um, cst_all, st_all, cw, cb, wbd, prm)


def _lane_row(*pieces):
    v = jnp.concatenate([p.reshape(-1) for p in pieces])
    return jnp.pad(v, (0, LANES - v.shape[0])).reshape(1, LANES)


def _cols(w, sizes, order, pad_to=None):
    offs = [sum(sizes[:i]) for i in range(len(sizes))]
    out = jnp.concatenate([w[..., offs[i]:offs[i] + sizes[i]] for i in order], axis=-1)
    if pad_to is not None:
        out = jnp.pad(out, [(0, 0)] * (out.ndim - 1) + [(0, pad_to - out.shape[-1])])
    return out


def _prep_weights(w):
    n_even = w['ab_w_in'].shape[0]
    n_odd = w['cd_w_in'].shape[0]
    D = w['ab_w_in'].shape[1]
    P = {'ab': [], 'cd': []}
    P['ab_w_main'] = _cols(w['ab_w_in'], AB_SIZES, (0, 3, 4, 5, 6, 8)).astype(BF16)
    P['ab_w_small'] = _cols(w['ab_w_in'], AB_SIZES, (1, 2, 7), LANES).astype(BF16)
    P['ab_w_out'] = w['ab_w_out'].astype(BF16)
    for j in range(n_even):
        P['ab'].append(dict(
            dn_cw=w['dn_conv_w'][j],
            dn_prm=jnp.concatenate([_lane_row(jnp.zeros((DN_HEADS,), F32), w['dn_a_log'][j]),
                                    _lane_row(jnp.zeros((DN_HEADS,), F32), w['dn_dt_bias'][j])], axis=0),
            dn_norm=w['dn_norm'][j].reshape(1, DN_DV),
            gla_wa=jnp.pad(w['gla_wa2'][j], ((2 * DN_HEADS, LANES - 2 * DN_HEADS - GLA_RANK), (0, 0))).astype(BF16),
            gla_ba=w['gla_ba'][j].reshape(1, GLA_KW),
            gla_norm=w['gla_norm'][j].reshape(1, GLA_DV),
        ))
    P['cd_w_main'] = _cols(w['cd_w_in'], CD_SIZES, (0, 3, 4, 1)).astype(BF16)
    P['cd_w_small'] = _cols(w['cd_w_in'], CD_SIZES, (2,), LANES).astype(BF16)
    P['cd_w_out'] = w['cd_w_out'].astype(BF16)
    eye2 = jnp.eye(2, dtype=F32)
    for j in range(n_odd):
        wa = w['lru_wa'][j].reshape(LRU_W // LANES, 2, LRU_BS, LRU_BS)
        wx = w['lru_wx'][j].reshape(LRU_W // LANES, 2, LRU_BS, LRU_BS)
        bd = lambda m: jnp.einsum('gbcd,be->gbced', m, eye2).reshape(LRU_W // LANES, LANES, LANES)
        P['cd'].append(dict(
            ssd_cw=w['ssd_conv_w'][j],
            ssd_cb=w['ssd_conv_b'][j].reshape(1, SSD_XBC),
            ssd_prm=jnp.concatenate([_lane_row(w['ssd_a_log'][j]), _lane_row(w['ssd_dt_bias'][j]),
                                     _lane_row(w['ssd_d'][j])], axis=0),
            ssd_norm=w['ssd_norm'][j].reshape(1, SSD_W),
            lru_cw=w['lru_conv_w'][j],
            lru_cb=w['lru_conv_b'][j].reshape(1, LRU_W),
            lru_wbd=jnp.concatenate([bd(wa), bd(wx)], axis=2).astype(BF16),
            lru_prm=jnp.stack([w['lru_ba'][j], w['lru_bx'][j], w['lru_lambda'][j]], axis=0),
        ))
    P['norm_g'] = w['norm_g'].reshape(-1, 1, D)
    P['ple_norm'] = w['ple_norm'].reshape(-1, 1, D)
    P['ple_w_gate'] = w['ple_w_gate'].astype(BF16)
    P['ple_w_proj'] = w['ple_w_proj'].astype(BF16)
    P['final_norm'] = w['final_norm'].reshape(1, D)
    return P


def _trunk(x, p, states, P, *, lvalid, bb, tb, c_dn, c_gla, c_ssd):
    B, L, D = x.shape
    depth = P['norm_g'].shape[0]
    st_dn, st_dnc, st_gla, st_ssd, st_ssdc, st_lru, st_lruc = states
    st_lru = st_lru.reshape(st_lru.shape[0], B, 1, LRU_W)
    dnc_new, ssdc_new, lru_new, lruc_new = [], [], [], []
    T = B * L
    h = x.reshape(T, D)
    p_all = p.reshape(depth, T, PLE_DIM)
    lo = lvalid - (CONV_W - 1)
    for li in range(depth):
        j = li // 2
        if li % 2 == 0:
            W = P['ab'][j]
            um, us = _inproj(h, P['norm_g'], li, P['ab_w_main'], P['ab_w_small'], j)
            um = um.reshape(B, L, AB_MAIN)
            us = us.reshape(B, L, LANES)
            oa, st_dn = _dn_mixer(um, us, st_dnc, st_dn, j, W['dn_cw'], W['dn_prm'], W['dn_norm'],
                                  bb=bb, tb=tb, C=c_dn, lvalid=lvalid)
            ob, st_gla = _gla_mixer(um, us, st_gla, j, W['gla_wa'], W['gla_ba'], W['gla_norm'],
                                    bb=bb, tb=tb, C=c_gla, lvalid=lvalid)
            dnc_new.append(um[:, lo:lvalid, AB_QKV:AB_QKV + DN_QKV])
        else:
            W = P['cd'][j]
            um, us = _inproj(h, P['norm_g'], li, P['cd_w_main'], P['cd_w_small'], j)
            um = um.reshape(B, L, CD_MAIN)
            us = us.reshape(B, L, LANES)
            oa, st_ssd = _ssd_mixer(um, us, st_ssdc, st_ssd, j, W['ssd_cw'], W['ssd_cb'], W['ssd_prm'],
                                    W['ssd_norm'], bb=bb, tb=tb, lvalid=lvalid)
            ob, lru_j = _lru_mixer(um, st_lruc, st_lru, j, W['lru_cw'], W['lru_cb'],
                                   W['lru_wbd'], W['lru_prm'], bb=bb, tb=tb, lvalid=lvalid)
            ssdc_new.append(um[:, lo:lvalid, CD_XBC:CD_XBC + SSD_XBC])
            lru_new.append(lru_j.reshape(B, LRU_W))
            lruc_new.append(um[:, lo:lvalid, CD_XLRU:CD_XLRU + LRU_W])
        h = _outproj(oa.reshape(T, -1), ob.reshape(T, -1), h, p_all, li, P['ab_w_out' if li % 2 == 0 else 'cd_w_out'], j,
                     P['ple_norm'], P['ple_w_gate'], P['ple_w_proj'], P['final_norm'], final=(li == depth - 1))
    new_states = [st_dn, jnp.stack(dnc_new), st_gla, st_ssd, jnp.stack(ssdc_new), jnp.stack(lru_new),
                  jnp.stack(lruc_new)]
    return h.reshape(B, L, D), new_states


def _pick_tb(L):
    tb = min(L, 512)
    assert L % tb == 0
    return tb


def kernel(x_prompt, x_sample, state_dn, state_dn_conv, state_gla, state_ssd, state_ssd_conv, state_lru, state_lru_conv, p_prompt, p_sample, norm_g, final_norm, ab_w_in, dn_conv_w, dn_a_log, dn_dt_bias, dn_norm, gla_wa2, gla_ba, gla_norm, ab_w_out, cd_w_in, ssd_conv_w, ssd_conv_b, ssd_a_log, ssd_dt_bias, ssd_d, ssd_norm, lru_conv_w, lru_conv_b, lru_wa, lru_ba, lru_wx, lru_bx, lru_lambda, cd_w_out, ple_w_proj, ple_norm, ple_w_gate):
    w = {
        'norm_g': norm_g, 'final_norm': final_norm,
        'ab_w_in': ab_w_in, 'dn_conv_w': dn_conv_w, 'dn_a_log': dn_a_log, 'dn_dt_bias': dn_dt_bias,
        'dn_norm': dn_norm, 'gla_wa2': gla_wa2, 'gla_ba': gla_ba, 'gla_norm': gla_norm, 'ab_w_out': ab_w_out,
        'cd_w_in': cd_w_in, 'ssd_conv_w': ssd_conv_w, 'ssd_conv_b': ssd_conv_b, 'ssd_a_log': ssd_a_log,
        'ssd_dt_bias': ssd_dt_bias, 'ssd_d': ssd_d, 'ssd_norm': ssd_norm,
        'lru_conv_w': lru_conv_w, 'lru_conv_b': lru_conv_b, 'lru_wa': lru_wa, 'lru_ba': lru_ba,
        'lru_wx': lru_wx, 'lru_bx': lru_bx, 'lru_lambda': lru_lambda, 'cd_w_out': cd_w_out,
        'ple_w_proj': ple_w_proj, 'ple_norm': ple_norm, 'ple_w_gate': ple_w_gate,
    }
    P = _prep_weights(w)
    n_even, n_odd = ab_w_in.shape[0], cd_w_in.shape[0]

    Bp, Lp, _ = x_prompt.shape
    assert Lp >= CONV_W - 1 and Lp % 64 == 0
    zero_states = (
        jnp.zeros((n_even, Bp, DN_HEADS, DN_DK, DN_DV), F32),
        jnp.zeros((n_even, Bp, CONV_W - 1, DN_QKV), F32),
        jnp.zeros((n_even, Bp, GLA_HEADS, GLA_DK, GLA_DV), F32),
        jnp.zeros((n_odd, Bp, SSD_HEADS, SSD_STATE, SSD_HEADDIM), F32),
        jnp.zeros((n_odd, Bp, CONV_W - 1, SSD_XBC), F32),
        jnp.zeros((n_odd, Bp, LRU_W), F32),
        jnp.zeros((n_odd, Bp, CONV_W - 1, LRU_W), F32),
    )
    y_p, st_p = _trunk(x_prompt, p_prompt, zero_states, P, lvalid=Lp, bb=1, tb=_pick_tb(Lp),
                       c_dn=64, c_gla=16, c_ssd=64)

    Bs, Ls, _ = x_sample.shape
    assert CONV_W - 1 <= Ls <= SUBLANES
    pad = SUBLANES - Ls
    xs = jnp.pad(x_sample, ((0, 0), (0, pad), (0, 0)))
    ps = jnp.pad(p_sample, ((0, 0), (0, 0), (0, pad), (0, 0)))
    sample_states = (state_dn, state_dn_conv, state_gla, state_ssd, state_ssd_conv, state_lru, state_lru_conv)
    bb_s = 8 if Bs % 8 == 0 else 1
    y_s, st_s = _trunk(xs, ps, sample_states, P, lvalid=Ls, bb=bb_s, tb=SUBLANES,
                       c_dn=SUBLANES, c_gla=SUBLANES, c_ssd=SUBLANES)
    return (y_p, y_s[:, :Ls], *st_p, *st_s)
```

```python
import functools

import jax
import jax.numpy as jnp
from jax import lax
from jax.experimental import pallas as pl
from jax.experimental.pallas import tpu as pltpu

F32 = jnp.float32
BF16 = jnp.bfloat16
HI = lax.Precision.HIGHEST

EPS = 1e-6
LOG2E = 1.4426950408889634
CONV_W = 4
PLE_DIM = 256
DN_HEADS, DN_DK, DN_DV = 8, 128, 128
DN_W = DN_HEADS * DN_DV
DN_QKV = 2 * DN_HEADS * DN_DK + DN_W
GLA_HEADS, GLA_DK, GLA_DV = 4, 128, 256
GLA_KW = GLA_HEADS * GLA_DK
GLA_W = GLA_HEADS * GLA_DV
GLA_RANK = 16
GLA_GATE_NORM = 16.0
AB_SIZES = (DN_QKV, DN_HEADS, DN_HEADS, DN_W, GLA_KW, GLA_KW, GLA_W, GLA_RANK, GLA_W)
SSD_HEADS, SSD_HEADDIM, SSD_GROUPS, SSD_STATE = 16, 64, 4, 64
SSD_W = SSD_HEADS * SSD_HEADDIM
SSD_XBC = SSD_W + 2 * SSD_GROUPS * SSD_STATE
LRU_W, LRU_BLOCKS = 1024, 16
LRU_BS = LRU_W // LRU_BLOCKS
LRU_C = 8.0
CD_SIZES = (SSD_W, SSD_XBC, SSD_HEADS, LRU_W, LRU_W)

LANES = 128
SUBLANES = 8
VMEM_LIMIT = 52 * 1024 * 1024

AB_QKV, AB_ZDN, AB_GQ, AB_GK, AB_GV, AB_ZGLA = 0, 3072, 4096, 4608, 5120, 6144
AB_MAIN = 7168
CD_Z, CD_ZLRU, CD_XLRU, CD_XBC = 0, 1024, 2048, 3072
CD_MAIN = 4608


def _silu(x):
    return x * jax.nn.sigmoid(x)


def _nt(a, b):
    return lax.dot_general(a, b, (((1,), (1,)), ((), ())), preferred_element_type=F32)


def _tn(a, b):
    return lax.dot_general(a, b, (((0,), (0,)), ((), ())), preferred_element_type=F32)


def _mm(a, b):
    return jnp.dot(a, b, preferred_element_type=F32)


def _mm_hi(a, b):
    return jnp.dot(a, b, preferred_element_type=F32, precision=HI)


def _iota2(shape, dim):
    return lax.broadcasted_iota(jnp.int32, shape, dim)


def _conv_rows(halo, x, w, bias=None):
    cat = jnp.concatenate([halo, x], axis=0)
    out = x * w[3:4]
    for s in (1, 2, 3):
        out = out + pltpu.roll(cat, s, axis=0)[SUBLANES:] * w[3 - s:4 - s]
    if bias is not None:
        out = out + bias
    return out


def _params(**kw):
    return pltpu.CompilerParams(vmem_limit_bytes=VMEM_LIMIT, **kw)


def _own_layer(st_ref, so_ref, carry):
    if carry is None:
        return st_ref, so_ref
    n, jl = carry
    for l in range(n):
        if l != jl:
            so_ref[l] = st_ref[l]
    return st_ref.at[jl], so_ref.at[jl]


def _state_specs(st_all, j, bb, carry):
    tail = st_all.shape[2:]
    zeros = (0,) * len(tail)
    if carry:
        return pl.BlockSpec((st_all.shape[0], bb) + tail, lambda b, *_: (0, b) + zeros)
    return pl.BlockSpec((None, bb) + tail, lambda b, *_: (j, b) + zeros)


def _inproj_kernel(h_ref, g_ref, wm_ref, ws_ref, um_ref, us_ref, xn_ref):
    @pl.when(pl.program_id(1) == 0)
    def _():
        x = h_ref[...]
        r = lax.rsqrt(jnp.mean(x * x, axis=-1, keepdims=True) + EPS)
        xn = (x * r * g_ref[...]).astype(BF16)
        xn_ref[...] = xn
        us_ref[...] = _nt(xn, ws_ref[...].astype(BF16))

    um_ref[...] = _nt(xn_ref[...], wm_ref[...])


def _inproj(h, g_all, li, wm_all, ws_all, j):
    T, D = h.shape
    Nm = wm_all.shape[1]
    tm = min(T, 1024)
    tn = 1024 if Nm % 1024 == 0 else 1536
    return pl.pallas_call(
        _inproj_kernel,
        grid=(T // tm, Nm // tn),
        in_specs=[pl.BlockSpec((tm, D), lambda i, c: (i, 0)),
                  pl.BlockSpec((None, 1, D), lambda i, c: (li, 0, 0)),
                  pl.BlockSpec((None, tn, D), lambda i, c: (j, c, 0)),
                  pl.BlockSpec((None, LANES, D), lambda i, c: (j, 0, 0))],
        out_specs=[pl.BlockSpec((tm, tn), lambda i, c: (i, c)),
                   pl.BlockSpec((tm, LANES), lambda i, c: (i, 0))],
        out_shape=[jax.ShapeDtypeStruct((T, Nm), F32), jax.ShapeDtypeStruct((T, LANES), F32)],
        scratch_shapes=[pltpu.VMEM((tm, D), BF16)],
        compiler_params=_params(dimension_semantics=("parallel", "arbitrary")),
        name="inproj",
    )(h, g_all, wm_all, ws_all)


def _outproj_kernel(oa_ref, ob_ref, h_ref, p_ref, wa_ref, wb_ref, pg_ref, wg_ref, wp_ref, fg_ref, out_ref, *, final):
    h1 = h_ref[...] + _mm(oa_ref[...], wa_ref[...]) + _mm(ob_ref[...], wb_ref[...])
    r = lax.rsqrt(jnp.mean(h1 * h1, axis=-1, keepdims=True) + EPS)
    hn = (h1 * r * pg_ref[...]).astype(BF16)
    gate = jax.nn.sigmoid(_mm(hn, wg_ref[...]))
    pp = _mm(p_ref[...].astype(BF16), wp_ref[...])
    h2 = h1 + gate * pp
    if final:
        r2 = lax.rsqrt(jnp.mean(h2 * h2, axis=-1, keepdims=True) + EPS)
        h2 = h2 * r2 * fg_ref[...]
    out_ref[...] = h2


def _outproj(oa, ob, h, p_all, li, w_out_all, j, pg_all, wg_all, wp_all, fg, final):
    T, D = h.shape
    Wa, Wb = oa.shape[1], ob.shape[1]
    assert Wa == Wb
    tm = min(T, 512)
    resident = lambda shape, idx: pl.BlockSpec(shape, lambda i: idx, pipeline_mode=pl.Buffered(1))
    return pl.pallas_call(
        functools.partial(_outproj_kernel, final=final),
        grid=(T // tm,),
        in_specs=[pl.BlockSpec((tm, Wa), lambda i: (i, 0)),
                  pl.BlockSpec((tm, Wb), lambda i: (i, 0)),
                  pl.BlockSpec((tm, D), lambda i: (i, 0)),
                  pl.BlockSpec((None, tm, PLE_DIM), lambda i: (li, i, 0)),
                  resident((None, Wa, D), (j, 0, 0)),
                  resident((None, Wb, D), (j, 1, 0)),
                  resident((None, 1, D), (li, 0, 0)),
                  resident((None, D, D), (li, 0, 0)),
                  resident((None, PLE_DIM, D), (li, 0, 0)),
                  resident((1, D), (0, 0))],
        out_specs=pl.BlockSpec((tm, D), lambda i: (i, 0)),
        out_shape=jax.ShapeDtypeStruct((T, D), F32),
        compiler_params=_params(dimension_semantics=("parallel",)),
        name="outproj",
    )(oa, ob, h, p_all, w_out_all, w_out_all, pg_all, wg_all, wp_all, fg)


R_BD = 128


def _bd_masks(C):
    ii = _iota2((R_BD, R_BD), 0)
    jj = _iota2((R_BD, R_BD), 1)
    sh = C.bit_length() - 1
    lower = ((ii >> sh) == (jj >> sh)) & (jj <= ii)
    neg = jnp.where(lower, 0.0, -1e30)
    eye = (ii == jj).astype(F32)
    return neg, 1.0 - eye, eye


def _dn_groups(Q, K, V, beta, gc, gr, neg, offdiag, eye, C):
    n = range(len(Q))
    Kb = [K[g].astype(BF16) for g in n]
    KK = [_nt(Kb[g], Kb[g]) for g in n]
    QK = [_nt(Q[g].astype(BF16), Kb[g]) for g in n]
    dm = [jnp.exp((gc[g] - gr[g]) + neg) for g in n]
    A = [(beta[g] * KK[g]) * (dm[g] * offdiag) for g in n]
    T = [eye - A[g] for g in n]
    X = A
    p = 2
    while p < C:
        Xb = [X[g].astype(BF16) for g in n]
        X = [_mm(Xb[g], Xb[g]) for g in n]
        T = [T[g] + _mm(T[g].astype(BF16), X[g].astype(BF16)) for g in n]
        p *= 2
    eg = [jnp.exp(gc[g]) for g in n]
    rhs = [jnp.concatenate([beta[g] * V[g], (beta[g] * eg[g]) * K[g]], axis=1).astype(BF16) for g in n]
    sol = [_mm(T[g].astype(BF16), rhs[g]) for g in n]
    return ([sol[g][:, :DN_DV] for g in n], [sol[g][:, DN_DV:] for g in n],
            [QK[g] * dm[g] for g in n], [Q[g] * eg[g] for g in n])


def _l2n(x):
    return x * lax.rsqrt(jnp.sum(x * x, axis=-1, keepdims=True) + EPS)


def _dn_apply(S, u0, wk, qkd, qg, K, gl, gc, C):
    n = range(len(S))
    nu = len(S[0])
    blk = lambda x, i: x[i * C:(i + 1) * C]
    r = [[None] * nu for _ in n]
    for i in range(nu):
        for g in n:
            lhs = jnp.concatenate([blk(wk[g], i), blk(qg[g], i)], axis=0).astype(BF16)
            r[g][i] = _mm(lhs, S[g][i].astype(BF16))
    Ub = [jnp.concatenate([blk(u0[g], i) - r[g][i][:C] for i in range(nu)], axis=0).astype(BF16) for g in n]
    O = [jnp.concatenate([r[g][i][C:] for i in range(nu)], axis=0) + _mm(qkd[g].astype(BF16), Ub[g]) for g in n]
    kend = [(K[g] * jnp.exp(gl[g] - gc[g])).astype(BF16) for g in n]
    egl = [jnp.exp(gl[g]) for g in n]
    S_new = [[None] * nu for _ in n]
    for i in range(nu):
        for g in n:
            S_new[g][i] = egl[g][i * C:i * C + 1, :] * S[g][i] + _tn(blk(kend[g], i), blk(Ub[g], i))
    return O, S_new


def _dn_prompt_kernel(qkv_ref, z_ref, us_ref, cst_ref, st_ref, cw_ref, prm_ref, nrm_ref, o_ref, so_ref, halo_ref,
                      *, tb, C, nc):
    t = pl.program_id(1)
    hg = R_BD // C
    CR = nc * C

    @pl.when(t == 0)
    def _():
        so_ref[...] = st_ref[...]
        halo_ref[0:5, :] = jnp.zeros((5, DN_QKV), F32)
        halo_ref[5:8, :] = cst_ref[...]

    neg, offdiag, eye = _bd_masks(C)
    ii = _iota2((CR, CR), 0)
    jj = _iota2((CR, CR), 1)
    sh = C.bit_length() - 1
    tril = (((ii >> sh) == (jj >> sh)) & (jj <= ii)).astype(F32)
    lane = _iota2((1, LANES), 1)
    gmask = (lane >= DN_HEADS) & (lane < 2 * DN_HEADS)
    neg_a = -jnp.exp(prm_ref[0:1, :])
    bias = prm_ref[1:2, :]
    cw = cw_ref[...]
    nrm = nrm_ref[...]
    head_sets = [range(gi * hg, (gi + 1) * hg) for gi in range(DN_HEADS // hg)]
    groups = [(ci, hs) for ci in range(nc) for hs in head_sets]
    rws = lambda ci: slice(ci * C, (ci + 1) * C)

    def step(cp, carry):
        r0 = pl.multiple_of(cp * CR, CR)
        x = qkv_ref[pl.ds(r0, CR), :]
        prev = qkv_ref[pl.ds(pl.multiple_of(jnp.maximum(r0 - SUBLANES, 0), SUBLANES), SUBLANES), :]
        halo = jnp.where(cp == 0, halo_ref[...], prev)
        act = _silu(_conv_rows(halo, x, cw))
        us = us_ref[pl.ds(r0, CR), :]
        g = jnp.where(gmask, neg_a * jax.nn.softplus(us + bias), 0.0)
        beta_all = jax.nn.sigmoid(us)
        G = _mm_hi(tril, g)
        GT = G.T
        z = z_ref[pl.ds(r0, CR), :]
        cat0 = lambda f: [jnp.concatenate([f(ci, h) for h in hs], axis=0) for ci, hs in groups]
        Q = [_l2n(a) * (DN_DK ** -0.5) for a in cat0(lambda ci, h: act[rws(ci), h * DN_DK:(h + 1) * DN_DK])]
        K = [_l2n(a) for a in cat0(lambda ci, h: act[rws(ci), DN_W + h * DN_DK:DN_W + (h + 1) * DN_DK])]
        V = cat0(lambda ci, h: act[rws(ci), 2 * DN_W + h * DN_DV:2 * DN_W + (h + 1) * DN_DV])
        beta = cat0(lambda ci, h: beta_all[rws(ci), h:h + 1])
        gc = cat0(lambda ci, h: G[rws(ci), DN_HEADS + h:DN_HEADS + h + 1])
        gl = cat0(lambda ci, h: jnp.broadcast_to(
            G[(ci + 1) * C - 1:(ci + 1) * C, DN_HEADS + h:DN_HEADS + h + 1], (C, 1)))
        gr = [jnp.concatenate([GT[DN_HEADS + h:DN_HEADS + h + 1, rws(ci)] for h in hs], axis=1)
              for ci, hs in groups]
        u0, wk, qkd, qg = _dn_groups(Q, K, V, beta, gc, gr, neg, offdiag, eye, C)
        Z = cat0(lambda ci, h: z[rws(ci), h * DN_DV:(h + 1) * DN_DV])
        S = [[so_ref[h] for h in hs] for hs in head_sets]
        ng = len(head_sets)
        for ci in range(nc):
            pick = lambda xs: xs[ci * ng:(ci + 1) * ng]
            O, S = _dn_apply(S, pick(u0), pick(wk), pick(qkd), pick(qg), pick(K), pick(gl), pick(gc), C)
            for gi, hs in enumerate(head_sets):
                o = O[gi] * lax.rsqrt(jnp.mean(O[gi] * O[gi], axis=-1, keepdims=True) + EPS) * nrm
                out = (o * _silu(Z[ci * ng + gi])).astype(o_ref.dtype)
                for i, h in enumerate(hs):
                    o_ref[pl.ds(r0 + ci * C, C), h * DN_DV:(h + 1) * DN_DV] = out[i * C:(i + 1) * C]
        for gi, hs in enumerate(head_sets):
            for i, h in enumerate(hs):
                so_ref[h] = S[gi][i]
        return carry

    lax.fori_loop(0, tb // CR, step, 0)
    halo_ref[...] = qkv_ref[tb - SUBLANES:tb, :]


def _dn_sample_kernel(qkv_ref, z_ref, us_ref, cst_ref, st_ref, cw_ref, prm_ref, nrm_ref, o_ref, so_ref, halo_ref,
                      *, bb, lvalid, carry):
    st_ref, so_ref = _own_layer(st_ref, so_ref, carry)
    C = SUBLANES
    nb = R_BD // (DN_HEADS * C)
    rows_all = bb * C

    halo_ref[:, 0:5, :] = jnp.zeros((bb, 5, DN_QKV), F32)
    halo_ref[:, 5:8, :] = cst_ref[...]
    cw = cw_ref[...]
    nrm = nrm_ref[...]
    acts = [_silu(_conv_rows(halo_ref[b], qkv_ref[b], cw)) for b in range(bb)]

    neg, offdiag, eye = _bd_masks(C)
    lane = _iota2((1, LANES), 1)
    gmask = (lane >= DN_HEADS) & (lane < 2 * DN_HEADS)
    neg_a = -jnp.exp(prm_ref[0:1, :])
    bias = prm_ref[1:2, :]
    us = us_ref[...].reshape(rows_all, LANES)
    valid = (_iota2((rows_all, 1), 0) & (C - 1)) < lvalid
    g = jnp.where(gmask & valid, neg_a * jax.nn.softplus(us + bias), 0.0)
    beta_all = jnp.where(valid, jax.nn.sigmoid(us), 0.0)
    ii = _iota2((rows_all, rows_all), 0)
    jj = _iota2((rows_all, rows_all), 1)
    same = (ii >> 3) == (jj >> 3)
    G_all = _mm_hi((same & (jj <= ii)).astype(F32), g)
    GL_all = _mm_hi(same.astype(F32), g)
    GT_all = G_all.T

    groups = [[(h, gi * nb + bl) for h in range(DN_HEADS) for bl in range(nb)]
              for gi in range(bb // nb)]
    rows = [slice(gi * nb * C, (gi + 1) * nb * C) for gi in range(bb // nb)]
    cat0 = lambda f: [jnp.concatenate([f(h, b) for h, b in us_], axis=0) for us_ in groups]
    cath = lambda f: [jnp.concatenate([f(h, rs) for h in range(DN_HEADS)], axis=0) for rs in rows]
    Q = [_l2n(x) * (DN_DK ** -0.5) for x in cat0(lambda h, b: acts[b][:, h * DN_DK:(h + 1) * DN_DK])]
    K = [_l2n(x) for x in cat0(lambda h, b: acts[b][:, DN_W + h * DN_DK:DN_W + (h + 1) * DN_DK])]
    V = cat0(lambda h, b: acts[b][:, 2 * DN_W + h * DN_DV:2 * DN_W + (h + 1) * DN_DV])
    beta = cath(lambda h, rs: beta_all[rs, h:h + 1])
    gc = cath(lambda h, rs: G_all[rs, DN_HEADS + h:DN_HEADS + h + 1])
    gl = cath(lambda h, rs: GL_all[rs, DN_HEADS + h:DN_HEADS + h + 1])
    gr = [jnp.concatenate([GT_all[DN_HEADS + h:DN_HEADS + h + 1, rs] for h in range(DN_HEADS)], axis=1)
          for rs in rows]
    u0, wk, qkd, qg = _dn_groups(Q, K, V, beta, gc, gr, neg, offdiag, eye, C)
    S = [[st_ref[b, h] for h, b in us_] for us_ in groups]
    O, S_new = _dn_apply(S, u0, wk, qkd, qg, K, gl, gc, C)
    Z = cat0(lambda h, b: z_ref[b][:, h * DN_DV:(h + 1) * DN_DV])
    for g, us_ in enumerate(groups):
        o = O[g] * lax.rsqrt(jnp.mean(O[g] * O[g], axis=-1, keepdims=True) + EPS) * nrm
        out = (o * _silu(Z[g])).astype(o_ref.dtype)
        for i, (h, b) in enumerate(us_):
            so_ref[b, h] = S_new[g][i]
            o_ref[b, :, h * DN_DV:(h + 1) * DN_DV] = out[i * C:(i + 1) * C]


def _dn_mixer(um, us, cst_all, st_all, j, cw, prm, nrm, *, bb, tb, C, lvalid):
    B, L, _ = um.shape
    const = lambda *_: (0, 0)
    out_shape = [jax.ShapeDtypeStruct((B, L, DN_W), BF16), jax.ShapeDtypeStruct(st_all.shape, F32)]
    carry = None
    if lvalid == L:
        assert bb == 1
        kern = functools.partial(_dn_prompt_kernel, tb=tb, C=C, nc=4)
        grid = (B, L // tb)
        in_specs = [pl.BlockSpec((None, tb, DN_QKV), lambda b, t: (b, t, AB_QKV // DN_QKV)),
                    pl.BlockSpec((None, tb, DN_W), lambda b, t: (b, t, AB_ZDN // DN_W)),
                    pl.BlockSpec((None, tb, LANES), lambda b, t: (b, t, 0)),
                    pl.BlockSpec((None, None, CONV_W - 1, DN_QKV), lambda b, t: (j, b, 0, 0)),
                    pl.BlockSpec((None, None, DN_HEADS, DN_DK, DN_DV), lambda b, t: (j, b, 0, 0, 0)),
                    pl.BlockSpec((CONV_W, DN_QKV), const),
                    pl.BlockSpec((2, LANES), const),
                    pl.BlockSpec((1, DN_DV), const)]
        out_specs = [pl.BlockSpec((None, tb, DN_W), lambda b, t: (b, t, 0)),
                     pl.BlockSpec((None, None, DN_HEADS, DN_DK, DN_DV), lambda b, t: (j, b, 0, 0, 0))]
        scratch = [pltpu.VMEM((SUBLANES, DN_QKV), F32)]
        sem = ("parallel", "arbitrary")
    else:
        assert L == SUBLANES and C == SUBLANES
        carry = (st_all.shape[0], j) if j == 0 else None
        kern = functools.partial(_dn_sample_kernel, bb=bb, lvalid=lvalid, carry=carry)
        grid = (B // bb,)
        in_specs = [pl.BlockSpec((bb, L, DN_QKV), lambda b: (b, 0, AB_QKV // DN_QKV)),
                    pl.BlockSpec((bb, L, DN_W), lambda b: (b, 0, AB_ZDN // DN_W)),
                    pl.BlockSpec((bb, L, LANES), lambda b: (b, 0, 0)),
                    pl.BlockSpec((None, bb, CONV_W - 1, DN_QKV), lambda b: (j, b, 0, 0)),
                    _state_specs(st_all, j, bb, carry),
                    pl.BlockSpec((CONV_W, DN_QKV), const),
                    pl.BlockSpec((2, LANES), const),
                    pl.BlockSpec((1, DN_DV), const)]
        out_specs = [pl.BlockSpec((bb, L, DN_W), lambda b: (b, 0, 0)), _state_specs(st_all, j, bb, carry)]
        scratch = [pltpu.VMEM((bb, SUBLANES, DN_QKV), F32)]
        sem = ("parallel",)
    return pl.pallas_call(
        kern, grid=grid, in_specs=in_specs, out_specs=out_specs, out_shape=out_shape,
        scratch_shapes=scratch, input_output_aliases={} if carry else {4: 1},
        compiler_params=_params(dimension_semantics=sem), name="dn_mixer",
    )(um, um, us, cst_all, st_all, cw, prm, nrm)


def _gla_sample_kernel(q_ref, k_ref, v_ref, z_ref, us_ref, st_ref, wa_ref, ba_ref, nrm_ref, o_ref, so_ref,
                       *, bb, lvalid, carry):
    st_ref, so_ref = _own_layer(st_ref, so_ref, carry)
    C = SUBLANES
    R = bb * C
    ii = _iota2((C, C), 0)
    jj = _iota2((C, C), 1)
    nrm = nrm_ref[...]
    valid = (_iota2((R, 1), 0) & (C - 1)) < lvalid
    us = us_ref[...].reshape(R, LANES)
    glog = jax.nn.log_sigmoid(_mm(us.astype(BF16), wa_ref[...]) + ba_ref[...]) * (1.0 / GLA_GATE_NORM)
    glog = jnp.where(valid, glog, 0.0)
    ri = _iota2((R, R), 0)
    ci = _iota2((R, R), 1)
    G = _mm_hi((((ri >> 3) == (ci >> 3)) & (ci <= ri)).astype(F32), glog)
    q = q_ref[...].reshape(R, GLA_KW) * (GLA_DK ** -0.5)
    k = jnp.where(valid, k_ref[...].reshape(R, GLA_KW), 0.0)
    v = jnp.where(valid, v_ref[...].reshape(R, GLA_W), 0.0)
    z = z_ref[...].reshape(R, GLA_W)
    units = [(b, h) for b in range(bb) for h in range(GLA_HEADS)]
    U = range(len(units))
    tile = lambda x, b, h, w: x[b * C:(b + 1) * C, h * w:(h + 1) * w]
    Gu = [tile(G, b, h, GLA_DK) for b, h in units]
    qu = [tile(q, b, h, GLA_DK) for b, h in units]
    ku = [tile(k, b, h, GLA_DK) for b, h in units]
    vu = [tile(v, b, h, GLA_DV).astype(BF16) for b, h in units]
    AT = [jnp.zeros((C, C), F32) for _ in U]
    for i in range(C):
        for u in U:
            e = jnp.exp(jnp.minimum(Gu[u][i:i + 1, :] - Gu[u], 0.0))
            col = jnp.sum(ku[u] * e * qu[u][i:i + 1, :], axis=1, keepdims=True)
            AT[u] = AT[u] + jnp.where(jj == i, col, 0.0)
    AT = [jnp.where(ii <= jj, AT[u], 0.0).astype(BF16) for u in U]
    S = [st_ref[b, h] for b, h in units]
    o = [_mm((qu[u] * jnp.exp(Gu[u])).astype(BF16), S[u].astype(BF16)) + _tn(AT[u], vu[u]) for u in U]
    kend = [(ku[u] * jnp.exp(Gu[u][C - 1:C, :] - Gu[u])).astype(BF16) for u in U]
    gend = [jnp.exp(Gu[u].T[:, C - 1:C]) for u in U]
    for u, (b, h) in enumerate(units):
        so_ref[b, h] = gend[u] * S[u] + _tn(kend[u], vu[u])
    o = [o[u] * lax.rsqrt(jnp.mean(o[u] * o[u], axis=-1, keepdims=True) + EPS) * nrm for u in U]
    o = [o[u] * _silu(tile(z, b, h, GLA_DV)) for u, (b, h) in enumerate(units)]
    rows = [jnp.concatenate(o[b * GLA_HEADS:(b + 1) * GLA_HEADS], axis=1) for b in range(bb)]
    o_ref[...] = jnp.concatenate(rows, axis=0).astype(o_ref.dtype).reshape(bb, C, GLA_W)


def _gla_prompt_kernel(q_ref, k_ref, v_ref, z_ref, us_ref, st_ref, wa_ref, ba_ref, nrm_ref, o_ref, so_ref, g_ref,
                       *, tb, C):
    t = pl.program_id(1)
    H = range(GLA_HEADS)
    nch = tb // C

    @pl.when(t == 0)
    def _():
        so_ref[...] = st_ref[...]

    glog = jax.nn.log_sigmoid(_mm(us_ref[...].astype(BF16), wa_ref[...]) + ba_ref[...]) * (1.0 / GLA_GATE_NORM)
    step_in_chunk = _iota2((tb, 1), 0) & (C - 1)
    d = 1
    while d < C:
        glog = glog + jnp.where(step_in_chunk >= d, pltpu.roll(glog, d, axis=0), 0.0)
        d *= 2
    g_ref[...] = glog * LOG2E

    ii = _iota2((C, C), 0)
    jj = _iota2((C, C), 1)
    nrm = nrm_ref[...]
    hs = lambda x, h, w: x[:, h * w:(h + 1) * w]

    def scores(c):
        rows = pl.ds(pl.multiple_of(c * C, C), C)
        G = g_ref[rows, :]
        q = q_ref[rows, :] * (GLA_DK ** -0.5)
        k = k_ref[rows, :]
        nt = C // SUBLANES
        Gt = [[hs(G, h, GLA_DK)[t * SUBLANES:(t + 1) * SUBLANES] for t in range(nt)] for h in H]
        kt = [[hs(k, h, GLA_DK)[t * SUBLANES:(t + 1) * SUBLANES] for t in range(nt)] for h in H]
        acc = [[jnp.zeros((SUBLANES, C), F32) for _ in range(nt)] for _ in H]
        for i in range(C):
            for t in range(i // SUBLANES + 1):
                for h in H:
                    e = jnp.exp2(jnp.minimum(hs(G, h, GLA_DK)[i:i + 1, :] - Gt[h][t], 0.0))
                    col = jnp.sum(kt[h][t] * e * hs(q, h, GLA_DK)[i:i + 1, :], axis=1, keepdims=True)
                    acc[h][t] = acc[h][t] + jnp.where(jj[:SUBLANES] == i, col, 0.0)
        AT = [jnp.where(ii <= jj, jnp.concatenate(acc[h], axis=0), 0.0).astype(BF16) for h in H]
        gend = [jnp.exp2(hs(G, h, GLA_DK).T[:, C - 1:C]) for h in H]
        return AT, gend

    def finish(o, c):
        rows = pl.ds(pl.multiple_of(c * C, C), C)
        z = z_ref[rows, :]
        o = [o[h] * lax.rsqrt(jnp.mean(o[h] * o[h], axis=-1, keepdims=True) + EPS) * nrm for h in H]
        o = [o[h] * _silu(hs(z, h, GLA_DV)) for h in H]
        o_ref[rows, :] = jnp.concatenate(o, axis=1).astype(o_ref.dtype)

    def body(c, carry):
        AT, gend, o_prev = carry
        nxt = scores(jnp.minimum(c + 1, nch - 1))
        rows = pl.ds(pl.multiple_of(c * C, C), C)
        G = g_ref[rows, :]
        q = q_ref[rows, :] * (GLA_DK ** -0.5)
        k = k_ref[rows, :]
        v = v_ref[rows, :]
        S = [so_ref[h] for h in H]
        vh = [hs(v, h, GLA_DV).astype(BF16) for h in H]
        o = [_mm((hs(q, h, GLA_DK) * jnp.exp2(hs(G, h, GLA_DK))).astype(BF16), S[h].astype(BF16)) + _tn(AT[h], vh[h])
             for h in H]
        kend = [(hs(k, h, GLA_DK) * jnp.exp2(hs(G, h, GLA_DK)[C - 1:C, :] - hs(G, h, GLA_DK))).astype(BF16) for h in H]
        for h in H:
            so_ref[h] = gend[h] * S[h] + _tn(kend[h], vh[h])
        finish(o_prev, jnp.maximum(c - 1, 0))
        return nxt + (o,)

    AT0, gend0 = scores(0)
    _, _, o_last = lax.fori_loop(0, nch, body, (AT0, gend0, [jnp.zeros((C, GLA_DV), F32) for _ in H]))
    finish(o_last, nch - 1)


def _gla_mixer(um, us, st_all, j, wa, ba, nrm, *, bb, tb, C, lvalid):
    B, L, _ = um.shape
    const = lambda *_: (0, 0)
    carry = None
    if lvalid == L:
        assert bb == 1
        kern = functools.partial(_gla_prompt_kernel, tb=tb, C=C)
        sq = None
        scratch = [pltpu.VMEM((tb, GLA_KW), F32)]
        st_spec = pl.BlockSpec((None, None, GLA_HEADS, GLA_DK, GLA_DV), lambda b, t: (j, b, 0, 0, 0))
    else:
        assert L == SUBLANES and tb == L
        carry = (st_all.shape[0], j) if j == 0 else None
        kern = functools.partial(_gla_sample_kernel, bb=bb, lvalid=lvalid, carry=carry)
        sq = bb
        scratch = []
        st_spec = _state_specs(st_all, j, bb, carry)
    return pl.pallas_call(
        kern,
        grid=(B // bb, L // tb),
        in_specs=[pl.BlockSpec((sq, tb, GLA_KW), lambda b, t: (b, t, AB_GQ // GLA_KW)),
                  pl.BlockSpec((sq, tb, GLA_KW), lambda b, t: (b, t, AB_GK // GLA_KW)),
                  pl.BlockSpec((sq, tb, GLA_W), lambda b, t: (b, t, AB_GV // GLA_W)),
                  pl.BlockSpec((sq, tb, GLA_W), lambda b, t: (b, t, AB_ZGLA // GLA_W)),
                  pl.BlockSpec((sq, tb, LANES), lambda b, t: (b, t, 0)),
                  st_spec,
                  pl.BlockSpec((LANES, GLA_KW), const),
                  pl.BlockSpec((1, GLA_KW), const),
                  pl.BlockSpec((1, GLA_DV), const)],
        out_specs=[pl.BlockSpec((sq, tb, GLA_W), lambda b, t: (b, t, 0)), st_spec],
        out_shape=[jax.ShapeDtypeStruct((B, L, GLA_W), BF16), jax.ShapeDtypeStruct(st_all.shape, F32)],
        scratch_shapes=scratch,
        input_output_aliases={} if carry else {5: 1},
        compiler_params=_params(dimension_semantics=("parallel", "arbitrary")),
        name="gla_mixer",
    )(um, um, um, um, us, st_all, wa, ba, nrm)


SSD_GW = SSD_W // SSD_GROUPS
SSD_REP = SSD_HEADS // SSD_GROUPS
SSD_ROWS = 64


def _lane_expand(x, h0, width):
    return jnp.concatenate([jnp.broadcast_to(x[:, h0 + j:h0 + j + 1], (x.shape[0], width)) for j in range(SSD_REP)],
                           axis=1)


def _ssd_masks(C):
    ii = _iota2((SSD_ROWS, SSD_GW), 0)
    jj = _iota2((SSD_ROWS, SSD_GW), 1) & (SSD_ROWS - 1)
    sh = C.bit_length() - 1
    negc = jnp.where(((ii >> sh) == (jj >> sh)) & (jj <= ii), 0.0, -1e30)
    ri = _iota2((SSD_GW, SSD_GW), 0)
    ci = _iota2((SSD_GW, SSD_GW), 1)
    bd = ((ri >> 6) == (ci >> 6)).astype(F32)
    return negc, bd


def _ssd_intra(act, G, GT, dt, gi, negc, bd):
    h0 = gi * SSD_REP
    xs = act[:, gi * SSD_GW:(gi + 1) * SSD_GW]
    bm = act[:, SSD_W + gi * SSD_STATE:SSD_W + (gi + 1) * SSD_STATE].astype(BF16)
    cm = act[:, SSD_W + (SSD_GROUPS + gi) * SSD_STATE:SSD_W + (SSD_GROUPS + gi + 1) * SSD_STATE].astype(BF16)
    gcc = _lane_expand(G, h0, SSD_HEADDIM)
    grc = jnp.concatenate([GT[h0 + j:h0 + j + 1, :] for j in range(SSD_REP)], axis=1)
    vcat = xs * _lane_expand(dt, h0, SSD_HEADDIM)
    cbm = _nt(cm, bm)
    acat = (jnp.concatenate([cbm] * SSD_REP, axis=1) * jnp.exp((gcc - grc) + negc)).astype(BF16)
    vbd = (jnp.concatenate([vcat] * SSD_REP, axis=0) * bd).astype(BF16)
    return _mm(acat, vbd), xs, vcat, gcc, bm, cm


def _ssd_finish(o, xs, z, dsk, nrm):
    y = (o + dsk * xs) * _silu(z)
    return y * lax.rsqrt(jnp.mean(y * y, axis=-1, keepdims=True) + EPS) * nrm


def _ssd_prompt_kernel(z_ref, xbc_ref, us_ref, cst_ref, st_ref, cw_ref, cb_ref, prm_ref, nrm_ref, o_ref, so_ref,
                       halo_ref, scat_ref, *, tb):
    t = pl.program_id(1)
    C = SSD_ROWS
    cat_heads = lambda f, gi: jnp.concatenate([f(gi * SSD_REP + j) for j in range(SSD_REP)], axis=1)

    @pl.when(t == 0)
    def _():
        for gi in range(SSD_GROUPS):
            scat_ref[gi] = cat_heads(lambda h: st_ref[h], gi)
        halo_ref[0:5, :] = jnp.zeros((5, SSD_XBC), F32)
        halo_ref[5:8, :] = cst_ref[...]

    negc, bd = _ssd_masks(C)
    tril = (_iota2((C, C), 1) <= _iota2((C, C), 0)).astype(F32)
    lane = _iota2((1, LANES), 1)
    hmask = lane < SSD_HEADS
    neg_a = -jnp.exp(prm_ref[0:1, :])
    bias = prm_ref[1:2, :]
    dsk = [_lane_expand(prm_ref[2:3, :], gi * SSD_REP, SSD_HEADDIM) for gi in range(SSD_GROUPS)]
    cw = cw_ref[...]
    cb = cb_ref[...]
    nrm = nrm_ref[...]
    groups = range(SSD_GROUPS)

    def chunk(c, carry):
        r0 = pl.multiple_of(c * C, C)
        x = xbc_ref[pl.ds(r0, C), :]
        prev = xbc_ref[pl.ds(pl.multiple_of(jnp.maximum(r0 - SUBLANES, 0), SUBLANES), SUBLANES), :]
        halo = jnp.where(c == 0, halo_ref[...], prev)
        act = _silu(_conv_rows(halo, x, cw, cb))
        us = us_ref[pl.ds(r0, C), :]
        dt = jnp.where(hmask, jax.nn.softplus(us + bias), 0.0)
        G = _mm_hi(tril, neg_a * dt)
        GT = G.T
        z = z_ref[pl.ds(r0, C), :]
        parts = [_ssd_intra(act, G, GT, dt, gi, negc, bd) for gi in groups]
        S = [scat_ref[gi] for gi in groups]
        inter = [_mm(parts[gi][5], S[gi].astype(BF16)) for gi in groups]
        for gi in groups:
            o_in, xs, vcat, gcc, bm, cm = parts[gi]
            glc = gcc[C - 1:C, :]
            w = (vcat * jnp.exp(glc - gcc)).astype(BF16)
            scat_ref[gi] = jnp.exp(glc) * S[gi] + _tn(bm, w)
            y = _ssd_finish(jnp.exp(gcc) * inter[gi] + o_in, xs, z[:, gi * SSD_GW:(gi + 1) * SSD_GW], dsk[gi],
                            nrm[:, gi * SSD_GW:(gi + 1) * SSD_GW])
            o_ref[pl.ds(r0, C), gi * SSD_GW:(gi + 1) * SSD_GW] = y.astype(o_ref.dtype)
        return carry

    lax.fori_loop(0, tb // C, chunk, 0)
    halo_ref[...] = xbc_ref[tb - SUBLANES:tb, :]

    @pl.when(t == pl.num_programs(1) - 1)
    def _():
        for gi in groups:
            s = scat_ref[gi]
            for j in range(SSD_REP):
                so_ref[gi * SSD_REP + j] = s[:, j * SSD_HEADDIM:(j + 1) * SSD_HEADDIM]


def _ssd_sample_kernel(z_ref, xbc_ref, us_ref, cst_ref, st_ref, cw_ref, cb_ref, prm_ref, nrm_ref, o_ref, so_ref,
                       halo_ref, *, bb, lvalid, carry):
    st_ref, so_ref = _own_layer(st_ref, so_ref, carry)
    C = SUBLANES
    assert bb * C == SSD_ROWS
    halo_ref[:, 0:5, :] = jnp.zeros((bb, 5, SSD_XBC), F32)
    halo_ref[:, 5:8, :] = cst_ref[...]
    cw = cw_ref[...]
    cb = cb_ref[...]
    nrm = nrm_ref[...]
    act = jnp.concatenate([_silu(_conv_rows(halo_ref[b], xbc_ref[b], cw, cb)) for b in range(bb)], axis=0)

    negc, bd = _ssd_masks(C)
    lane = _iota2((1, LANES), 1)
    neg_a = -jnp.exp(prm_ref[0:1, :])
    bias = prm_ref[1:2, :]
    us = us_ref[...].reshape(SSD_ROWS, LANES)
    valid = (_iota2((SSD_ROWS, 1), 0) & (C - 1)) < lvalid
    dt = jnp.where((lane < SSD_HEADS) & valid, jax.nn.softplus(us + bias), 0.0)
    ii = _iota2((SSD_ROWS, SSD_ROWS), 0)
    jj = _iota2((SSD_ROWS, SSD_ROWS), 1)
    same = (ii >> 3) == (jj >> 3)
    g = neg_a * dt
    G = _mm_hi((same & (jj <= ii)).astype(F32), g)
    GL = _mm_hi(same.astype(F32), g)
    GT = G.T
    z = z_ref[...].reshape(SSD_ROWS, SSD_W)
    groups = range(SSD_GROUPS)
    parts = [_ssd_intra(act, G, GT, dt, gi, negc, bd) for gi in groups]
    units = [(gi, b) for gi in groups for b in range(bb)]
    rows = lambda b: slice(b * C, (b + 1) * C)
    glc = [_lane_expand(GL, gi * SSD_REP, SSD_HEADDIM) for gi in groups]
    w = [(parts[gi][2] * jnp.exp(glc[gi] - parts[gi][3])).astype(BF16) for gi in groups]
    egl = [jnp.exp(glc[gi]) for gi in groups]
    S = [jnp.concatenate([st_ref[b, gi * SSD_REP + j] for j in range(SSD_REP)], axis=1) for gi, b in units]
    inter = [_mm(parts[gi][5][rows(b)], S[u].astype(BF16)) for u, (gi, b) in enumerate(units)]
    s_new = [egl[gi][b * C:b * C + 1, :] * S[u] + _tn(parts[gi][4][rows(b)], w[gi][rows(b)])
             for u, (gi, b) in enumerate(units)]
    for u, (gi, b) in enumerate(units):
        for j in range(SSD_REP):
            so_ref[b, gi * SSD_REP + j] = s_new[u][:, j * SSD_HEADDIM:(j + 1) * SSD_HEADDIM]
    outs = []
    for gi in groups:
        o_in, xs, vcat, gcc, bm, cm = parts[gi]
        o = jnp.exp(gcc) * jnp.concatenate(inter[gi * bb:(gi + 1) * bb], axis=0) + o_in
        dsk = _lane_expand(prm_ref[2:3, :], gi * SSD_REP, SSD_HEADDIM)
        outs.append(_ssd_finish(o, xs, z[:, gi * SSD_GW:(gi + 1) * SSD_GW], dsk, nrm[:, gi * SSD_GW:(gi + 1) * SSD_GW]))
    o_ref[...] = jnp.concatenate(outs, axis=1).astype(o_ref.dtype).reshape(bb, C, SSD_W)


def _ssd_mixer(um, us, cst_all, st_all, j, cw, cb, prm, nrm, *, bb, tb, lvalid):
    B, L, _ = um.shape
    const = lambda *_: (0, 0)
    out_shape = [jax.ShapeDtypeStruct((B, L, SSD_W), BF16), jax.ShapeDtypeStruct(st_all.shape, F32)]
    st_blk = (SSD_HEADS, SSD_STATE, SSD_HEADDIM)
    w_specs = [pl.BlockSpec((CONV_W, SSD_XBC), const), pl.BlockSpec((1, SSD_XBC), const),
               pl.BlockSpec((3, LANES), const), pl.BlockSpec((1, SSD_W), const)]
    carry = None
    if lvalid == L:
        assert bb == 1 and tb % SSD_ROWS == 0
        kern = functools.partial(_ssd_prompt_kernel, tb=tb)
        grid = (B, L // tb)
        in_specs = [pl.BlockSpec((None, tb, SSD_W), lambda b, t: (b, t, CD_Z // SSD_W)),
                    pl.BlockSpec((None, tb, SSD_XBC), lambda b, t: (b, t, CD_XBC // SSD_XBC)),
                    pl.BlockSpec((None, tb, LANES), lambda b, t: (b, t, 0)),
                    pl.BlockSpec((None, None, CONV_W - 1, SSD_XBC), lambda b, t: (j, b, 0, 0)),
                    pl.BlockSpec((None, None) + st_blk, lambda b, t: (j, b, 0, 0, 0))] + w_specs
        out_specs = [pl.BlockSpec((None, tb, SSD_W), lambda b, t: (b, t, 0)),
                     pl.BlockSpec((None, None) + st_blk, lambda b, t: (j, b, 0, 0, 0))]
        scratch = [pltpu.VMEM((SUBLANES, SSD_XBC), F32), pltpu.VMEM((SSD_GROUPS, SSD_STATE, SSD_GW), F32)]
        sem = ("parallel", "arbitrary")
    else:
        assert L == SUBLANES
        carry = (st_all.shape[0], j) if j == 0 else None
        kern = functools.partial(_ssd_sample_kernel, bb=bb, lvalid=lvalid, carry=carry)
        grid = (B // bb,)
        in_specs = [pl.BlockSpec((bb, L, SSD_W), lambda b: (b, 0, CD_Z // SSD_W)),
                    pl.BlockSpec((bb, L, SSD_XBC), lambda b: (b, 0, CD_XBC // SSD_XBC)),
                    pl.BlockSpec((bb, L, LANES), lambda b: (b, 0, 0)),
                    pl.BlockSpec((None, bb, CONV_W - 1, SSD_XBC), lambda b: (j, b, 0, 0)),
                    _state_specs(st_all, j, bb, carry)] + w_specs
        out_specs = [pl.BlockSpec((bb, L, SSD_W), lambda b: (b, 0, 0)), _state_specs(st_all, j, bb, carry)]
        scratch = [pltpu.VMEM((bb, SUBLANES, SSD_XBC), F32)]
        sem = ("parallel",)
    return pl.pallas_call(
        kern, grid=grid, in_specs=in_specs, out_specs=out_specs, out_shape=out_shape,
        scratch_shapes=scratch, input_output_aliases={} if carry else {4: 1},
        compiler_params=_params(dimension_semantics=sem), name="ssd_mixer",
    )(um, um, us, cst_all, st_all, cw, cb, prm, nrm)


def _lru_kernel(z_ref, x_ref, cst_ref, st_ref, cw_ref, cb_ref, wbd_ref, prm_ref, o_ref, so_ref, halo_ref, carry_ref,
                *, bb, tb, lvalid, ltotal):
    t = pl.program_id(1)

    @pl.when(t == 0)
    def _():
        carry_ref[...] = st_ref[...]
        halo_ref[:, 0:5, :] = jnp.zeros((bb, 5, LRU_W), F32)
        halo_ref[:, 5:8, :] = cst_ref[...]

    cw = cw_ref[...]
    cb = cb_ref[...]
    b_a = prm_ref[0:1, :]
    b_x = prm_ref[1:2, :]
    sp = jax.nn.softplus(-prm_ref[2:3, :])
    row = _iota2((tb, 1), 0)
    last_t, last_r = (lvalid - 1) // tb, (lvalid - 1) % tb

    def per_batch(bi, carry):
        x = x_ref[bi]
        xc = _conv_rows(halo_ref[bi], x, cw, cb)
        halo_ref[bi] = x[tb - SUBLANES:tb, :]
        xb = xc.astype(BF16)
        rs, is_ = [], []
        for cg in range(LRU_W // LANES):
            ri = _mm(xb[:, cg * LANES:(cg + 1) * LANES], wbd_ref[cg])
            rs.append(ri[:, :LANES])
            is_.append(ri[:, LANES:])
        r = jax.nn.sigmoid(jnp.concatenate(rs, axis=1) + b_a)
        i = jax.nn.sigmoid(jnp.concatenate(is_, axis=1) + b_x)
        log_a = (-LRU_C) * r * sp
        a = jnp.exp(log_a)
        om = 1.0 - jnp.exp(2.0 * log_a)
        b = jnp.where(om > 0.0, om * lax.rsqrt(om), 0.0) * (i * xc)
        d = 1
        while d < tb:
            m = row >= d
            a_s = pltpu.roll(a, d, axis=0)
            b_s = pltpu.roll(b, d, axis=0)
            b = jnp.where(m, a * b_s + b, b)
            a = jnp.where(m, a * a_s, a)
            d *= 2
        hs = a * carry_ref[bi] + b
        carry_ref[bi] = hs[tb - 1:tb, :]

        @pl.when(t == last_t)
        def _():
            so_ref[bi] = hs[last_r:last_r + 1, :]

        o_ref[bi] = (hs * _silu(z_ref[bi])).astype(o_ref.dtype)
        return carry

    lax.fori_loop(0, bb, per_batch, 0)


def _lru_mixer(um, cst_all, st_all, j, cw, cb, wbd, prm, *, bb, tb, lvalid):
    B, L, _ = um.shape
    kern = functools.partial(_lru_kernel, bb=bb, tb=tb, lvalid=lvalid, ltotal=L)
    const = lambda b, t: (0, 0)
    return pl.pallas_call(
        kern,
        grid=(B // bb, L // tb),
        in_specs=[pl.BlockSpec((bb, tb, LRU_W), lambda b, t: (b, t, CD_ZLRU // LRU_W)),
                  pl.BlockSpec((bb, tb, LRU_W), lambda b, t: (b, t, CD_XLRU // LRU_W)),
                  pl.BlockSpec((None, bb, CONV_W - 1, LRU_W), lambda b, t: (j, b, 0, 0)),
                  pl.BlockSpec((None, bb, 1, LRU_W), lambda b, t: (j, b, 0, 0)),
                  pl.BlockSpec((CONV_W, LRU_W), const),
                  pl.BlockSpec((1, LRU_W), const),
                  pl.BlockSpec((LRU_W // LANES, LANES, 2 * LANES), lambda b, t: (0, 0, 0)),
                  pl.BlockSpec((3, LRU_W), const)],
        out_specs=[pl.BlockSpec((bb, tb, LRU_W), lambda b, t: (b, t, 0)),
                   pl.BlockSpec((bb, 1, LRU_W), lambda b, t: (b, 0, 0))],
        out_shape=[jax.ShapeDtypeStruct((B, L, LRU_W), BF16),
                   jax.ShapeDtypeStruct((B, 1, LRU_W), F32)],
        scratch_shapes=[pltpu.VMEM((bb, SUBLANES, LRU_W), F32), pltpu.VMEM((bb, 1, LRU_W), F32)],
        compiler_params=_params(dimension_semantics=("parallel", "arbitrary")),
        name="lru_mixer",
    )(um, um, cst_all, st_all, cw, cb, wbd, prm)


def _lane_row(*pieces):
    v = jnp.concatenate([p.reshape(-1) for p in pieces])
    return jnp.pad(v, (0, LANES - v.shape[0])).reshape(1, LANES)


RP_TILE = 512
RP_TAIL = 32


def _repack_kernel(a_ref, b_ref, o_ref, *, plan):
    r = pl.program_id(1)
    lo = 0
    for n_tiles, _, shift in plan:
        @pl.when((r >= lo) & (r < lo + n_tiles))
        def _(shift=shift):
            if shift == 0:
                o_ref[...] = a_ref[...].astype(BF16)
            else:
                o_ref[0:RP_TILE - shift, :] = a_ref[shift:, :].astype(BF16)
                o_ref[RP_TILE - shift:, :] = b_ref[0:shift, :].astype(BF16)
        lo += n_tiles


def _repack_wt(wt, plan):
    n, _, D = wt.shape
    total = sum(p[0] for p in plan)

    def src_tile(r):
        lo, idx = 0, 0
        for n_tiles, start, _ in plan:
            idx = jnp.where((r >= lo) & (r < lo + n_tiles), start // RP_TILE + (r - lo), idx)
            lo += n_tiles
        return idx

    for _, start, shift in plan:
        assert start % RP_TILE == 0 and shift % 16 == 0 and shift <= RP_TAIL
    return pl.pallas_call(
        functools.partial(_repack_kernel, plan=plan),
        grid=(n, total),
        in_specs=[pl.BlockSpec((None, RP_TILE, D), lambda l, r: (l, src_tile(r), 0)),
                  pl.BlockSpec((None, RP_TAIL, D), lambda l, r: (l, (src_tile(r) + 1) * (RP_TILE // RP_TAIL), 0))],
        out_specs=pl.BlockSpec((None, RP_TILE, D), lambda l, r: (l, r, 0)),
        out_shape=jax.ShapeDtypeStruct((n, total * RP_TILE, D), BF16),
        compiler_params=_params(dimension_semantics=("parallel", "parallel")),
        name="repack_w",
    )(wt, wt)


def _rows_t(w, sizes, order, pad_to=None):
    wt = jnp.swapaxes(w, 1, 2)
    offs = [sum(sizes[:i]) for i in range(len(sizes))]
    out = jnp.concatenate([wt[:, offs[i]:offs[i] + sizes[i], :] for i in order], axis=1)
    if pad_to is not None:
        out = jnp.pad(out, ((0, 0), (0, pad_to - out.shape[1]), (0, 0)))
    return out


def _prep_weights(w):
    n_even = w['ab_w_in'].shape[0]
    n_odd = w['cd_w_in'].shape[0]
    D = w['ab_w_in'].shape[1]
    P = {'ab': [], 'cd': []}
    P['ab_w_main'] = _repack_wt(jnp.swapaxes(w['ab_w_in'], 1, 2), [(6, 0, 0), (6, 3072, 16), (2, 6144, 32)])
    P['ab_w_small'] = _rows_t(w['ab_w_in'], AB_SIZES, (1, 2, 7), LANES)
    P['ab_w_out'] = w['ab_w_out'].astype(BF16)
    for j in range(n_even):
        P['ab'].append(dict(
            dn_cw=w['dn_conv_w'][j],
            dn_prm=jnp.concatenate([_lane_row(jnp.zeros((DN_HEADS,), F32), w['dn_a_log'][j]),
                                    _lane_row(jnp.zeros((DN_HEADS,), F32), w['dn_dt_bias'][j])], axis=0),
            dn_norm=w['dn_norm'][j].reshape(1, DN_DV),
            gla_wa=jnp.pad(w['gla_wa2'][j], ((2 * DN_HEADS, LANES - 2 * DN_HEADS - GLA_RANK), (0, 0))).astype(BF16),
            gla_ba=w['gla_ba'][j].reshape(1, GLA_KW),
            gla_norm=w['gla_norm'][j].reshape(1, GLA_DV),
        ))
    P['cd_w_main'] = _repack_wt(jnp.swapaxes(w['cd_w_in'], 1, 2), [(2, 0, 0), (4, 2560, 16), (3, 1024, 0)])
    P['cd_w_small'] = _rows_t(w['cd_w_in'], CD_SIZES, (2,), LANES)
    P['cd_w_out'] = w['cd_w_out'].astype(BF16)
    eye2 = jnp.eye(2, dtype=F32)
    for j in range(n_odd):
        wa = w['lru_wa'][j].reshape(LRU_W // LANES, 2, LRU_BS, LRU_BS)
        wx = w['lru_wx'][j].reshape(LRU_W // LANES, 2, LRU_BS, LRU_BS)
        bd = lambda m: jnp.einsum('gbcd,be->gbced', m, eye2).reshape(LRU_W // LANES, LANES, LANES)
        P['cd'].append(dict(
            ssd_cw=w['ssd_conv_w'][j],
            ssd_cb=w['ssd_conv_b'][j].reshape(1, SSD_XBC),
            ssd_prm=jnp.concatenate([_lane_row(w['ssd_a_log'][j]), _lane_row(w['ssd_dt_bias'][j]),
                                     _lane_row(w['ssd_d'][j])], axis=0),
            ssd_norm=w['ssd_norm'][j].reshape(1, SSD_W),
            lru_cw=w['lru_conv_w'][j],
            lru_cb=w['lru_conv_b'][j].reshape(1, LRU_W),
            lru_wbd=jnp.concatenate([bd(wa), bd(wx)], axis=2).astype(BF16),
            lru_prm=jnp.stack([w['lru_ba'][j], w['lru_bx'][j], w['lru_lambda'][j]], axis=0),
        ))
    P['norm_g'] = w['norm_g'].reshape(-1, 1, D)
    P['ple_norm'] = w['ple_norm'].reshape(-1, 1, D)
    P['ple_w_gate'] = w['ple_w_gate'].astype(BF16)
    P['ple_w_proj'] = w['ple_w_proj'].astype(BF16)
    P['final_norm'] = w['final_norm'].reshape(1, D)
    return P


def _trunk(x, p, states, P, *, lvalid, bb, tb, c_dn, c_gla, c_ssd):
    B, L, D = x.shape
    depth = P['norm_g'].shape[0]
    st_dn, st_dnc, st_gla, st_ssd, st_ssdc, st_lru, st_lruc = states
    st_lru = st_lru.reshape(st_lru.shape[0], B, 1, LRU_W)
    dnc_new, ssdc_new, lru_new, lruc_new = [], [], [], []
    T = B * L
    h = x.reshape(T, D)
    p_all = p.reshape(depth, T, PLE_DIM)
    lo = lvalid - (CONV_W - 1)
    for li in range(depth):
        j = li // 2
        if li % 2 == 0:
            W = P['ab'][j]
            um, us = _inproj(h, P['norm_g'], li, P['ab_w_main'], P['ab_w_small'], j)
            um = um.reshape(B, L, AB_MAIN)
            us = us.reshape(B, L, LANES)
            oa, st_dn = _dn_mixer(um, us, st_dnc, st_dn, j, W['dn_cw'], W['dn_prm'], W['dn_norm'],
                                  bb=bb, tb=tb, C=c_dn, lvalid=lvalid)
            ob, st_gla = _gla_mixer(um, us, st_gla, j, W['gla_wa'], W['gla_ba'], W['gla_norm'],
                                    bb=bb, tb=tb, C=c_gla, lvalid=lvalid)
            dnc_new.append(um[:, lo:lvalid, AB_QKV:AB_QKV + DN_QKV])
        else:
            W = P['cd'][j]
            um, us = _inproj(h, P['norm_g'], li, P['cd_w_main'], P['cd_w_small'], j)
            um = um.reshape(B, L, CD_MAIN)
            us = us.reshape(B, L, LANES)
            oa, st_ssd = _ssd_mixer(um, us, st_ssdc, st_ssd, j, W['ssd_cw'], W['ssd_cb'], W['ssd_prm'],
                                    W['ssd_norm'], bb=bb, tb=tb, lvalid=lvalid)
            ob, lru_j = _lru_mixer(um, st_lruc, st_lru, j, W['lru_cw'], W['lru_cb'],
                                   W['lru_wbd'], W['lru_prm'], bb=bb, tb=tb, lvalid=lvalid)
            ssdc_new.append(um[:, lo:lvalid, CD_XBC:CD_XBC + SSD_XBC])
            lru_new.append(lru_j.reshape(B, LRU_W))
            lruc_new.append(um[:, lo:lvalid, CD_XLRU:CD_XLRU + LRU_W])
        h = _outproj(oa.reshape(T, -1), ob.reshape(T, -1), h, p_all, li, P['ab_w_out' if li % 2 == 0 else 'cd_w_out'], j,
                     P['ple_norm'], P['ple_w_gate'], P['ple_w_proj'], P['final_norm'], final=(li == depth - 1))
    new_states = [st_dn, jnp.stack(dnc_new), st_gla, st_ssd, jnp.stack(ssdc_new), jnp.stack(lru_new),
                  jnp.stack(lruc_new)]
    return h.reshape(B, L, D), new_states


def _pick_tb(L):
    tb = min(L, 512)
    assert L % tb == 0
    return tb


def kernel(x_prompt, x_sample, state_dn, state_dn_conv, state_gla, state_ssd, state_ssd_conv, state_lru, state_lru_conv, p_prompt, p_sample, norm_g, final_norm, ab_w_in, dn_conv_w, dn_a_log, dn_dt_bias, dn_norm, gla_wa2, gla_ba, gla_norm, ab_w_out, cd_w_in, ssd_conv_w, ssd_conv_b, ssd_a_log, ssd_dt_bias, ssd_d, ssd_norm, lru_conv_w, lru_conv_b, lru_wa, lru_ba, lru_wx, lru_bx, lru_lambda, cd_w_out, ple_w_proj, ple_norm, ple_w_gate):
    w = {
        'norm_g': norm_g, 'final_norm': final_norm,
        'ab_w_in': ab_w_in, 'dn_conv_w': dn_conv_w, 'dn_a_log': dn_a_log, 'dn_dt_bias': dn_dt_bias,
        'dn_norm': dn_norm, 'gla_wa2': gla_wa2, 'gla_ba': gla_ba, 'gla_norm': gla_norm, 'ab_w_out': ab_w_out,
        'cd_w_in': cd_w_in, 'ssd_conv_w': ssd_conv_w, 'ssd_conv_b': ssd_conv_b, 'ssd_a_log': ssd_a_log,
        'ssd_dt_bias': ssd_dt_bias, 'ssd_d': ssd_d, 'ssd_norm': ssd_norm,
        'lru_conv_w': lru_conv_w, 'lru_conv_b': lru_conv_b, 'lru_wa': lru_wa, 'lru_ba': lru_ba,
        'lru_wx': lru_wx, 'lru_bx': lru_bx, 'lru_lambda': lru_lambda, 'cd_w_out': cd_w_out,
        'ple_w_proj': ple_w_proj, 'ple_norm': ple_norm, 'ple_w_gate': ple_w_gate,
    }
    P = _prep_weights(w)
    n_even, n_odd = ab_w_in.shape[0], cd_w_in.shape[0]

    Bp, Lp, _ = x_prompt.shape
    assert Lp >= CONV_W - 1 and Lp % 64 == 0
    zero_states = (
        jnp.zeros((n_even, Bp, DN_HEADS, DN_DK, DN_DV), F32),
        jnp.zeros((n_even, Bp, CONV_W - 1, DN_QKV), F32),
        jnp.zeros((n_even, Bp, GLA_HEADS, GLA_DK, GLA_DV), F32),
        jnp.zeros((n_odd, Bp, SSD_HEADS, SSD_STATE, SSD_HEADDIM), F32),
        jnp.zeros((n_odd, Bp, CONV_W - 1, SSD_XBC), F32),
        jnp.zeros((n_odd, Bp, LRU_W), F32),
        jnp.zeros((n_odd, Bp, CONV_W - 1, LRU_W), F32),
    )
    y_p, st_p = _trunk(x_prompt, p_prompt, zero_states, P, lvalid=Lp, bb=1, tb=_pick_tb(Lp),
                       c_dn=64, c_gla=16, c_ssd=64)

    Bs, Ls, _ = x_sample.shape
    assert CONV_W - 1 <= Ls <= SUBLANES
    pad = SUBLANES - Ls
    xs = jnp.pad(x_sample, ((0, 0), (0, pad), (0, 0)))
    ps = jnp.pad(p_sample, ((0, 0), (0, 0), (0, pad), (0, 0)))
    sample_states = (state_dn, state_dn_conv, state_gla, state_ssd, state_ssd_conv, state_lru, state_lru_conv)
    bb_s = 8 if Bs % 8 == 0 else 1
    y_s, st_s = _trunk(xs, ps, sample_states, P, lvalid=Ls, bb=bb_s, tb=SUBLANES,
                       c_dn=SUBLANES, c_gla=SUBLANES, c_ssd=SUBLANES)
    return (y_p, y_s[:, :Ls], *st_p, *st_s)
```

```python
import functools

import jax
import jax.numpy as jnp
from jax import lax
from jax.experimental import pallas as pl
from jax.experimental.pallas import tpu as pltpu

F32 = jnp.float32
BF16 = jnp.bfloat16
HI = lax.Precision.HIGHEST

EPS = 1e-6
LOG2E = 1.4426950408889634
CONV_W = 4
PLE_DIM = 256
DN_HEADS, DN_DK, DN_DV = 8, 128, 128
DN_W = DN_HEADS * DN_DV
DN_QKV = 2 * DN_HEADS * DN_DK + DN_W
GLA_HEADS, GLA_DK, GLA_DV = 4, 128, 256
GLA_KW = GLA_HEADS * GLA_DK
GLA_W = GLA_HEADS * GLA_DV
GLA_RANK = 16
GLA_GATE_NORM = 16.0
AB_SIZES = (DN_QKV, DN_HEADS, DN_HEADS, DN_W, GLA_KW, GLA_KW, GLA_W, GLA_RANK, GLA_W)
SSD_HEADS, SSD_HEADDIM, SSD_GROUPS, SSD_STATE = 16, 64, 4, 64
SSD_W = SSD_HEADS * SSD_HEADDIM
SSD_XBC = SSD_W + 2 * SSD_GROUPS * SSD_STATE
LRU_W, LRU_BLOCKS = 1024, 16
LRU_BS = LRU_W // LRU_BLOCKS
LRU_C = 8.0
CD_SIZES = (SSD_W, SSD_XBC, SSD_HEADS, LRU_W, LRU_W)

LANES = 128
SUBLANES = 8
VMEM_LIMIT = 52 * 1024 * 1024

AB_QKV, AB_ZDN, AB_GQ, AB_GK, AB_GV, AB_ZGLA = 0, 3072, 4096, 4608, 5120, 6144
AB_MAIN = 7168
CD_Z, CD_ZLRU, CD_XLRU, CD_XBC = 0, 1024, 2048, 3072
CD_MAIN = 4608


def _silu(x):
    return x * jax.nn.sigmoid(x)


def _nt(a, b):
    return lax.dot_general(a, b, (((1,), (1,)), ((), ())), preferred_element_type=F32)


def _tn(a, b):
    return lax.dot_general(a, b, (((0,), (0,)), ((), ())), preferred_element_type=F32)


def _mm(a, b):
    return jnp.dot(a, b, preferred_element_type=F32)


def _mm_hi(a, b):
    return jnp.dot(a, b, preferred_element_type=F32, precision=HI)


def _iota2(shape, dim):
    return lax.broadcasted_iota(jnp.int32, shape, dim)


def _conv_rows(halo, x, w, bias=None):
    cat = jnp.concatenate([halo, x], axis=0)
    out = x * w[3:4]
    for s in (1, 2, 3):
        out = out + pltpu.roll(cat, s, axis=0)[SUBLANES:] * w[3 - s:4 - s]
    if bias is not None:
        out = out + bias
    return out


def _params(**kw):
    return pltpu.CompilerParams(vmem_limit_bytes=VMEM_LIMIT, **kw)


def _own_layer(st_ref, so_ref, carry):
    if carry is None:
        return st_ref, so_ref
    n, jl = carry
    for l in range(n):
        if l != jl:
            so_ref[l] = st_ref[l]
    return st_ref.at[jl], so_ref.at[jl]


def _state_specs(st_all, j, bb, carry):
    tail = st_all.shape[2:]
    zeros = (0,) * len(tail)
    if carry:
        return pl.BlockSpec((st_all.shape[0], bb) + tail, lambda b, *_: (0, b) + zeros)
    return pl.BlockSpec((None, bb) + tail, lambda b, *_: (j, b) + zeros)


def _inproj_kernel(h_ref, g_ref, wm_ref, ws_ref, um_ref, us_ref, xn_ref):
    @pl.when(pl.program_id(1) == 0)
    def _():
        x = h_ref[...]
        r = lax.rsqrt(jnp.mean(x * x, axis=-1, keepdims=True) + EPS)
        xn = (x * r * g_ref[...]).astype(BF16)
        xn_ref[...] = xn
        us_ref[...] = _nt(xn, ws_ref[...].astype(BF16))

    um_ref[...] = _nt(xn_ref[...], wm_ref[...])


def _inproj(h, g_all, li, wm_all, ws_all, j):
    T, D = h.shape
    Nm = wm_all.shape[1]
    tm = min(T, 1024)
    tn = 1024 if Nm % 1024 == 0 else 1536
    return pl.pallas_call(
        _inproj_kernel,
        grid=(T // tm, Nm // tn),
        in_specs=[pl.BlockSpec((tm, D), lambda i, c: (i, 0)),
                  pl.BlockSpec((None, 1, D), lambda i, c: (li, 0, 0)),
                  pl.BlockSpec((None, tn, D), lambda i, c: (j, c, 0)),
                  pl.BlockSpec((None, LANES, D), lambda i, c: (j, 0, 0))],
        out_specs=[pl.BlockSpec((tm, tn), lambda i, c: (i, c)),
                   pl.BlockSpec((tm, LANES), lambda i, c: (i, 0))],
        out_shape=[jax.ShapeDtypeStruct((T, Nm), F32), jax.ShapeDtypeStruct((T, LANES), F32)],
        scratch_shapes=[pltpu.VMEM((tm, D), BF16)],
        compiler_params=_params(dimension_semantics=("parallel", "arbitrary")),
        name="inproj",
    )(h, g_all, wm_all, ws_all)


def _outproj_kernel(oa_ref, ob_ref, h_ref, p_ref, wa_ref, wb_ref, pg_ref, wg_ref, wp_ref, fg_ref, out_ref, *, final):
    h1 = h_ref[...] + _mm(oa_ref[...], wa_ref[...]) + _mm(ob_ref[...], wb_ref[...])
    r = lax.rsqrt(jnp.mean(h1 * h1, axis=-1, keepdims=True) + EPS)
    hn = (h1 * r * pg_ref[...]).astype(BF16)
    gate = jax.nn.sigmoid(_mm(hn, wg_ref[...]))
    pp = _mm(p_ref[...].astype(BF16), wp_ref[...])
    h2 = h1 + gate * pp
    if final:
        r2 = lax.rsqrt(jnp.mean(h2 * h2, axis=-1, keepdims=True) + EPS)
        h2 = h2 * r2 * fg_ref[...]
    out_ref[...] = h2


def _outproj(oa, ob, h, p_all, li, w_out_all, j, pg_all, wg_all, wp_all, fg, final):
    T, D = h.shape
    Wa, Wb = oa.shape[1], ob.shape[1]
    assert Wa == Wb
    tm = min(T, 512)
    resident = lambda shape, idx: pl.BlockSpec(shape, lambda i: idx, pipeline_mode=pl.Buffered(1))
    return pl.pallas_call(
        functools.partial(_outproj_kernel, final=final),
        grid=(T // tm,),
        in_specs=[pl.BlockSpec((tm, Wa), lambda i: (i, 0)),
                  pl.BlockSpec((tm, Wb), lambda i: (i, 0)),
                  pl.BlockSpec((tm, D), lambda i: (i, 0)),
                  pl.BlockSpec((None, tm, PLE_DIM), lambda i: (li, i, 0)),
                  resident((None, Wa, D), (j, 0, 0)),
                  resident((None, Wb, D), (j, 1, 0)),
                  resident((None, 1, D), (li, 0, 0)),
                  resident((None, D, D), (li, 0, 0)),
                  resident((None, PLE_DIM, D), (li, 0, 0)),
                  resident((1, D), (0, 0))],
        out_specs=pl.BlockSpec((tm, D), lambda i: (i, 0)),
        out_shape=jax.ShapeDtypeStruct((T, D), F32),
        compiler_params=_params(dimension_semantics=("parallel",)),
        name="outproj",
    )(oa, ob, h, p_all, w_out_all, w_out_all, pg_all, wg_all, wp_all, fg)


R_BD = 128


def _bd_masks(C):
    ii = _iota2((R_BD, R_BD), 0)
    jj = _iota2((R_BD, R_BD), 1)
    sh = C.bit_length() - 1
    lower = ((ii >> sh) == (jj >> sh)) & (jj <= ii)
    neg = jnp.where(lower, 0.0, -1e30)
    eye = (ii == jj).astype(F32)
    return neg, 1.0 - eye, eye


def _dn_groups(Q, K, V, beta, gc, gr, neg, offdiag, eye, C):
    n = range(len(Q))
    Kb = [K[g].astype(BF16) for g in n]
    KK = [_nt(Kb[g], Kb[g]) for g in n]
    QK = [_nt(Q[g].astype(BF16), Kb[g]) for g in n]
    dm = [jnp.exp((gc[g] - gr[g]) + neg) for g in n]
    A = [(beta[g] * KK[g]) * (dm[g] * offdiag) for g in n]
    T = [eye - A[g] for g in n]
    X = A
    p = 2
    while p < C:
        Xb = [X[g].astype(BF16) for g in n]
        X = [_mm(Xb[g], Xb[g]) for g in n]
        T = [T[g] + _mm(T[g].astype(BF16), X[g].astype(BF16)) for g in n]
        p *= 2
    eg = [jnp.exp(gc[g]) for g in n]
    rhs = [jnp.concatenate([beta[g] * V[g], (beta[g] * eg[g]) * K[g]], axis=1).astype(BF16) for g in n]
    sol = [_mm(T[g].astype(BF16), rhs[g]) for g in n]
    return ([sol[g][:, :DN_DV] for g in n], [sol[g][:, DN_DV:] for g in n],
            [QK[g] * dm[g] for g in n], [Q[g] * eg[g] for g in n])


def _l2n(x):
    return x * lax.rsqrt(jnp.sum(x * x, axis=-1, keepdims=True) + EPS)


def _dn_apply(S, u0, wk, qkd, qg, K, gl, gc, C):
    n = range(len(S))
    nu = len(S[0])
    blk = lambda x, i: x[i * C:(i + 1) * C]
    r = [[None] * nu for _ in n]
    for i in range(nu):
        for g in n:
            lhs = jnp.concatenate([blk(wk[g], i), blk(qg[g], i)], axis=0).astype(BF16)
            r[g][i] = _mm(lhs, S[g][i].astype(BF16))
    Ub = [jnp.concatenate([blk(u0[g], i) - r[g][i][:C] for i in range(nu)], axis=0).astype(BF16) for g in n]
    O = [jnp.concatenate([r[g][i][C:] for i in range(nu)], axis=0) + _mm(qkd[g].astype(BF16), Ub[g]) for g in n]
    kend = [(K[g] * jnp.exp(gl[g] - gc[g])).astype(BF16) for g in n]
    egl = [jnp.exp(gl[g]) for g in n]
    S_new = [[None] * nu for _ in n]
    for i in range(nu):
        for g in n:
            S_new[g][i] = egl[g][i * C:i * C + 1, :] * S[g][i] + _tn(blk(kend[g], i), blk(Ub[g], i))
    return O, S_new


def _dn_prompt_kernel(qkv_ref, z_ref, us_ref, cst_ref, st_ref, cw_ref, prm_ref, nrm_ref, o_ref, so_ref, halo_ref,
                      *, tb, C, nc):
    t = pl.program_id(1)
    hg = R_BD // C
    CR = nc * C

    @pl.when(t == 0)
    def _():
        so_ref[...] = st_ref[...]
        halo_ref[0:5, :] = jnp.zeros((5, DN_QKV), F32)
        halo_ref[5:8, :] = cst_ref[...]

    neg, offdiag, eye = _bd_masks(C)
    ii = _iota2((CR, CR), 0)
    jj = _iota2((CR, CR), 1)
    sh = C.bit_length() - 1
    tril = (((ii >> sh) == (jj >> sh)) & (jj <= ii)).astype(F32)
    lane = _iota2((1, LANES), 1)
    gmask = (lane >= DN_HEADS) & (lane < 2 * DN_HEADS)
    neg_a = -jnp.exp(prm_ref[0:1, :])
    bias = prm_ref[1:2, :]
    cw = cw_ref[...]
    nrm = nrm_ref[...]
    head_sets = [range(gi * hg, (gi + 1) * hg) for gi in range(DN_HEADS // hg)]
    groups = [(ci, hs) for ci in range(nc) for hs in head_sets]
    rws = lambda ci: slice(ci * C, (ci + 1) * C)

    def step(cp, carry):
        r0 = pl.multiple_of(cp * CR, CR)
        x = qkv_ref[pl.ds(r0, CR), :]
        prev = qkv_ref[pl.ds(pl.multiple_of(jnp.maximum(r0 - SUBLANES, 0), SUBLANES), SUBLANES), :]
        halo = jnp.where(cp == 0, halo_ref[...], prev)
        act = _silu(_conv_rows(halo, x, cw))
        us = us_ref[pl.ds(r0, CR), :]
        g = jnp.where(gmask, neg_a * jax.nn.softplus(us + bias), 0.0)
        beta_all = jax.nn.sigmoid(us)
        G = _mm_hi(tril, g)
        GT = G.T
        z = z_ref[pl.ds(r0, CR), :]
        cat0 = lambda f: [jnp.concatenate([f(ci, h) for h in hs], axis=0) for ci, hs in groups]
        Q = [_l2n(a) * (DN_DK ** -0.5) for a in cat0(lambda ci, h: act[rws(ci), h * DN_DK:(h + 1) * DN_DK])]
        K = [_l2n(a) for a in cat0(lambda ci, h: act[rws(ci), DN_W + h * DN_DK:DN_W + (h + 1) * DN_DK])]
        V = cat0(lambda ci, h: act[rws(ci), 2 * DN_W + h * DN_DV:2 * DN_W + (h + 1) * DN_DV])
        beta = cat0(lambda ci, h: beta_all[rws(ci), h:h + 1])
        gc = cat0(lambda ci, h: G[rws(ci), DN_HEADS + h:DN_HEADS + h + 1])
        gl = cat0(lambda ci, h: jnp.broadcast_to(
            G[(ci + 1) * C - 1:(ci + 1) * C, DN_HEADS + h:DN_HEADS + h + 1], (C, 1)))
        gr = [jnp.concatenate([GT[DN_HEADS + h:DN_HEADS + h + 1, rws(ci)] for h in hs], axis=1)
              for ci, hs in groups]
        u0, wk, qkd, qg = _dn_groups(Q, K, V, beta, gc, gr, neg, offdiag, eye, C)
        Z = cat0(lambda ci, h: z[rws(ci), h * DN_DV:(h + 1) * DN_DV])
        S = [[so_ref[h] for h in hs] for hs in head_sets]
        ng = len(head_sets)
        for ci in range(nc):
            pick = lambda xs: xs[ci * ng:(ci + 1) * ng]
            O, S = _dn_apply(S, pick(u0), pick(wk), pick(qkd), pick(qg), pick(K), pick(gl), pick(gc), C)
            for gi, hs in enumerate(head_sets):
                o = O[gi] * lax.rsqrt(jnp.mean(O[gi] * O[gi], axis=-1, keepdims=True) + EPS) * nrm
                out = (o * _silu(Z[ci * ng + gi])).astype(o_ref.dtype)
                for i, h in enumerate(hs):
                    o_ref[pl.ds(r0 + ci * C, C), h * DN_DV:(h + 1) * DN_DV] = out[i * C:(i + 1) * C]
        for gi, hs in enumerate(head_sets):
            for i, h in enumerate(hs):
                so_ref[h] = S[gi][i]
        return carry

    lax.fori_loop(0, tb // CR, step, 0)
    halo_ref[...] = qkv_ref[tb - SUBLANES:tb, :]


def _dn_sample_kernel(qkv_ref, z_ref, us_ref, cst_ref, st_ref, cw_ref, prm_ref, nrm_ref, o_ref, so_ref, halo_ref,
                      *, bb, lvalid, carry):
    st_ref, so_ref = _own_layer(st_ref, so_ref, carry)
    C = SUBLANES
    nb = R_BD // (DN_HEADS * C)
    rows_all = bb * C

    halo_ref[:, 0:5, :] = jnp.zeros((bb, 5, DN_QKV), F32)
    halo_ref[:, 5:8, :] = cst_ref[...]
    cw = cw_ref[...]
    nrm = nrm_ref[...]
    acts = [_silu(_conv_rows(halo_ref[b], qkv_ref[b], cw)) for b in range(bb)]

    neg, offdiag, eye = _bd_masks(C)
    lane = _iota2((1, LANES), 1)
    gmask = (lane >= DN_HEADS) & (lane < 2 * DN_HEADS)
    neg_a = -jnp.exp(prm_ref[0:1, :])
    bias = prm_ref[1:2, :]
    us = us_ref[...].reshape(rows_all, LANES)
    valid = (_iota2((rows_all, 1), 0) & (C - 1)) < lvalid
    g = jnp.where(gmask & valid, neg_a * jax.nn.softplus(us + bias), 0.0)
    beta_all = jnp.where(valid, jax.nn.sigmoid(us), 0.0)
    ii = _iota2((rows_all, rows_all), 0)
    jj = _iota2((rows_all, rows_all), 1)
    same = (ii >> 3) == (jj >> 3)
    G_all = _mm_hi((same & (jj <= ii)).astype(F32), g)
    GL_all = _mm_hi(same.astype(F32), g)
    GT_all = G_all.T

    groups = [[(h, gi * nb + bl) for h in range(DN_HEADS) for bl in range(nb)]
              for gi in range(bb // nb)]
    rows = [slice(gi * nb * C, (gi + 1) * nb * C) for gi in range(bb // nb)]
    cat0 = lambda f: [jnp.concatenate([f(h, b) for h, b in us_], axis=0) for us_ in groups]
    cath = lambda f: [jnp.concatenate([f(h, rs) for h in range(DN_HEADS)], axis=0) for rs in rows]
    Q = [_l2n(x) * (DN_DK ** -0.5) for x in cat0(lambda h, b: acts[b][:, h * DN_DK:(h + 1) * DN_DK])]
    K = [_l2n(x) for x in cat0(lambda h, b: acts[b][:, DN_W + h * DN_DK:DN_W + (h + 1) * DN_DK])]
    V = cat0(lambda h, b: acts[b][:, 2 * DN_W + h * DN_DV:2 * DN_W + (h + 1) * DN_DV])
    beta = cath(lambda h, rs: beta_all[rs, h:h + 1])
    gc = cath(lambda h, rs: G_all[rs, DN_HEADS + h:DN_HEADS + h + 1])
    gl = cath(lambda h, rs: GL_all[rs, DN_HEADS + h:DN_HEADS + h + 1])
    gr = [jnp.concatenate([GT_all[DN_HEADS + h:DN_HEADS + h + 1, rs] for h in range(DN_HEADS)], axis=1)
          for rs in rows]
    u0, wk, qkd, qg = _dn_groups(Q, K, V, beta, gc, gr, neg, offdiag, eye, C)
    S = [[st_ref[b, h] for h, b in us_] for us_ in groups]
    O, S_new = _dn_apply(S, u0, wk, qkd, qg, K, gl, gc, C)
    Z = cat0(lambda h, b: z_ref[b][:, h * DN_DV:(h + 1) * DN_DV])
    for g, us_ in enumerate(groups):
        o = O[g] * lax.rsqrt(jnp.mean(O[g] * O[g], axis=-1, keepdims=True) + EPS) * nrm
        out = (o * _silu(Z[g])).astype(o_ref.dtype)
        for i, (h, b) in enumerate(us_):
            so_ref[b, h] = S_new[g][i]
            o_ref[b, :, h * DN_DV:(h + 1) * DN_DV] = out[i * C:(i + 1) * C]


def _dn_mixer(um, us, cst_all, st_all, j, cw, prm, nrm, *, bb, tb, C, lvalid):
    B, L, _ = um.shape
    const = lambda *_: (0, 0)
    out_shape = [jax.ShapeDtypeStruct((B, L, DN_W), BF16), jax.ShapeDtypeStruct(st_all.shape, F32)]
    carry = None
    if lvalid == L:
        assert bb == 1
        kern = functools.partial(_dn_prompt_kernel, tb=tb, C=C, nc=4)
        grid = (B, L // tb)
        in_specs = [pl.BlockSpec((None, tb, DN_QKV), lambda b, t: (b, t, AB_QKV // DN_QKV)),
                    pl.BlockSpec((None, tb, DN_W), lambda b, t: (b, t, AB_ZDN // DN_W)),
                    pl.BlockSpec((None, tb, LANES), lambda b, t: (b, t, 0)),
                    pl.BlockSpec((None, None, CONV_W - 1, DN_QKV), lambda b, t: (j, b, 0, 0)),
                    pl.BlockSpec((None, None, DN_HEADS, DN_DK, DN_DV), lambda b, t: (j, b, 0, 0, 0)),
                    pl.BlockSpec((CONV_W, DN_QKV), const),
                    pl.BlockSpec((2, LANES), const),
                    pl.BlockSpec((1, DN_DV), const)]
        out_specs = [pl.BlockSpec((None, tb, DN_W), lambda b, t: (b, t, 0)),
                     pl.BlockSpec((None, None, DN_HEADS, DN_DK, DN_DV), lambda b, t: (j, b, 0, 0, 0))]
        scratch = [pltpu.VMEM((SUBLANES, DN_QKV), F32)]
        sem = ("parallel", "arbitrary")
    else:
        assert L == SUBLANES and C == SUBLANES
        carry = (st_all.shape[0], j) if j == 0 else None
        kern = functools.partial(_dn_sample_kernel, bb=bb, lvalid=lvalid, carry=carry)
        grid = (B // bb,)
        in_specs = [pl.BlockSpec((bb, L, DN_QKV), lambda b: (b, 0, AB_QKV // DN_QKV)),
                    pl.BlockSpec((bb, L, DN_W), lambda b: (b, 0, AB_ZDN // DN_W)),
                    pl.BlockSpec((bb, L, LANES), lambda b: (b, 0, 0)),
                    pl.BlockSpec((None, bb, CONV_W - 1, DN_QKV), lambda b: (j, b, 0, 0)),
                    _state_specs(st_all, j, bb, carry),
                    pl.BlockSpec((CONV_W, DN_QKV), const),
                    pl.BlockSpec((2, LANES), const),
                    pl.BlockSpec((1, DN_DV), const)]
        out_specs = [pl.BlockSpec((bb, L, DN_W), lambda b: (b, 0, 0)), _state_specs(st_all, j, bb, carry)]
        scratch = [pltpu.VMEM((bb, SUBLANES, DN_QKV), F32)]
        sem = ("parallel",)
    return pl.pallas_call(
        kern, grid=grid, in_specs=in_specs, out_specs=out_specs, out_shape=out_shape,
        scratch_shapes=scratch, input_output_aliases={} if carry else {4: 1},
        compiler_params=_params(dimension_semantics=sem), name="dn_mixer",
    )(um, um, us, cst_all, st_all, cw, prm, nrm)


def _gla_sample_kernel(q_ref, k_ref, v_ref, z_ref, us_ref, st_ref, wa_ref, ba_ref, nrm_ref, o_ref, so_ref,
                       *, bb, lvalid, carry):
    st_ref, so_ref = _own_layer(st_ref, so_ref, carry)
    C = SUBLANES
    R = bb * C
    ii = _iota2((C, C), 0)
    jj = _iota2((C, C), 1)
    nrm = nrm_ref[...]
    valid = (_iota2((R, 1), 0) & (C - 1)) < lvalid
    us = us_ref[...].reshape(R, LANES)
    glog = jax.nn.log_sigmoid(_mm(us.astype(BF16), wa_ref[...]) + ba_ref[...]) * (1.0 / GLA_GATE_NORM)
    glog = jnp.where(valid, glog, 0.0)
    ri = _iota2((R, R), 0)
    ci = _iota2((R, R), 1)
    G = _mm_hi((((ri >> 3) == (ci >> 3)) & (ci <= ri)).astype(F32), glog)
    q = q_ref[...].reshape(R, GLA_KW) * (GLA_DK ** -0.5)
    k = jnp.where(valid, k_ref[...].reshape(R, GLA_KW), 0.0)
    v = jnp.where(valid, v_ref[...].reshape(R, GLA_W), 0.0)
    z = z_ref[...].reshape(R, GLA_W)
    units = [(b, h) for b in range(bb) for h in range(GLA_HEADS)]
    U = range(len(units))
    tile = lambda x, b, h, w: x[b * C:(b + 1) * C, h * w:(h + 1) * w]
    Gu = [tile(G, b, h, GLA_DK) for b, h in units]
    qu = [tile(q, b, h, GLA_DK) for b, h in units]
    ku = [tile(k, b, h, GLA_DK) for b, h in units]
    vu = [tile(v, b, h, GLA_DV).astype(BF16) for b, h in units]
    AT = [jnp.zeros((C, C), F32) for _ in U]
    for i in range(C):
        for u in U:
            e = jnp.exp(jnp.minimum(Gu[u][i:i + 1, :] - Gu[u], 0.0))
            col = jnp.sum(ku[u] * e * qu[u][i:i + 1, :], axis=1, keepdims=True)
            AT[u] = AT[u] + jnp.where(jj == i, col, 0.0)
    AT = [jnp.where(ii <= jj, AT[u], 0.0).astype(BF16) for u in U]
    S = [st_ref[b, h] for b, h in units]
    o = [_mm((qu[u] * jnp.exp(Gu[u])).astype(BF16), S[u].astype(BF16)) + _tn(AT[u], vu[u]) for u in U]
    kend = [(ku[u] * jnp.exp(Gu[u][C - 1:C, :] - Gu[u])).astype(BF16) for u in U]
    gend = [jnp.exp(Gu[u].T[:, C - 1:C]) for u in U]
    for u, (b, h) in enumerate(units):
        so_ref[b, h] = gend[u] * S[u] + _tn(kend[u], vu[u])
    o = [o[u] * lax.rsqrt(jnp.mean(o[u] * o[u], axis=-1, keepdims=True) + EPS) * nrm for u in U]
    o = [o[u] * _silu(tile(z, b, h, GLA_DV)) for u, (b, h) in enumerate(units)]
    rows = [jnp.concatenate(o[b * GLA_HEADS:(b + 1) * GLA_HEADS], axis=1) for b in range(bb)]
    o_ref[...] = jnp.concatenate(rows, axis=0).astype(o_ref.dtype).reshape(bb, C, GLA_W)


def _gla_prompt_kernel(q_ref, k_ref, v_ref, z_ref, us_ref, st_ref, wa_ref, ba_ref, nrm_ref, o_ref, so_ref, g_ref,
                       *, tb, C):
    t = pl.program_id(1)
    H = range(GLA_HEADS)
    nch = tb // C

    @pl.when(t == 0)
    def _():
        so_ref[...] = st_ref[...]

    glog = jax.nn.log_sigmoid(_mm(us_ref[...].astype(BF16), wa_ref[...]) + ba_ref[...]) * (1.0 / GLA_GATE_NORM)
    step_in_chunk = _iota2((tb, 1), 0) & (C - 1)
    d = 1
    while d < C:
        glog = glog + jnp.where(step_in_chunk >= d, pltpu.roll(glog, d, axis=0), 0.0)
        d *= 2
    g_ref[...] = glog * LOG2E

    ii = _iota2((C, C), 0)
    jj = _iota2((C, C), 1)
    nrm = nrm_ref[...]
    hs = lambda x, h, w: x[:, h * w:(h + 1) * w]

    def scores(c):
        rows = pl.ds(pl.multiple_of(c * C, C), C)
        G = g_ref[rows, :]
        q = q_ref[rows, :] * (GLA_DK ** -0.5)
        k = k_ref[rows, :]
        nt = C // SUBLANES
        Gt = [[hs(G, h, GLA_DK)[t * SUBLANES:(t + 1) * SUBLANES] for t in range(nt)] for h in H]
        kt = [[hs(k, h, GLA_DK)[t * SUBLANES:(t + 1) * SUBLANES] for t in range(nt)] for h in H]
        acc = [[jnp.zeros((SUBLANES, C), F32) for _ in range(nt)] for _ in H]
        for i in range(C):
            for t in range(i // SUBLANES + 1):
                for h in H:
                    e = jnp.exp2(jnp.minimum(hs(G, h, GLA_DK)[i:i + 1, :] - Gt[h][t], 0.0))
                    col = jnp.sum(kt[h][t] * e * hs(q, h, GLA_DK)[i:i + 1, :], axis=1, keepdims=True)
                    acc[h][t] = acc[h][t] + jnp.where(jj[:SUBLANES] == i, col, 0.0)
        AT = [jnp.where(ii <= jj, jnp.concatenate(acc[h], axis=0), 0.0).astype(BF16) for h in H]
        gend = [jnp.exp2(hs(G, h, GLA_DK).T[:, C - 1:C]) for h in H]
        return AT, gend

    def finish(o, c):
        rows = pl.ds(pl.multiple_of(c * C, C), C)
        z = z_ref[rows, :]
        o = [o[h] * lax.rsqrt(jnp.mean(o[h] * o[h], axis=-1, keepdims=True) + EPS) * nrm for h in H]
        o = [o[h] * _silu(hs(z, h, GLA_DV)) for h in H]
        o_ref[rows, :] = jnp.concatenate(o, axis=1).astype(o_ref.dtype)

    def body(c, carry):
        AT, gend, o_prev = carry
        nxt = scores(jnp.minimum(c + 1, nch - 1))
        rows = pl.ds(pl.multiple_of(c * C, C), C)
        G = g_ref[rows, :]
        q = q_ref[rows, :] * (GLA_DK ** -0.5)
        k = k_ref[rows, :]
        v = v_ref[rows, :]
        S = [so_ref[h] for h in H]
        vh = [hs(v, h, GLA_DV).astype(BF16) for h in H]
        o = [_mm((hs(q, h, GLA_DK) * jnp.exp2(hs(G, h, GLA_DK))).astype(BF16), S[h].astype(BF16)) + _tn(AT[h], vh[h])
             for h in H]
        kend = [(hs(k, h, GLA_DK) * jnp.exp2(hs(G, h, GLA_DK)[C - 1:C, :] - hs(G, h, GLA_DK))).astype(BF16) for h in H]
        for h in H:
            so_ref[h] = gend[h] * S[h] + _tn(kend[h], vh[h])
        finish(o_prev, jnp.maximum(c - 1, 0))
        return nxt + (o,)

    AT0, gend0 = scores(0)
    _, _, o_last = lax.fori_loop(0, nch, body, (AT0, gend0, [jnp.zeros((C, GLA_DV), F32) for _ in H]))
    finish(o_last, nch - 1)


def _gla_mixer(um, us, st_all, j, wa, ba, nrm, *, bb, tb, C, lvalid):
    B, L, _ = um.shape
    const = lambda *_: (0, 0)
    carry = None
    if lvalid == L:
        assert bb == 1
        kern = functools.partial(_gla_prompt_kernel, tb=tb, C=C)
        sq = None
        scratch = [pltpu.VMEM((tb, GLA_KW), F32)]
        st_spec = pl.BlockSpec((None, None, GLA_HEADS, GLA_DK, GLA_DV), lambda b, t: (j, b, 0, 0, 0))
    else:
        assert L == SUBLANES and tb == L
        carry = (st_all.shape[0], j) if j == 0 else None
        kern = functools.partial(_gla_sample_kernel, bb=bb, lvalid=lvalid, carry=carry)
        sq = bb
        scratch = []
        st_spec = _state_specs(st_all, j, bb, carry)
    return pl.pallas_call(
        kern,
        grid=(B // bb, L // tb),
        in_specs=[pl.BlockSpec((sq, tb, GLA_KW), lambda b, t: (b, t, AB_GQ // GLA_KW)),
                  pl.BlockSpec((sq, tb, GLA_KW), lambda b, t: (b, t, AB_GK // GLA_KW)),
                  pl.BlockSpec((sq, tb, GLA_W), lambda b, t: (b, t, AB_GV // GLA_W)),
                  pl.BlockSpec((sq, tb, GLA_W), lambda b, t: (b, t, AB_ZGLA // GLA_W)),
                  pl.BlockSpec((sq, tb, LANES), lambda b, t: (b, t, 0)),
                  st_spec,
                  pl.BlockSpec((LANES, GLA_KW), const),
                  pl.BlockSpec((1, GLA_KW), const),
                  pl.BlockSpec((1, GLA_DV), const)],
        out_specs=[pl.BlockSpec((sq, tb, GLA_W), lambda b, t: (b, t, 0)), st_spec],
        out_shape=[jax.ShapeDtypeStruct((B, L, GLA_W), BF16), jax.ShapeDtypeStruct(st_all.shape, F32)],
        scratch_shapes=scratch,
        input_output_aliases={} if carry else {5: 1},
        compiler_params=_params(dimension_semantics=("parallel", "arbitrary")),
        name="gla_mixer",
    )(um, um, um, um, us, st_all, wa, ba, nrm)


SSD_GW = SSD_W // SSD_GROUPS
SSD_REP = SSD_HEADS // SSD_GROUPS
SSD_ROWS = 64


def _lane_expand(x, h0, width):
    return jnp.concatenate([jnp.broadcast_to(x[:, h0 + j:h0 + j + 1], (x.shape[0], width)) for j in range(SSD_REP)],
                           axis=1)


def _ssd_masks(C):
    ii = _iota2((SSD_ROWS, SSD_GW), 0)
    jj = _iota2((SSD_ROWS, SSD_GW), 1) & (SSD_ROWS - 1)
    sh = C.bit_length() - 1
    negc = jnp.where(((ii >> sh) == (jj >> sh)) & (jj <= ii), 0.0, -1e30)
    ri = _iota2((SSD_GW, SSD_GW), 0)
    ci = _iota2((SSD_GW, SSD_GW), 1)
    bd = ((ri >> 6) == (ci >> 6)).astype(F32)
    return negc, bd


def _ssd_intra(act, G, GT, dt, gi, negc, bd):
    h0 = gi * SSD_REP
    xs = act[:, gi * SSD_GW:(gi + 1) * SSD_GW]
    bm = act[:, SSD_W + gi * SSD_STATE:SSD_W + (gi + 1) * SSD_STATE].astype(BF16)
    cm = act[:, SSD_W + (SSD_GROUPS + gi) * SSD_STATE:SSD_W + (SSD_GROUPS + gi + 1) * SSD_STATE].astype(BF16)
    gcc = _lane_expand(G, h0, SSD_HEADDIM)
    grc = jnp.concatenate([GT[h0 + j:h0 + j + 1, :] for j in range(SSD_REP)], axis=1)
    vcat = xs * _lane_expand(dt, h0, SSD_HEADDIM)
    cbm = _nt(cm, bm)
    acat = (jnp.concatenate([cbm] * SSD_REP, axis=1) * jnp.exp((gcc - grc) + negc)).astype(BF16)
    vbd = (jnp.concatenate([vcat] * SSD_REP, axis=0) * bd).astype(BF16)
    return _mm(acat, vbd), xs, vcat, gcc, bm, cm


def _ssd_finish(o, xs, z, dsk, nrm):
    y = (o + dsk * xs) * _silu(z)
    return y * lax.rsqrt(jnp.mean(y * y, axis=-1, keepdims=True) + EPS) * nrm


def _ssd_prompt_kernel(z_ref, xbc_ref, us_ref, cst_ref, st_ref, cw_ref, cb_ref, prm_ref, nrm_ref, o_ref, so_ref,
                       halo_ref, scat_ref, *, tb):
    t = pl.program_id(1)
    C = SSD_ROWS
    cat_heads = lambda f, gi: jnp.concatenate([f(gi * SSD_REP + j) for j in range(SSD_REP)], axis=1)

    @pl.when(t == 0)
    def _():
        for gi in range(SSD_GROUPS):
            scat_ref[gi] = cat_heads(lambda h: st_ref[h], gi)
        halo_ref[0:5, :] = jnp.zeros((5, SSD_XBC), F32)
        halo_ref[5:8, :] = cst_ref[...]

    negc, bd = _ssd_masks(C)
    tril = (_iota2((C, C), 1) <= _iota2((C, C), 0)).astype(F32)
    lane = _iota2((1, LANES), 1)
    hmask = lane < SSD_HEADS
    neg_a = -jnp.exp(prm_ref[0:1, :])
    bias = prm_ref[1:2, :]
    dsk = [_lane_expand(prm_ref[2:3, :], gi * SSD_REP, SSD_HEADDIM) for gi in range(SSD_GROUPS)]
    cw = cw_ref[...]
    cb = cb_ref[...]
    nrm = nrm_ref[...]
    groups = range(SSD_GROUPS)

    def chunk(c, carry):
        r0 = pl.multiple_of(c * C, C)
        x = xbc_ref[pl.ds(r0, C), :]
        prev = xbc_ref[pl.ds(pl.multiple_of(jnp.maximum(r0 - SUBLANES, 0), SUBLANES), SUBLANES), :]
        halo = jnp.where(c == 0, halo_ref[...], prev)
        act = _silu(_conv_rows(halo, x, cw, cb))
        us = us_ref[pl.ds(r0, C), :]
        dt = jnp.where(hmask, jax.nn.softplus(us + bias), 0.0)
        G = _mm_hi(tril, neg_a * dt)
        GT = G.T
        z = z_ref[pl.ds(r0, C), :]
        parts = [_ssd_intra(act, G, GT, dt, gi, negc, bd) for gi in groups]
        S = [scat_ref[gi] for gi in groups]
        inter = [_mm(parts[gi][5], S[gi].astype(BF16)) for gi in groups]
        for gi in groups:
            o_in, xs, vcat, gcc, bm, cm = parts[gi]
            glc = gcc[C - 1:C, :]
            w = (vcat * jnp.exp(glc - gcc)).astype(BF16)
            scat_ref[gi] = jnp.exp(glc) * S[gi] + _tn(bm, w)
            y = _ssd_finish(jnp.exp(gcc) * inter[gi] + o_in, xs, z[:, gi * SSD_GW:(gi + 1) * SSD_GW], dsk[gi],
                            nrm[:, gi * SSD_GW:(gi + 1) * SSD_GW])
            o_ref[pl.ds(r0, C), gi * SSD_GW:(gi + 1) * SSD_GW] = y.astype(o_ref.dtype)
        return carry

    lax.fori_loop(0, tb // C, chunk, 0)
    halo_ref[...] = xbc_ref[tb - SUBLANES:tb, :]

    @pl.when(t == pl.num_programs(1) - 1)
    def _():
        for gi in groups:
            s = scat_ref[gi]
            for j in range(SSD_REP):
                so_ref[gi * SSD_REP + j] = s[:, j * SSD_HEADDIM:(j + 1) * SSD_HEADDIM]


def _ssd_sample_kernel(xs_ref, bm_ref, cm_ref, z_ref, dt_ref, cxs_ref, cbm_ref, ccm_ref, st_ref, wxs_ref, wbm_ref,
                       wcm_ref, prm_ref, nrm_ref, o_ref, so_ref, acc_ref, *, ls, carry):
    st_ref, so_ref = _own_layer(st_ref, so_ref, carry)
    h = pl.program_id(0)
    hj = h % SSD_REP

    def conv(x_ref, c_ref, w_ref):
        xx = [c_ref[r] for r in range(CONV_W - 1)] + [x_ref[t] for t in range(ls)]
        return [_silu(sum(xx[t + s] * w_ref[s] for s in range(CONV_W)) + w_ref[CONV_W]) for t in range(ls)]

    xs = conv(xs_ref, cxs_ref, wxs_ref)
    bm = conv(bm_ref, cbm_ref, wbm_ref)
    cm = conv(cm_ref, ccm_ref, wcm_ref)
    row = lambda ref, k: ref[k, pl.ds(h, 1), :]
    dt = [jax.nn.softplus(dt_ref[t, pl.ds(h, 1), :] + row(prm_ref, 1)) for t in range(ls)]
    neg_a = -jnp.exp(row(prm_ref, 0))
    G = []
    for t in range(ls):
        G.append(neg_a * dt[t] + (G[-1] if G else 0.0))
    v = [xs[t] * dt[t] for t in range(ls)]
    y = []
    for t in range(ls):
        acc = None
        for u in range(t + 1):
            coef = jnp.sum(cm[t] * bm[u], axis=0, keepdims=True) * jnp.exp(G[t] - G[u])
            acc = coef * v[u] if acc is None else acc + coef * v[u]
        y.append(acc)
    eg = [jnp.exp(G[t]) for t in range(ls)]
    wk = [bm[u] * jnp.exp(G[ls - 1] - G[u]) for u in range(ls)]
    inter = [None] * ls
    for n in range(SSD_STATE):
        S = st_ref[0, n]
        for t in range(ls):
            term = cm[t][n:n + 1, :] * S
            inter[t] = term if inter[t] is None else inter[t] + term
        s_new = eg[ls - 1] * S
        for u in range(ls):
            s_new = s_new + wk[u][n:n + 1, :] * v[u]
        so_ref[0, n] = s_new
    dsk = row(prm_ref, 2)
    for t in range(ls):
        acc_ref[t, pl.ds(pl.multiple_of(hj * SSD_HEADDIM, SSD_HEADDIM), SSD_HEADDIM), :] = (
            (y[t] + eg[t] * inter[t] + dsk * xs[t]) * _silu(z_ref[t]))

    @pl.when(hj == SSD_REP - 1)
    def _():
        for t in range(ls):
            yg = acc_ref[t]
            r = lax.rsqrt(jnp.mean(yg * yg, axis=0, keepdims=True) + EPS)
            o_ref[t] = (yg * r * nrm_ref[...]).astype(o_ref.dtype)


def _ssd_mixer(um, us, cst_all, st_all, j, cw, cb, prm, nrm, *, bb, tb, lvalid):
    B, L, _ = um.shape
    if lvalid != L:
        return _ssd_mixer_sample(um, us, cst_all, st_all, j, cw, cb, prm, nrm, lvalid=lvalid)
    const = lambda *_: (0, 0)
    out_shape = [jax.ShapeDtypeStruct((B, L, SSD_W), BF16), jax.ShapeDtypeStruct(st_all.shape, F32)]
    st_blk = (SSD_HEADS, SSD_STATE, SSD_HEADDIM)
    assert bb == 1 and tb % SSD_ROWS == 0
    return pl.pallas_call(
        functools.partial(_ssd_prompt_kernel, tb=tb),
        grid=(B, L // tb),
        in_specs=[pl.BlockSpec((None, tb, SSD_W), lambda b, t: (b, t, CD_Z // SSD_W)),
                  pl.BlockSpec((None, tb, SSD_XBC), lambda b, t: (b, t, CD_XBC // SSD_XBC)),
                  pl.BlockSpec((None, tb, LANES), lambda b, t: (b, t, 0)),
                  pl.BlockSpec((None, None, CONV_W - 1, SSD_XBC), lambda b, t: (j, b, 0, 0)),
                  pl.BlockSpec((None, None) + st_blk, lambda b, t: (j, b, 0, 0, 0)),
                  pl.BlockSpec((CONV_W, SSD_XBC), const), pl.BlockSpec((1, SSD_XBC), const),
                  pl.BlockSpec((3, LANES), const), pl.BlockSpec((1, SSD_W), const)],
        out_specs=[pl.BlockSpec((None, tb, SSD_W), lambda b, t: (b, t, 0)),
                   pl.BlockSpec((None, None) + st_blk, lambda b, t: (j, b, 0, 0, 0))],
        out_shape=out_shape,
        scratch_shapes=[pltpu.VMEM((SUBLANES, SSD_XBC), F32), pltpu.VMEM((SSD_GROUPS, SSD_STATE, SSD_GW), F32)],
        input_output_aliases={4: 1},
        compiler_params=_params(dimension_semantics=("parallel", "arbitrary")), name="ssd_mixer",
    )(um, um, us, cst_all, st_all, cw, cb, prm, nrm)


def _ssd_mixer_sample(um, us, cst_all, st_all, j, cw, cb, prm, nrm, *, lvalid):
    B = um.shape[0]
    ls = lvalid
    n_layers = st_all.shape[0]
    chan_major = lambda x: jnp.transpose(x, (1, 2, 0))
    xbc_t = chan_major(um[:, :ls, CD_XBC:CD_XBC + SSD_XBC])
    z_t = chan_major(um[:, :ls, CD_Z:CD_Z + SSD_W])
    dt_t = chan_major(us[:, :ls, :SSD_HEADS])
    cst_t = chan_major(cst_all[j])
    st_t = jnp.transpose(st_all, (0, 2, 3, 4, 1))
    lanes = lambda x: jnp.broadcast_to(x[..., None], x.shape + (B,))
    w_t = lanes(jnp.concatenate([cw, cb], axis=0))
    prm_t = lanes(prm[:, :SSD_HEADS])
    nrm_t = lanes(nrm.reshape(SSD_W))
    carry = (n_layers, j) if j == 0 else None
    hd = SSD_HEADDIM
    n_xs = SSD_W // hd
    grp = lambda h: h // SSD_REP
    win = lambda rows, f: pl.BlockSpec((rows, hd, B), lambda h: (0, f(h), 0))
    if carry:
        st_spec = pl.BlockSpec((n_layers, 1, SSD_STATE, hd, B), lambda h: (0, h, 0, 0, 0))
    else:
        st_spec = pl.BlockSpec((None, 1, SSD_STATE, hd, B), lambda h: (j, h, 0, 0, 0))
    y_t, st_new = pl.pallas_call(
        functools.partial(_ssd_sample_kernel, ls=ls, carry=carry),
        grid=(SSD_HEADS,),
        in_specs=[win(ls, lambda h: h), win(ls, lambda h: n_xs + grp(h)), win(ls, lambda h: n_xs + SSD_GROUPS + grp(h)),
                  win(ls, lambda h: h),
                  pl.BlockSpec((ls, SSD_HEADS, B), lambda h: (0, 0, 0)),
                  win(CONV_W - 1, lambda h: h), win(CONV_W - 1, lambda h: n_xs + grp(h)),
                  win(CONV_W - 1, lambda h: n_xs + SSD_GROUPS + grp(h)),
                  st_spec,
                  win(CONV_W + 1, lambda h: h), win(CONV_W + 1, lambda h: n_xs + grp(h)),
                  win(CONV_W + 1, lambda h: n_xs + SSD_GROUPS + grp(h)),
                  pl.BlockSpec((3, SSD_HEADS, B), lambda h: (0, 0, 0)),
                  pl.BlockSpec((SSD_GW, B), lambda h: (grp(h), 0))],
        out_specs=[pl.BlockSpec((ls, SSD_GW, B), lambda h: (0, grp(h), 0)), st_spec],
        out_shape=[jax.ShapeDtypeStruct((ls, SSD_W, B), F32), jax.ShapeDtypeStruct(st_t.shape, F32)],
        scratch_shapes=[pltpu.VMEM((ls, SSD_GW, B), F32)],
        input_output_aliases={} if carry else {8: 1},
        compiler_params=_params(dimension_semantics=("arbitrary",)), name="ssd_mixer",
    )(xbc_t, xbc_t, xbc_t, z_t, dt_t, cst_t, cst_t, cst_t, st_t, w_t, w_t, w_t, prm_t, nrm_t)
    y = jnp.transpose(y_t, (2, 0, 1)).astype(BF16)
    y = jnp.pad(y, ((0, 0), (0, um.shape[1] - ls), (0, 0)))
    return y, jnp.transpose(st_new, (0, 4, 1, 2, 3))


def _lru_kernel(z_ref, x_ref, cst_ref, st_ref, cw_ref, cb_ref, wbd_ref, prm_ref, o_ref, so_ref, halo_ref, carry_ref,
                *, bb, tb, lvalid, ltotal):
    t = pl.program_id(1)

    @pl.when(t == 0)
    def _():
        carry_ref[...] = st_ref[...]
        halo_ref[:, 0:5, :] = jnp.zeros((bb, 5, LRU_W), F32)
        halo_ref[:, 5:8, :] = cst_ref[...]

    cw = cw_ref[...]
    cb = cb_ref[...]
    b_a = prm_ref[0:1, :]
    b_x = prm_ref[1:2, :]
    sp = jax.nn.softplus(-prm_ref[2:3, :])
    row = _iota2((tb, 1), 0)
    last_t, last_r = (lvalid - 1) // tb, (lvalid - 1) % tb

    def per_batch(bi, carry):
        x = x_ref[bi]
        xc = _conv_rows(halo_ref[bi], x, cw, cb)
        halo_ref[bi] = x[tb - SUBLANES:tb, :]
        xb = xc.astype(BF16)
        rs, is_ = [], []
        for cg in range(LRU_W // LANES):
            ri = _mm(xb[:, cg * LANES:(cg + 1) * LANES], wbd_ref[cg])
            rs.append(ri[:, :LANES])
            is_.append(ri[:, LANES:])
        r = jax.nn.sigmoid(jnp.concatenate(rs, axis=1) + b_a)
        i = jax.nn.sigmoid(jnp.concatenate(is_, axis=1) + b_x)
        log_a = (-LRU_C) * r * sp
        a = jnp.exp(log_a)
        om = 1.0 - jnp.exp(2.0 * log_a)
        b = jnp.where(om > 0.0, om * lax.rsqrt(om), 0.0) * (i * xc)
        d = 1
        while d < tb:
            m = row >= d
            a_s = pltpu.roll(a, d, axis=0)
            b_s = pltpu.roll(b, d, axis=0)
            b = jnp.where(m, a * b_s + b, b)
            a = jnp.where(m, a * a_s, a)
            d *= 2
        hs = a * carry_ref[bi] + b
        carry_ref[bi] = hs[tb - 1:tb, :]

        @pl.when(t == last_t)
        def _():
            so_ref[bi] = hs[last_r:last_r + 1, :]

        o_ref[bi] = (hs * _silu(z_ref[bi])).astype(o_ref.dtype)
        return carry

    lax.fori_loop(0, bb, per_batch, 0)


def _lru_mixer(um, cst_all, st_all, j, cw, cb, wbd, prm, *, bb, tb, lvalid):
    B, L, _ = um.shape
    kern = functools.partial(_lru_kernel, bb=bb, tb=tb, lvalid=lvalid, ltotal=L)
    const = lambda b, t: (0, 0)
    return pl.pallas_call(
        kern,
        grid=(B // bb, L // tb),
        in_specs=[pl.BlockSpec((bb, tb, LRU_W), lambda b, t: (b, t, CD_ZLRU // LRU_W)),
                  pl.BlockSpec((bb, tb, LRU_W), lambda b, t: (b, t, CD_XLRU // LRU_W)),
                  pl.BlockSpec((None, bb, CONV_W - 1, LRU_W), lambda b, t: (j, b, 0, 0)),
                  pl.BlockSpec((None, bb, 1, LRU_W), lambda b, t: (j, b, 0, 0)),
                  pl.BlockSpec((CONV_W, LRU_W), const),
                  pl.BlockSpec((1, LRU_W), const),
                  pl.BlockSpec((LRU_W // LANES, LANES, 2 * LANES), lambda b, t: (0, 0, 0)),
                  pl.BlockSpec((3, LRU_W), const)],
        out_specs=[pl.BlockSpec((bb, tb, LRU_W), lambda b, t: (b, t, 0)),
                   pl.BlockSpec((bb, 1, LRU_W), lambda b, t: (b, 0, 0))],
        out_shape=[jax.ShapeDtypeStruct((B, L, LRU_W), BF16),
                   jax.ShapeDtypeStruct((B, 1, LRU_W), F32)],
        scratch_shapes=[pltpu.VMEM((bb, SUBLANES, LRU_W), F32), pltpu.VMEM((bb, 1, LRU_W), F32)],
        compiler_params=_params(dimension_semantics=("parallel", "arbitrary")),
        name="lru_mixer",
    )(um, um, cst_all, st_all, cw, cb, wbd, prm)


def _lane_row(*pieces):
    v = jnp.concatenate([p.reshape(-1) for p in pieces])
    return jnp.pad(v, (0, LANES - v.shape[0])).reshape(1, LANES)


RP_TILE = 512
RP_TAIL = 32


def _repack_kernel(a_ref, b_ref, o_ref, *, plan):
    r = pl.program_id(1)
    lo = 0
    for n_tiles, _, shift in plan:
        @pl.when((r >= lo) & (r < lo + n_tiles))
        def _(shift=shift):
            if shift == 0:
                o_ref[...] = a_ref[...].astype(BF16)
            else:
                o_ref[0:RP_TILE - shift, :] = a_ref[shift:, :].astype(BF16)
                o_ref[RP_TILE - shift:, :] = b_ref[0:shift, :].astype(BF16)
        lo += n_tiles


def _repack_wt(wt, plan):
    n, _, D = wt.shape
    total = sum(p[0] for p in plan)

    def src_tile(r):
        lo, idx = 0, 0
        for n_tiles, start, _ in plan:
            idx = jnp.where((r >= lo) & (r < lo + n_tiles), start // RP_TILE + (r - lo), idx)
            lo += n_tiles
        return idx

    for _, start, shift in plan:
        assert start % RP_TILE == 0 and shift % 16 == 0 and shift <= RP_TAIL
    return pl.pallas_call(
        functools.partial(_repack_kernel, plan=plan),
        grid=(n, total),
        in_specs=[pl.BlockSpec((None, RP_TILE, D), lambda l, r: (l, src_tile(r), 0)),
                  pl.BlockSpec((None, RP_TAIL, D), lambda l, r: (l, (src_tile(r) + 1) * (RP_TILE // RP_TAIL), 0))],
        out_specs=pl.BlockSpec((None, RP_TILE, D), lambda l, r: (l, r, 0)),
        out_shape=jax.ShapeDtypeStruct((n, total * RP_TILE, D), BF16),
        compiler_params=_params(dimension_semantics=("parallel", "parallel")),
        name="repack_w",
    )(wt, wt)


def _rows_t(w, sizes, order, pad_to=None):
    wt = jnp.swapaxes(w, 1, 2)
    offs = [sum(sizes[:i]) for i in range(len(sizes))]
    out = jnp.concatenate([wt[:, offs[i]:offs[i] + sizes[i], :] for i in order], axis=1)
    if pad_to is not None:
        out = jnp.pad(out, ((0, 0), (0, pad_to - out.shape[1]), (0, 0)))
    return out


def _prep_weights(w):
    n_even = w['ab_w_in'].shape[0]
    n_odd = w['cd_w_in'].shape[0]
    D = w['ab_w_in'].shape[1]
    P = {'ab': [], 'cd': []}
    P['ab_w_main'] = _repack_wt(jnp.swapaxes(w['ab_w_in'], 1, 2), [(6, 0, 0), (6, 3072, 16), (2, 6144, 32)])
    P['ab_w_small'] = _rows_t(w['ab_w_in'], AB_SIZES, (1, 2, 7), LANES)
    P['ab_w_out'] = w['ab_w_out'].astype(BF16)
    for j in range(n_even):
        P['ab'].append(dict(
            dn_cw=w['dn_conv_w'][j],
            dn_prm=jnp.concatenate([_lane_row(jnp.zeros((DN_HEADS,), F32), w['dn_a_log'][j]),
                                    _lane_row(jnp.zeros((DN_HEADS,), F32), w['dn_dt_bias'][j])], axis=0),
            dn_norm=w['dn_norm'][j].reshape(1, DN_DV),
            gla_wa=jnp.pad(w['gla_wa2'][j], ((2 * DN_HEADS, LANES - 2 * DN_HEADS - GLA_RANK), (0, 0))).astype(BF16),
            gla_ba=w['gla_ba'][j].reshape(1, GLA_KW),
            gla_norm=w['gla_norm'][j].reshape(1, GLA_DV),
        ))
    P['cd_w_main'] = _repack_wt(jnp.swapaxes(w['cd_w_in'], 1, 2), [(2, 0, 0), (4, 2560, 16), (3, 1024, 0)])
    P['cd_w_small'] = _rows_t(w['cd_w_in'], CD_SIZES, (2,), LANES)
    P['cd_w_out'] = w['cd_w_out'].astype(BF16)
    eye2 = jnp.eye(2, dtype=F32)
    for j in range(n_odd):
        wa = w['lru_wa'][j].reshape(LRU_W // LANES, 2, LRU_BS, LRU_BS)
        wx = w['lru_wx'][j].reshape(LRU_W // LANES, 2, LRU_BS, LRU_BS)
        bd = lambda m: jnp.einsum('gbcd,be->gbced', m, eye2).reshape(LRU_W // LANES, LANES, LANES)
        P['cd'].append(dict(
            ssd_cw=w['ssd_conv_w'][j],
            ssd_cb=w['ssd_conv_b'][j].reshape(1, SSD_XBC),
            ssd_prm=jnp.concatenate([_lane_row(w['ssd_a_log'][j]), _lane_row(w['ssd_dt_bias'][j]),
                                     _lane_row(w['ssd_d'][j])], axis=0),
            ssd_norm=w['ssd_norm'][j].reshape(1, SSD_W),
            lru_cw=w['lru_conv_w'][j],
            lru_cb=w['lru_conv_b'][j].reshape(1, LRU_W),
            lru_wbd=jnp.concatenate([bd(wa), bd(wx)], axis=2).astype(BF16),
            lru_prm=jnp.stack([w['lru_ba'][j], w['lru_bx'][j], w['lru_lambda'][j]], axis=0),
        ))
    P['norm_g'] = w['norm_g'].reshape(-1, 1, D)
    P['ple_norm'] = w['ple_norm'].reshape(-1, 1, D)
    P['ple_w_gate'] = w['ple_w_gate'].astype(BF16)
    P['ple_w_proj'] = w['ple_w_proj'].astype(BF16)
    P['final_norm'] = w['final_norm'].reshape(1, D)
    return P


def _trunk(x, p, states, P, *, lvalid, bb, tb, c_dn, c_gla, c_ssd):
    B, L, D = x.shape
    depth = P['norm_g'].shape[0]
    st_dn, st_dnc, st_gla, st_ssd, st_ssdc, st_lru, st_lruc = states
    st_lru = st_lru.reshape(st_lru.shape[0], B, 1, LRU_W)
    dnc_new, ssdc_new, lru_new, lruc_new = [], [], [], []
    T = B * L
    h = x.reshape(T, D)
    p_all = p.reshape(depth, T, PLE_DIM)
    lo = lvalid - (CONV_W - 1)
    for li in range(depth):
        j = li // 2
        if li % 2 == 0:
            W = P['ab'][j]
            um, us = _inproj(h, P['norm_g'], li, P['ab_w_main'], P['ab_w_small'], j)
            um = um.reshape(B, L, AB_MAIN)
            us = us.reshape(B, L, LANES)
            oa, st_dn = _dn_mixer(um, us, st_dnc, st_dn, j, W['dn_cw'], W['dn_prm'], W['dn_norm'],
                                  bb=bb, tb=tb, C=c_dn, lvalid=lvalid)
            ob, st_gla = _gla_mixer(um, us, st_gla, j, W['gla_wa'], W['gla_ba'], W['gla_norm'],
                                    bb=bb, tb=tb, C=c_gla, lvalid=lvalid)
            dnc_new.append(um[:, lo:lvalid, AB_QKV:AB_QKV + DN_QKV])
        else:
            W = P['cd'][j]
            um, us = _inproj(h, P['norm_g'], li, P['cd_w_main'], P['cd_w_small'], j)
            um = um.reshape(B, L, CD_MAIN)
            us = us.reshape(B, L, LANES)
            oa, st_ssd = _ssd_mixer(um, us, st_ssdc, st_ssd, j, W['ssd_cw'], W['ssd_cb'], W['ssd_prm'],
                                    W['ssd_norm'], bb=bb, tb=tb, lvalid=lvalid)
            ob, lru_j = _lru_mixer(um, st_lruc, st_lru, j, W['lru_cw'], W['lru_cb'],
                                   W['lru_wbd'], W['lru_prm'], bb=bb, tb=tb, lvalid=lvalid)
            ssdc_new.append(um[:, lo:lvalid, CD_XBC:CD_XBC + SSD_XBC])
            lru_new.append(lru_j.reshape(B, LRU_W))
            lruc_new.append(um[:, lo:lvalid, CD_XLRU:CD_XLRU + LRU_W])
        h = _outproj(oa.reshape(T, -1), ob.reshape(T, -1), h, p_all, li, P['ab_w_out' if li % 2 == 0 else 'cd_w_out'], j,
                     P['ple_norm'], P['ple_w_gate'], P['ple_w_proj'], P['final_norm'], final=(li == depth - 1))
    new_states = [st_dn, jnp.stack(dnc_new), st_gla, st_ssd, jnp.stack(ssdc_new), jnp.stack(lru_new),
                  jnp.stack(lruc_new)]
    return h.reshape(B, L, D), new_states


def _pick_tb(L):
    tb = min(L, 512)
    assert L % tb == 0
    return tb


def kernel(x_prompt, x_sample, state_dn, state_dn_conv, state_gla, state_ssd, state_ssd_conv, state_lru, state_lru_conv, p_prompt, p_sample, norm_g, final_norm, ab_w_in, dn_conv_w, dn_a_log, dn_dt_bias, dn_norm, gla_wa2, gla_ba, gla_norm, ab_w_out, cd_w_in, ssd_conv_w, ssd_conv_b, ssd_a_log, ssd_dt_bias, ssd_d, ssd_norm, lru_conv_w, lru_conv_b, lru_wa, lru_ba, lru_wx, lru_bx, lru_lambda, cd_w_out, ple_w_proj, ple_norm, ple_w_gate):
    w = {
        'norm_g': norm_g, 'final_norm': final_norm,
        'ab_w_in': ab_w_in, 'dn_conv_w': dn_conv_w, 'dn_a_log': dn_a_log, 'dn_dt_bias': dn_dt_bias,
        'dn_norm': dn_norm, 'gla_wa2': gla_wa2, 'gla_ba': gla_ba, 'gla_norm': gla_norm, 'ab_w_out': ab_w_out,
        'cd_w_in': cd_w_in, 'ssd_conv_w': ssd_conv_w, 'ssd_conv_b': ssd_conv_b, 'ssd_a_log': ssd_a_log,
        'ssd_dt_bias': ssd_dt_bias, 'ssd_d': ssd_d, 'ssd_norm': ssd_norm,
        'lru_conv_w': lru_conv_w, 'lru_conv_b': lru_conv_b, 'lru_wa': lru_wa, 'lru_ba': lru_ba,
        'lru_wx': lru_wx, 'lru_bx': lru_bx, 'lru_lambda': lru_lambda, 'cd_w_out': cd_w_out,
        'ple_w_proj': ple_w_proj, 'ple_norm': ple_norm, 'ple_w_gate': ple_w_gate,
    }
    P = _prep_weights(w)
    n_even, n_odd = ab_w_in.shape[0], cd_w_in.shape[0]

    Bp, Lp, _ = x_prompt.shape
    assert Lp >= CONV_W - 1 and Lp % 64 == 0
    zero_states = (
        jnp.zeros((n_even, Bp, DN_HEADS, DN_DK, DN_DV), F32),
        jnp.zeros((n_even, Bp, CONV_W - 1, DN_QKV), F32),
        jnp.zeros((n_even, Bp, GLA_HEADS, GLA_DK, GLA_DV), F32),
        jnp.zeros((n_odd, Bp, SSD_HEADS, SSD_STATE, SSD_HEADDIM), F32),
        jnp.zeros((n_odd, Bp, CONV_W - 1, SSD_XBC), F32),
        jnp.zeros((n_odd, Bp, LRU_W), F32),
        jnp.zeros((n_odd, Bp, CONV_W - 1, LRU_W), F32),
    )
    y_p, st_p = _trunk(x_prompt, p_prompt, zero_states, P, lvalid=Lp, bb=1, tb=_pick_tb(Lp),
                       c_dn=64, c_gla=16, c_ssd=64)

    Bs, Ls, _ = x_sample.shape
    assert CONV_W - 1 <= Ls <= SUBLANES
    pad = SUBLANES - Ls
    xs = jnp.pad(x_sample, ((0, 0), (0, pad), (0, 0)))
    ps = jnp.pad(p_sample, ((0, 0), (0, 0), (0, pad), (0, 0)))
    sample_states = (state_dn, state_dn_conv, state_gla, state_ssd, state_ssd_conv, state_lru, state_lru_conv)
    bb_s = 8 if Bs % 8 == 0 else 1
    y_s, st_s = _trunk(xs, ps, sample_states, P, lvalid=Ls, bb=bb_s, tb=SUBLANES,
                       c_dn=SUBLANES, c_gla=SUBLANES, c_ssd=SUBLANES)
    return (y_p, y_s[:, :Ls], *st_p, *st_s)
```

```python
import functools

import jax
import jax.numpy as jnp
from jax import lax
from jax.experimental import pallas as pl
from jax.experimental.pallas import tpu as pltpu

F32 = jnp.float32
BF16 = jnp.bfloat16
HI = lax.Precision.HIGHEST

EPS = 1e-6
LOG2E = 1.4426950408889634
CONV_W = 4
PLE_DIM = 256
DN_HEADS, DN_DK, DN_DV = 8, 128, 128
DN_W = DN_HEADS * DN_DV
DN_QKV = 2 * DN_HEADS * DN_DK + DN_W
GLA_HEADS, GLA_DK, GLA_DV = 4, 128, 256
GLA_KW = GLA_HEADS * GLA_DK
GLA_W = GLA_HEADS * GLA_DV
GLA_RANK = 16
GLA_GATE_NORM = 16.0
AB_SIZES = (DN_QKV, DN_HEADS, DN_HEADS, DN_W, GLA_KW, GLA_KW, GLA_W, GLA_RANK, GLA_W)
SSD_HEADS, SSD_HEADDIM, SSD_GROUPS, SSD_STATE = 16, 64, 4, 64
SSD_W = SSD_HEADS * SSD_HEADDIM
SSD_XBC = SSD_W + 2 * SSD_GROUPS * SSD_STATE
LRU_W, LRU_BLOCKS = 1024, 16
LRU_BS = LRU_W // LRU_BLOCKS
LRU_C = 8.0
CD_SIZES = (SSD_W, SSD_XBC, SSD_HEADS, LRU_W, LRU_W)

LANES = 128
SUBLANES = 8
VMEM_LIMIT = 52 * 1024 * 1024

AB_QKV, AB_ZDN, AB_GQ, AB_GK, AB_GV, AB_ZGLA = 0, 3072, 4096, 4608, 5120, 6144
AB_MAIN = 7168
CD_Z, CD_ZLRU, CD_XLRU, CD_XBC = 0, 1024, 2048, 3072
CD_MAIN = 4608


def _silu(x):
    return x * jax.nn.sigmoid(x)


def _nt(a, b):
    return lax.dot_general(a, b, (((1,), (1,)), ((), ())), preferred_element_type=F32)


def _tn(a, b):
    return lax.dot_general(a, b, (((0,), (0,)), ((), ())), preferred_element_type=F32)


def _mm(a, b):
    return jnp.dot(a, b, preferred_element_type=F32)


def _mm_hi(a, b):
    return jnp.dot(a, b, preferred_element_type=F32, precision=HI)


def _iota2(shape, dim):
    return lax.broadcasted_iota(jnp.int32, shape, dim)


def _conv_rows(halo, x, w, bias=None):
    cat = jnp.concatenate([halo, x], axis=0)
    out = x * w[3:4]
    for s in (1, 2, 3):
        out = out + pltpu.roll(cat, s, axis=0)[SUBLANES:] * w[3 - s:4 - s]
    if bias is not None:
        out = out + bias
    return out


def _params(**kw):
    return pltpu.CompilerParams(vmem_limit_bytes=VMEM_LIMIT, **kw)


def _own_layer(st_ref, so_ref, carry):
    if carry is None:
        return st_ref, so_ref
    n, jl = carry
    for l in range(n):
        if l != jl:
            so_ref[l] = st_ref[l]
    return st_ref.at[jl], so_ref.at[jl]


def _state_specs(st_all, j, bb, carry):
    tail = st_all.shape[2:]
    zeros = (0,) * len(tail)
    if carry:
        return pl.BlockSpec((st_all.shape[0], bb) + tail, lambda b, *_: (0, b) + zeros)
    return pl.BlockSpec((None, bb) + tail, lambda b, *_: (j, b) + zeros)


def _inproj_kernel(h_ref, g_ref, wm_ref, ws_ref, um_ref, us_ref, xn_ref):
    @pl.when(pl.program_id(1) == 0)
    def _():
        x = h_ref[...]
        r = lax.rsqrt(jnp.mean(x * x, axis=-1, keepdims=True) + EPS)
        xn = (x * r * g_ref[...]).astype(BF16)
        xn_ref[...] = xn
        us_ref[...] = _nt(xn, ws_ref[...].astype(BF16))

    um_ref[...] = _nt(xn_ref[...], wm_ref[...])


def _inproj(h, g_all, li, wm_all, ws_all, j):
    T, D = h.shape
    Nm = wm_all.shape[1]
    tm = min(T, 1024)
    tn = 1024 if Nm % 1024 == 0 else 1536
    return pl.pallas_call(
        _inproj_kernel,
        grid=(T // tm, Nm // tn),
        in_specs=[pl.BlockSpec((tm, D), lambda i, c: (i, 0)),
                  pl.BlockSpec((None, 1, D), lambda i, c: (li, 0, 0)),
                  pl.BlockSpec((None, tn, D), lambda i, c: (j, c, 0)),
                  pl.BlockSpec((None, LANES, D), lambda i, c: (j, 0, 0))],
        out_specs=[pl.BlockSpec((tm, tn), lambda i, c: (i, c)),
                   pl.BlockSpec((tm, LANES), lambda i, c: (i, 0))],
        out_shape=[jax.ShapeDtypeStruct((T, Nm), F32), jax.ShapeDtypeStruct((T, LANES), F32)],
        scratch_shapes=[pltpu.VMEM((tm, D), BF16)],
        compiler_params=_params(dimension_semantics=("parallel", "arbitrary")),
        name="inproj",
    )(h, g_all, wm_all, ws_all)


def _outproj_kernel(oa_ref, ob_ref, h_ref, p_ref, wa_ref, wb_ref, pg_ref, wg_ref, wp_ref, fg_ref, out_ref, *, final):
    h1 = h_ref[...] + _mm(oa_ref[...], wa_ref[...]) + _mm(ob_ref[...], wb_ref[...])
    r = lax.rsqrt(jnp.mean(h1 * h1, axis=-1, keepdims=True) + EPS)
    hn = (h1 * r * pg_ref[...]).astype(BF16)
    gate = jax.nn.sigmoid(_mm(hn, wg_ref[...]))
    pp = _mm(p_ref[...].astype(BF16), wp_ref[...])
    h2 = h1 + gate * pp
    if final:
        r2 = lax.rsqrt(jnp.mean(h2 * h2, axis=-1, keepdims=True) + EPS)
        h2 = h2 * r2 * fg_ref[...]
    out_ref[...] = h2


def _outproj(oa, ob, h, p_all, li, w_out_all, j, pg_all, wg_all, wp_all, fg, final):
    T, D = h.shape
    Wa, Wb = oa.shape[1], ob.shape[1]
    assert Wa == Wb
    tm = min(T, 512)
    resident = lambda shape, idx: pl.BlockSpec(shape, lambda i: idx, pipeline_mode=pl.Buffered(1))
    return pl.pallas_call(
        functools.partial(_outproj_kernel, final=final),
        grid=(T // tm,),
        in_specs=[pl.BlockSpec((tm, Wa), lambda i: (i, 0)),
                  pl.BlockSpec((tm, Wb), lambda i: (i, 0)),
                  pl.BlockSpec((tm, D), lambda i: (i, 0)),
                  pl.BlockSpec((None, tm, PLE_DIM), lambda i: (li, i, 0)),
                  resident((None, Wa, D), (j, 0, 0)),
                  resident((None, Wb, D), (j, 1, 0)),
                  resident((None, 1, D), (li, 0, 0)),
                  resident((None, D, D), (li, 0, 0)),
                  resident((None, PLE_DIM, D), (li, 0, 0)),
                  resident((1, D), (0, 0))],
        out_specs=pl.BlockSpec((tm, D), lambda i: (i, 0)),
        out_shape=jax.ShapeDtypeStruct((T, D), F32),
        compiler_params=_params(dimension_semantics=("parallel",)),
        name="outproj",
    )(oa, ob, h, p_all, w_out_all, w_out_all, pg_all, wg_all, wp_all, fg)


R_BD = 128


def _bd_masks(C):
    ii = _iota2((R_BD, R_BD), 0)
    jj = _iota2((R_BD, R_BD), 1)
    sh = C.bit_length() - 1
    lower = ((ii >> sh) == (jj >> sh)) & (jj <= ii)
    neg = jnp.where(lower, 0.0, -1e30)
    eye = (ii == jj).astype(F32)
    return neg, 1.0 - eye, eye


def _dn_groups(Q, K, V, beta, gc, gr, neg, offdiag, eye, C):
    n = range(len(Q))
    Kb = [K[g].astype(BF16) for g in n]
    KK = [_nt(Kb[g], Kb[g]) for g in n]
    QK = [_nt(Q[g].astype(BF16), Kb[g]) for g in n]
    dm = [jnp.exp((gc[g] - gr[g]) + neg) for g in n]
    A = [(beta[g] * KK[g]) * (dm[g] * offdiag) for g in n]
    T = [eye - A[g] for g in n]
    X = A
    p = 2
    while p < C:
        Xb = [X[g].astype(BF16) for g in n]
        X = [_mm(Xb[g], Xb[g]) for g in n]
        T = [T[g] + _mm(T[g].astype(BF16), X[g].astype(BF16)) for g in n]
        p *= 2
    eg = [jnp.exp(gc[g]) for g in n]
    rhs = [jnp.concatenate([beta[g] * V[g], (beta[g] * eg[g]) * K[g]], axis=1).astype(BF16) for g in n]
    sol = [_mm(T[g].astype(BF16), rhs[g]) for g in n]
    return ([sol[g][:, :DN_DV] for g in n], [sol[g][:, DN_DV:] for g in n],
            [QK[g] * dm[g] for g in n], [Q[g] * eg[g] for g in n])


def _l2n(x):
    return x * lax.rsqrt(jnp.sum(x * x, axis=-1, keepdims=True) + EPS)


def _dn_apply(S, u0, wk, qkd, qg, K, gl, gc, C):
    n = range(len(S))
    nu = len(S[0])
    blk = lambda x, i: x[i * C:(i + 1) * C]
    r = [[None] * nu for _ in n]
    for i in range(nu):
        for g in n:
            lhs = jnp.concatenate([blk(wk[g], i), blk(qg[g], i)], axis=0).astype(BF16)
            r[g][i] = _mm(lhs, S[g][i].astype(BF16))
    Ub = [jnp.concatenate([blk(u0[g], i) - r[g][i][:C] for i in range(nu)], axis=0).astype(BF16) for g in n]
    O = [jnp.concatenate([r[g][i][C:] for i in range(nu)], axis=0) + _mm(qkd[g].astype(BF16), Ub[g]) for g in n]
    kend = [(K[g] * jnp.exp(gl[g] - gc[g])).astype(BF16) for g in n]
    egl = [jnp.exp(gl[g]) for g in n]
    S_new = [[None] * nu for _ in n]
    for i in range(nu):
        for g in n:
            S_new[g][i] = egl[g][i * C:i * C + 1, :] * S[g][i] + _tn(blk(kend[g], i), blk(Ub[g], i))
    return O, S_new


def _dn_prompt_kernel(qkv_ref, z_ref, us_ref, cst_ref, st_ref, cw_ref, prm_ref, nrm_ref, o_ref, so_ref, halo_ref,
                      *, tb, C, nc):
    t = pl.program_id(1)
    hg = R_BD // C
    CR = nc * C

    @pl.when(t == 0)
    def _():
        so_ref[...] = st_ref[...]
        halo_ref[0:5, :] = jnp.zeros((5, DN_QKV), F32)
        halo_ref[5:8, :] = cst_ref[...]

    neg, offdiag, eye = _bd_masks(C)
    ii = _iota2((CR, CR), 0)
    jj = _iota2((CR, CR), 1)
    sh = C.bit_length() - 1
    tril = (((ii >> sh) == (jj >> sh)) & (jj <= ii)).astype(F32)
    lane = _iota2((1, LANES), 1)
    gmask = (lane >= DN_HEADS) & (lane < 2 * DN_HEADS)
    neg_a = -jnp.exp(prm_ref[0:1, :])
    bias = prm_ref[1:2, :]
    cw = cw_ref[...]
    nrm = nrm_ref[...]
    head_sets = [range(gi * hg, (gi + 1) * hg) for gi in range(DN_HEADS // hg)]
    groups = [(ci, hs) for ci in range(nc) for hs in head_sets]
    rws = lambda ci: slice(ci * C, (ci + 1) * C)

    def step(cp, carry):
        r0 = pl.multiple_of(cp * CR, CR)
        x = qkv_ref[pl.ds(r0, CR), :]
        prev = qkv_ref[pl.ds(pl.multiple_of(jnp.maximum(r0 - SUBLANES, 0), SUBLANES), SUBLANES), :]
        halo = jnp.where(cp == 0, halo_ref[...], prev)
        act = _silu(_conv_rows(halo, x, cw))
        us = us_ref[pl.ds(r0, CR), :]
        g = jnp.where(gmask, neg_a * jax.nn.softplus(us + bias), 0.0)
        beta_all = jax.nn.sigmoid(us)
        G = _mm_hi(tril, g)
        GT = G.T
        z = z_ref[pl.ds(r0, CR), :]
        cat0 = lambda f: [jnp.concatenate([f(ci, h) for h in hs], axis=0) for ci, hs in groups]
        Q = [_l2n(a) * (DN_DK ** -0.5) for a in cat0(lambda ci, h: act[rws(ci), h * DN_DK:(h + 1) * DN_DK])]
        K = [_l2n(a) for a in cat0(lambda ci, h: act[rws(ci), DN_W + h * DN_DK:DN_W + (h + 1) * DN_DK])]
        V = cat0(lambda ci, h: act[rws(ci), 2 * DN_W + h * DN_DV:2 * DN_W + (h + 1) * DN_DV])
        beta = cat0(lambda ci, h: beta_all[rws(ci), h:h + 1])
        gc = cat0(lambda ci, h: G[rws(ci), DN_HEADS + h:DN_HEADS + h + 1])
        gl = cat0(lambda ci, h: jnp.broadcast_to(
            G[(ci + 1) * C - 1:(ci + 1) * C, DN_HEADS + h:DN_HEADS + h + 1], (C, 1)))
        gr = [jnp.concatenate([GT[DN_HEADS + h:DN_HEADS + h + 1, rws(ci)] for h in hs], axis=1)
              for ci, hs in groups]
        u0, wk, qkd, qg = _dn_groups(Q, K, V, beta, gc, gr, neg, offdiag, eye, C)
        Z = cat0(lambda ci, h: z[rws(ci), h * DN_DV:(h + 1) * DN_DV])
        S = [[so_ref[h] for h in hs] for hs in head_sets]
        ng = len(head_sets)
        for ci in range(nc):
            pick = lambda xs: xs[ci * ng:(ci + 1) * ng]
            O, S = _dn_apply(S, pick(u0), pick(wk), pick(qkd), pick(qg), pick(K), pick(gl), pick(gc), C)
            for gi, hs in enumerate(head_sets):
                o = O[gi] * lax.rsqrt(jnp.mean(O[gi] * O[gi], axis=-1, keepdims=True) + EPS) * nrm
                out = (o * _silu(Z[ci * ng + gi])).astype(o_ref.dtype)
                for i, h in enumerate(hs):
                    o_ref[pl.ds(r0 + ci * C, C), h * DN_DV:(h + 1) * DN_DV] = out[i * C:(i + 1) * C]
        for gi, hs in enumerate(head_sets):
            for i, h in enumerate(hs):
                so_ref[h] = S[gi][i]
        return carry

    lax.fori_loop(0, tb // CR, step, 0)
    halo_ref[...] = qkv_ref[tb - SUBLANES:tb, :]


def _dn_sample_kernel(qkv_ref, z_ref, us_ref, cst_ref, st_ref, cw_ref, prm_ref, nrm_ref, o_ref, so_ref, halo_ref,
                      *, bb, lvalid, carry):
    st_ref, so_ref = _own_layer(st_ref, so_ref, carry)
    C = SUBLANES
    nb = R_BD // (DN_HEADS * C)
    rows_all = bb * C

    halo_ref[:, 0:5, :] = jnp.zeros((bb, 5, DN_QKV), F32)
    halo_ref[:, 5:8, :] = cst_ref[...]
    cw = cw_ref[...]
    nrm = nrm_ref[...]
    acts = [_silu(_conv_rows(halo_ref[b], qkv_ref[b], cw)) for b in range(bb)]

    neg, offdiag, eye = _bd_masks(C)
    lane = _iota2((1, LANES), 1)
    gmask = (lane >= DN_HEADS) & (lane < 2 * DN_HEADS)
    neg_a = -jnp.exp(prm_ref[0:1, :])
    bias = prm_ref[1:2, :]
    us = us_ref[...].reshape(rows_all, LANES)
    valid = (_iota2((rows_all, 1), 0) & (C - 1)) < lvalid
    g = jnp.where(gmask & valid, neg_a * jax.nn.softplus(us + bias), 0.0)
    beta_all = jnp.where(valid, jax.nn.sigmoid(us), 0.0)
    ii = _iota2((rows_all, rows_all), 0)
    jj = _iota2((rows_all, rows_all), 1)
    same = (ii >> 3) == (jj >> 3)
    G_all = _mm_hi((same & (jj <= ii)).astype(F32), g)
    GL_all = _mm_hi(same.astype(F32), g)
    GT_all = G_all.T

    groups = [[(h, gi * nb + bl) for h in range(DN_HEADS) for bl in range(nb)]
              for gi in range(bb // nb)]
    rows = [slice(gi * nb * C, (gi + 1) * nb * C) for gi in range(bb // nb)]
    cat0 = lambda f: [jnp.concatenate([f(h, b) for h, b in us_], axis=0) for us_ in groups]
    cath = lambda f: [jnp.concatenate([f(h, rs) for h in range(DN_HEADS)], axis=0) for rs in rows]
    Q = [_l2n(x) * (DN_DK ** -0.5) for x in cat0(lambda h, b: acts[b][:, h * DN_DK:(h + 1) * DN_DK])]
    K = [_l2n(x) for x in cat0(lambda h, b: acts[b][:, DN_W + h * DN_DK:DN_W + (h + 1) * DN_DK])]
    V = cat0(lambda h, b: acts[b][:, 2 * DN_W + h * DN_DV:2 * DN_W + (h + 1) * DN_DV])
    beta = cath(lambda h, rs: beta_all[rs, h:h + 1])
    gc = cath(lambda h, rs: G_all[rs, DN_HEADS + h:DN_HEADS + h + 1])
    gl = cath(lambda h, rs: GL_all[rs, DN_HEADS + h:DN_HEADS + h + 1])
    gr = [jnp.concatenate([GT_all[DN_HEADS + h:DN_HEADS + h + 1, rs] for h in range(DN_HEADS)], axis=1)
          for rs in rows]
    u0, wk, qkd, qg = _dn_groups(Q, K, V, beta, gc, gr, neg, offdiag, eye, C)
    S = [[st_ref[b, h] for h, b in us_] for us_ in groups]
    O, S_new = _dn_apply(S, u0, wk, qkd, qg, K, gl, gc, C)
    Z = cat0(lambda h, b: z_ref[b][:, h * DN_DV:(h + 1) * DN_DV])
    for g, us_ in enumerate(groups):
        o = O[g] * lax.rsqrt(jnp.mean(O[g] * O[g], axis=-1, keepdims=True) + EPS) * nrm
        out = (o * _silu(Z[g])).astype(o_ref.dtype)
        for i, (h, b) in enumerate(us_):
            so_ref[b, h] = S_new[g][i]
            o_ref[b, :, h * DN_DV:(h + 1) * DN_DV] = out[i * C:(i + 1) * C]


def _dn_mixer(um, us, cst_all, st_all, j, cw, prm, nrm, *, bb, tb, C, lvalid):
    B, L, _ = um.shape
    const = lambda *_: (0, 0)
    out_shape = [jax.ShapeDtypeStruct((B, L, DN_W), BF16), jax.ShapeDtypeStruct(st_all.shape, F32)]
    carry = None
    if lvalid == L:
        assert bb == 1
        kern = functools.partial(_dn_prompt_kernel, tb=tb, C=C, nc=4)
        grid = (B, L // tb)
        in_specs = [pl.BlockSpec((None, tb, DN_QKV), lambda b, t: (b, t, AB_QKV // DN_QKV)),
                    pl.BlockSpec((None, tb, DN_W), lambda b, t: (b, t, AB_ZDN // DN_W)),
                    pl.BlockSpec((None, tb, LANES), lambda b, t: (b, t, 0)),
                    pl.BlockSpec((None, None, CONV_W - 1, DN_QKV), lambda b, t: (j, b, 0, 0)),
                    pl.BlockSpec((None, None, DN_HEADS, DN_DK, DN_DV), lambda b, t: (j, b, 0, 0, 0)),
                    pl.BlockSpec((CONV_W, DN_QKV), const),
                    pl.BlockSpec((2, LANES), const),
                    pl.BlockSpec((1, DN_DV), const)]
        out_specs = [pl.BlockSpec((None, tb, DN_W), lambda b, t: (b, t, 0)),
                     pl.BlockSpec((None, None, DN_HEADS, DN_DK, DN_DV), lambda b, t: (j, b, 0, 0, 0))]
        scratch = [pltpu.VMEM((SUBLANES, DN_QKV), F32)]
        sem = ("parallel", "arbitrary")
    else:
        assert L == SUBLANES and C == SUBLANES
        carry = (st_all.shape[0], j) if j == 0 else None
        kern = functools.partial(_dn_sample_kernel, bb=bb, lvalid=lvalid, carry=carry)
        grid = (B // bb,)
        in_specs = [pl.BlockSpec((bb, L, DN_QKV), lambda b: (b, 0, AB_QKV // DN_QKV)),
                    pl.BlockSpec((bb, L, DN_W), lambda b: (b, 0, AB_ZDN // DN_W)),
                    pl.BlockSpec((bb, L, LANES), lambda b: (b, 0, 0)),
                    pl.BlockSpec((None, bb, CONV_W - 1, DN_QKV), lambda b: (j, b, 0, 0)),
                    _state_specs(st_all, j, bb, carry),
                    pl.BlockSpec((CONV_W, DN_QKV), const),
                    pl.BlockSpec((2, LANES), const),
                    pl.BlockSpec((1, DN_DV), const)]
        out_specs = [pl.BlockSpec((bb, L, DN_W), lambda b: (b, 0, 0)), _state_specs(st_all, j, bb, carry)]
        scratch = [pltpu.VMEM((bb, SUBLANES, DN_QKV), F32)]
        sem = ("parallel",)
    return pl.pallas_call(
        kern, grid=grid, in_specs=in_specs, out_specs=out_specs, out_shape=out_shape,
        scratch_shapes=scratch, input_output_aliases={} if carry else {4: 1},
        compiler_params=_params(dimension_semantics=sem), name="dn_mixer",
    )(um, um, us, cst_all, st_all, cw, prm, nrm)


def _gla_sample_kernel(q_ref, k_ref, v_ref, z_ref, us_ref, st_ref, wa_ref, ba_ref, nrm_ref, o_ref, so_ref,
                       *, bb, lvalid, carry):
    st_ref, so_ref = _own_layer(st_ref, so_ref, carry)
    C = SUBLANES
    R = bb * C
    ii = _iota2((C, C), 0)
    jj = _iota2((C, C), 1)
    nrm = nrm_ref[...]
    valid = (_iota2((R, 1), 0) & (C - 1)) < lvalid
    us = us_ref[...].reshape(R, LANES)
    glog = jax.nn.log_sigmoid(_mm(us.astype(BF16), wa_ref[...]) + ba_ref[...]) * (1.0 / GLA_GATE_NORM)
    glog = jnp.where(valid, glog, 0.0)
    ri = _iota2((R, R), 0)
    ci = _iota2((R, R), 1)
    G = _mm_hi((((ri >> 3) == (ci >> 3)) & (ci <= ri)).astype(F32), glog)
    q = q_ref[...].reshape(R, GLA_KW) * (GLA_DK ** -0.5)
    k = jnp.where(valid, k_ref[...].reshape(R, GLA_KW), 0.0)
    v = jnp.where(valid, v_ref[...].reshape(R, GLA_W), 0.0)
    z = z_ref[...].reshape(R, GLA_W)
    units = [(b, h) for b in range(bb) for h in range(GLA_HEADS)]
    U = range(len(units))
    tile = lambda x, b, h, w: x[b * C:(b + 1) * C, h * w:(h + 1) * w]
    Gu = [tile(G, b, h, GLA_DK) for b, h in units]
    qu = [tile(q, b, h, GLA_DK) for b, h in units]
    ku = [tile(k, b, h, GLA_DK) for b, h in units]
    vu = [tile(v, b, h, GLA_DV).astype(BF16) for b, h in units]
    AT = [jnp.zeros((C, C), F32) for _ in U]
    for i in range(C):
        for u in U:
            e = jnp.exp(jnp.minimum(Gu[u][i:i + 1, :] - Gu[u], 0.0))
            col = jnp.sum(ku[u] * e * qu[u][i:i + 1, :], axis=1, keepdims=True)
            AT[u] = AT[u] + jnp.where(jj == i, col, 0.0)
    AT = [jnp.where(ii <= jj, AT[u], 0.0).astype(BF16) for u in U]
    S = [st_ref[b, h] for b, h in units]
    o = [_mm((qu[u] * jnp.exp(Gu[u])).astype(BF16), S[u].astype(BF16)) + _tn(AT[u], vu[u]) for u in U]
    kend = [(ku[u] * jnp.exp(Gu[u][C - 1:C, :] - Gu[u])).astype(BF16) for u in U]
    gend = [jnp.exp(Gu[u].T[:, C - 1:C]) for u in U]
    for u, (b, h) in enumerate(units):
        so_ref[b, h] = gend[u] * S[u] + _tn(kend[u], vu[u])
    o = [o[u] * lax.rsqrt(jnp.mean(o[u] * o[u], axis=-1, keepdims=True) + EPS) * nrm for u in U]
    o = [o[u] * _silu(tile(z, b, h, GLA_DV)) for u, (b, h) in enumerate(units)]
    rows = [jnp.concatenate(o[b * GLA_HEADS:(b + 1) * GLA_HEADS], axis=1) for b in range(bb)]
    o_ref[...] = jnp.concatenate(rows, axis=0).astype(o_ref.dtype).reshape(bb, C, GLA_W)


def _gla_prompt_kernel(q_ref, k_ref, v_ref, z_ref, us_ref, st_ref, wa_ref, ba_ref, nrm_ref, o_ref, so_ref, g_ref,
                       *, tb, C):
    t = pl.program_id(1)
    H = range(GLA_HEADS)
    nch = tb // C

    @pl.when(t == 0)
    def _():
        so_ref[...] = st_ref[...]

    glog = jax.nn.log_sigmoid(_mm(us_ref[...].astype(BF16), wa_ref[...]) + ba_ref[...]) * (1.0 / GLA_GATE_NORM)
    step_in_chunk = _iota2((tb, 1), 0) & (C - 1)
    d = 1
    while d < C:
        glog = glog + jnp.where(step_in_chunk >= d, pltpu.roll(glog, d, axis=0), 0.0)
        d *= 2
    g_ref[...] = glog * LOG2E

    ii = _iota2((C, C), 0)
    jj = _iota2((C, C), 1)
    nrm = nrm_ref[...]
    hs = lambda x, h, w: x[:, h * w:(h + 1) * w]

    def scores(c):
        rows = pl.ds(pl.multiple_of(c * C, C), C)
        G = g_ref[rows, :]
        q = q_ref[rows, :] * (GLA_DK ** -0.5)
        k = k_ref[rows, :]
        nt = C // SUBLANES
        Gt = [[hs(G, h, GLA_DK)[t * SUBLANES:(t + 1) * SUBLANES] for t in range(nt)] for h in H]
        kt = [[hs(k, h, GLA_DK)[t * SUBLANES:(t + 1) * SUBLANES] for t in range(nt)] for h in H]
        acc = [[jnp.zeros((SUBLANES, C), F32) for _ in range(nt)] for _ in H]
        for i in range(C):
            for t in range(i // SUBLANES + 1):
                for h in H:
                    e = jnp.exp2(jnp.minimum(hs(G, h, GLA_DK)[i:i + 1, :] - Gt[h][t], 0.0))
                    col = jnp.sum(kt[h][t] * e * hs(q, h, GLA_DK)[i:i + 1, :], axis=1, keepdims=True)
                    acc[h][t] = acc[h][t] + jnp.where(jj[:SUBLANES] == i, col, 0.0)
        AT = [jnp.where(ii <= jj, jnp.concatenate(acc[h], axis=0), 0.0).astype(BF16) for h in H]
        gend = [jnp.exp2(hs(G, h, GLA_DK).T[:, C - 1:C]) for h in H]
        return AT, gend

    def finish(o, c):
        rows = pl.ds(pl.multiple_of(c * C, C), C)
        z = z_ref[rows, :]
        o = [o[h] * lax.rsqrt(jnp.mean(o[h] * o[h], axis=-1, keepdims=True) + EPS) * nrm for h in H]
        o = [o[h] * _silu(hs(z, h, GLA_DV)) for h in H]
        o_ref[rows, :] = jnp.concatenate(o, axis=1).astype(o_ref.dtype)

    def body(c, carry):
        AT, gend, o_prev = carry
        nxt = scores(jnp.minimum(c + 1, nch - 1))
        rows = pl.ds(pl.multiple_of(c * C, C), C)
        G = g_ref[rows, :]
        q = q_ref[rows, :] * (GLA_DK ** -0.5)
        k = k_ref[rows, :]
        v = v_ref[rows, :]
        S = [so_ref[h] for h in H]
        vh = [hs(v, h, GLA_DV).astype(BF16) for h in H]
        o = [_mm((hs(q, h, GLA_DK) * jnp.exp2(hs(G, h, GLA_DK))).astype(BF16), S[h].astype(BF16)) + _tn(AT[h], vh[h])
             for h in H]
        kend = [(hs(k, h, GLA_DK) * jnp.exp2(hs(G, h, GLA_DK)[C - 1:C, :] - hs(G, h, GLA_DK))).astype(BF16) for h in H]
        for h in H:
            so_ref[h] = gend[h] * S[h] + _tn(kend[h], vh[h])
        finish(o_prev, jnp.maximum(c - 1, 0))
        return nxt + (o,)

    AT0, gend0 = scores(0)
    _, _, o_last = lax.fori_loop(0, nch, body, (AT0, gend0, [jnp.zeros((C, GLA_DV), F32) for _ in H]))
    finish(o_last, nch - 1)


def _gla_mixer(um, us, st_all, j, wa, ba, nrm, *, bb, tb, C, lvalid):
    B, L, _ = um.shape
    const = lambda *_: (0, 0)
    carry = None
    if lvalid == L:
        assert bb == 1
        kern = functools.partial(_gla_prompt_kernel, tb=tb, C=C)
        sq = None
        scratch = [pltpu.VMEM((tb, GLA_KW), F32)]
        st_spec = pl.BlockSpec((None, None, GLA_HEADS, GLA_DK, GLA_DV), lambda b, t: (j, b, 0, 0, 0))
    else:
        assert L == SUBLANES and tb == L
        carry = (st_all.shape[0], j) if j == 0 else None
        kern = functools.partial(_gla_sample_kernel, bb=bb, lvalid=lvalid, carry=carry)
        sq = bb
        scratch = []
        st_spec = _state_specs(st_all, j, bb, carry)
    return pl.pallas_call(
        kern,
        grid=(B // bb, L // tb),
        in_specs=[pl.BlockSpec((sq, tb, GLA_KW), lambda b, t: (b, t, AB_GQ // GLA_KW)),
                  pl.BlockSpec((sq, tb, GLA_KW), lambda b, t: (b, t, AB_GK // GLA_KW)),
                  pl.BlockSpec((sq, tb, GLA_W), lambda b, t: (b, t, AB_GV // GLA_W)),
                  pl.BlockSpec((sq, tb, GLA_W), lambda b, t: (b, t, AB_ZGLA // GLA_W)),
                  pl.BlockSpec((sq, tb, LANES), lambda b, t: (b, t, 0)),
                  st_spec,
                  pl.BlockSpec((LANES, GLA_KW), const),
                  pl.BlockSpec((1, GLA_KW), const),
                  pl.BlockSpec((1, GLA_DV), const)],
        out_specs=[pl.BlockSpec((sq, tb, GLA_W), lambda b, t: (b, t, 0)), st_spec],
        out_shape=[jax.ShapeDtypeStruct((B, L, GLA_W), BF16), jax.ShapeDtypeStruct(st_all.shape, F32)],
        scratch_shapes=scratch,
        input_output_aliases={} if carry else {5: 1},
        compiler_params=_params(dimension_semantics=("parallel", "arbitrary")),
        name="gla_mixer",
    )(um, um, um, um, us, st_all, wa, ba, nrm)


SSD_GW = SSD_W // SSD_GROUPS
SSD_REP = SSD_HEADS // SSD_GROUPS
SSD_ROWS = 64


def _lane_expand(x, h0, width):
    return jnp.concatenate([jnp.broadcast_to(x[:, h0 + j:h0 + j + 1], (x.shape[0], width)) for j in range(SSD_REP)],
                           axis=1)


def _ssd_masks(C):
    ii = _iota2((SSD_ROWS, SSD_GW), 0)
    jj = _iota2((SSD_ROWS, SSD_GW), 1) & (SSD_ROWS - 1)
    sh = C.bit_length() - 1
    negc = jnp.where(((ii >> sh) == (jj >> sh)) & (jj <= ii), 0.0, -1e30)
    ri = _iota2((SSD_GW, SSD_GW), 0)
    ci = _iota2((SSD_GW, SSD_GW), 1)
    bd = ((ri >> 6) == (ci >> 6)).astype(F32)
    return negc, bd


def _ssd_intra(act, G, GT, dt, gi, negc, bd):
    h0 = gi * SSD_REP
    xs = act[:, gi * SSD_GW:(gi + 1) * SSD_GW]
    bm = act[:, SSD_W + gi * SSD_STATE:SSD_W + (gi + 1) * SSD_STATE].astype(BF16)
    cm = act[:, SSD_W + (SSD_GROUPS + gi) * SSD_STATE:SSD_W + (SSD_GROUPS + gi + 1) * SSD_STATE].astype(BF16)
    gcc = _lane_expand(G, h0, SSD_HEADDIM)
    grc = jnp.concatenate([GT[h0 + j:h0 + j + 1, :] for j in range(SSD_REP)], axis=1)
    vcat = xs * _lane_expand(dt, h0, SSD_HEADDIM)
    cbm = _nt(cm, bm)
    acat = (jnp.concatenate([cbm] * SSD_REP, axis=1) * jnp.exp((gcc - grc) + negc)).astype(BF16)
    vbd = (jnp.concatenate([vcat] * SSD_REP, axis=0) * bd).astype(BF16)
    return _mm(acat, vbd), xs, vcat, gcc, bm, cm


def _ssd_finish(o, xs, z, dsk, nrm):
    y = (o + dsk * xs) * _silu(z)
    return y * lax.rsqrt(jnp.mean(y * y, axis=-1, keepdims=True) + EPS) * nrm


def _ssd_prompt_kernel(z_ref, xbc_ref, us_ref, cst_ref, st_ref, cw_ref, cb_ref, prm_ref, nrm_ref, o_ref, so_ref,
                       halo_ref, scat_ref, *, tb):
    t = pl.program_id(1)
    C = SSD_ROWS
    cat_heads = lambda f, gi: jnp.concatenate([f(gi * SSD_REP + j) for j in range(SSD_REP)], axis=1)

    @pl.when(t == 0)
    def _():
        for gi in range(SSD_GROUPS):
            scat_ref[gi] = cat_heads(lambda h: st_ref[h], gi)
        halo_ref[0:5, :] = jnp.zeros((5, SSD_XBC), F32)
        halo_ref[5:8, :] = cst_ref[...]

    negc, bd = _ssd_masks(C)
    tril = (_iota2((C, C), 1) <= _iota2((C, C), 0)).astype(F32)
    lane = _iota2((1, LANES), 1)
    hmask = lane < SSD_HEADS
    neg_a = -jnp.exp(prm_ref[0:1, :])
    bias = prm_ref[1:2, :]
    dsk = [_lane_expand(prm_ref[2:3, :], gi * SSD_REP, SSD_HEADDIM) for gi in range(SSD_GROUPS)]
    cw = cw_ref[...]
    cb = cb_ref[...]
    nrm = nrm_ref[...]
    groups = range(SSD_GROUPS)

    def chunk(c, carry):
        r0 = pl.multiple_of(c * C, C)
        x = xbc_ref[pl.ds(r0, C), :]
        prev = xbc_ref[pl.ds(pl.multiple_of(jnp.maximum(r0 - SUBLANES, 0), SUBLANES), SUBLANES), :]
        halo = jnp.where(c == 0, halo_ref[...], prev)
        act = _silu(_conv_rows(halo, x, cw, cb))
        us = us_ref[pl.ds(r0, C), :]
        dt = jnp.where(hmask, jax.nn.softplus(us + bias), 0.0)
        G = _mm_hi(tril, neg_a * dt)
        GT = G.T
        z = z_ref[pl.ds(r0, C), :]
        parts = [_ssd_intra(act, G, GT, dt, gi, negc, bd) for gi in groups]
        S = [scat_ref[gi] for gi in groups]
        inter = [_mm(parts[gi][5], S[gi].astype(BF16)) for gi in groups]
        for gi in groups:
            o_in, xs, vcat, gcc, bm, cm = parts[gi]
            glc = gcc[C - 1:C, :]
            w = (vcat * jnp.exp(glc - gcc)).astype(BF16)
            scat_ref[gi] = jnp.exp(glc) * S[gi] + _tn(bm, w)
            y = _ssd_finish(jnp.exp(gcc) * inter[gi] + o_in, xs, z[:, gi * SSD_GW:(gi + 1) * SSD_GW], dsk[gi],
                            nrm[:, gi * SSD_GW:(gi + 1) * SSD_GW])
            o_ref[pl.ds(r0, C), gi * SSD_GW:(gi + 1) * SSD_GW] = y.astype(o_ref.dtype)
        return carry

    lax.fori_loop(0, tb // C, chunk, 0)
    halo_ref[...] = xbc_ref[tb - SUBLANES:tb, :]

    @pl.when(t == pl.num_programs(1) - 1)
    def _():
        for gi in groups:
            s = scat_ref[gi]
            for j in range(SSD_REP):
                so_ref[gi * SSD_REP + j] = s[:, j * SSD_HEADDIM:(j + 1) * SSD_HEADDIM]


def _ssd_sample_kernel(xs_ref, bm_ref, cm_ref, z_ref, dt_ref, cxs_ref, cbm_ref, ccm_ref, st_ref, wxs_ref, wbm_ref,
                       wcm_ref, prm_ref, nrm_ref, o_ref, so_ref, acc_ref, *, ls, carry):
    st_ref, so_ref = _own_layer(st_ref, so_ref, carry)
    h = pl.program_id(0)
    hj = h % SSD_REP

    def conv(x_ref, c_ref, w_ref):
        xx = [c_ref[r] for r in range(CONV_W - 1)] + [x_ref[t] for t in range(ls)]
        return [_silu(sum(xx[t + s] * w_ref[s] for s in range(CONV_W)) + w_ref[CONV_W]) for t in range(ls)]

    xs = conv(xs_ref, cxs_ref, wxs_ref)
    bm = conv(bm_ref, cbm_ref, wbm_ref)
    cm = conv(cm_ref, ccm_ref, wcm_ref)
    row = lambda ref, k: ref[k, pl.ds(h, 1), :]
    dt = [jax.nn.softplus(dt_ref[t, pl.ds(h, 1), :] + row(prm_ref, 1)) for t in range(ls)]
    neg_a = -jnp.exp(row(prm_ref, 0))
    G = []
    for t in range(ls):
        G.append(neg_a * dt[t] + (G[-1] if G else 0.0))
    v = [xs[t] * dt[t] for t in range(ls)]
    y = []
    for t in range(ls):
        acc = None
        for u in range(t + 1):
            coef = jnp.sum(cm[t] * bm[u], axis=0, keepdims=True) * jnp.exp(G[t] - G[u])
            acc = coef * v[u] if acc is None else acc + coef * v[u]
        y.append(acc)
    eg = [jnp.exp(G[t]) for t in range(ls)]
    wk = [bm[u] * jnp.exp(G[ls - 1] - G[u]) for u in range(ls)]
    inter = [None] * ls
    for n in range(SSD_STATE):
        S = st_ref[0, n]
        for t in range(ls):
            term = cm[t][n:n + 1, :] * S
            inter[t] = term if inter[t] is None else inter[t] + term
        s_new = eg[ls - 1] * S
        for u in range(ls):
            s_new = s_new + wk[u][n:n + 1, :] * v[u]
        so_ref[0, n] = s_new
    dsk = row(prm_ref, 2)
    for t in range(ls):
        acc_ref[t, pl.ds(pl.multiple_of(hj * SSD_HEADDIM, SSD_HEADDIM), SSD_HEADDIM), :] = (
            (y[t] + eg[t] * inter[t] + dsk * xs[t]) * _silu(z_ref[t]))

    @pl.when(hj == SSD_REP - 1)
    def _():
        for t in range(ls):
            yg = acc_ref[t]
            r = lax.rsqrt(jnp.mean(yg * yg, axis=0, keepdims=True) + EPS)
            o_ref[t] = (yg * r * nrm_ref[...]).astype(o_ref.dtype)


def _ssd_mixer(um, us, cst_all, st_all, j, cw, cb, prm, nrm, *, tb, sample):
    B, L, _ = um.shape
    if sample:
        return _ssd_mixer_sample(um, us, cst_all, st_all, j, cw, cb, prm, nrm)
    const = lambda *_: (0, 0)
    out_shape = [jax.ShapeDtypeStruct((B, L, SSD_W), BF16), jax.ShapeDtypeStruct(st_all.shape, F32)]
    st_blk = (SSD_HEADS, SSD_STATE, SSD_HEADDIM)
    assert tb % SSD_ROWS == 0
    return pl.pallas_call(
        functools.partial(_ssd_prompt_kernel, tb=tb),
        grid=(B, L // tb),
        in_specs=[pl.BlockSpec((None, tb, SSD_W), lambda b, t: (b, t, CD_Z // SSD_W)),
                  pl.BlockSpec((None, tb, SSD_XBC), lambda b, t: (b, t, CD_XBC // SSD_XBC)),
                  pl.BlockSpec((None, tb, LANES), lambda b, t: (b, t, 0)),
                  pl.BlockSpec((None, None, CONV_W - 1, SSD_XBC), lambda b, t: (j, b, 0, 0)),
                  pl.BlockSpec((None, None) + st_blk, lambda b, t: (j, b, 0, 0, 0)),
                  pl.BlockSpec((CONV_W, SSD_XBC), const), pl.BlockSpec((1, SSD_XBC), const),
                  pl.BlockSpec((3, LANES), const), pl.BlockSpec((1, SSD_W), const)],
        out_specs=[pl.BlockSpec((None, tb, SSD_W), lambda b, t: (b, t, 0)),
                   pl.BlockSpec((None, None) + st_blk, lambda b, t: (j, b, 0, 0, 0))],
        out_shape=out_shape,
        scratch_shapes=[pltpu.VMEM((SUBLANES, SSD_XBC), F32), pltpu.VMEM((SSD_GROUPS, SSD_STATE, SSD_GW), F32)],
        input_output_aliases={4: 1},
        compiler_params=_params(dimension_semantics=("parallel", "arbitrary")), name="ssd_mixer",
    )(um, um, us, cst_all, st_all, cw, cb, prm, nrm)


def _ssd_mixer_sample(um, us, cst_all, st_all, j, cw, cb, prm, nrm):
    B, ls, _ = um.shape
    n_layers = st_all.shape[0]
    chan_major = lambda x: jnp.transpose(x, (1, 2, 0))
    xbc_t = chan_major(um[:, :, CD_XBC:CD_XBC + SSD_XBC])
    z_t = chan_major(um[:, :, CD_Z:CD_Z + SSD_W])
    dt_t = chan_major(us[:, :, :SSD_HEADS])
    cst_t = chan_major(cst_all[j])
    st_t = jnp.transpose(st_all, (0, 2, 3, 4, 1))
    lanes = lambda x: jnp.broadcast_to(x[..., None], x.shape + (B,))
    w_t = lanes(jnp.concatenate([cw, cb], axis=0))
    prm_t = lanes(prm[:, :SSD_HEADS])
    nrm_t = lanes(nrm.reshape(SSD_W))
    carry = (n_layers, j) if j == 0 else None
    hd = SSD_HEADDIM
    n_xs = SSD_W // hd
    grp = lambda h: h // SSD_REP
    win = lambda rows, f: pl.BlockSpec((rows, hd, B), lambda h: (0, f(h), 0))
    if carry:
        st_spec = pl.BlockSpec((n_layers, 1, SSD_STATE, hd, B), lambda h: (0, h, 0, 0, 0))
    else:
        st_spec = pl.BlockSpec((None, 1, SSD_STATE, hd, B), lambda h: (j, h, 0, 0, 0))
    y_t, st_new = pl.pallas_call(
        functools.partial(_ssd_sample_kernel, ls=ls, carry=carry),
        grid=(SSD_HEADS,),
        in_specs=[win(ls, lambda h: h), win(ls, lambda h: n_xs + grp(h)), win(ls, lambda h: n_xs + SSD_GROUPS + grp(h)),
                  win(ls, lambda h: h),
                  pl.BlockSpec((ls, SSD_HEADS, B), lambda h: (0, 0, 0)),
                  win(CONV_W - 1, lambda h: h), win(CONV_W - 1, lambda h: n_xs + grp(h)),
                  win(CONV_W - 1, lambda h: n_xs + SSD_GROUPS + grp(h)),
                  st_spec,
                  win(CONV_W + 1, lambda h: h), win(CONV_W + 1, lambda h: n_xs + grp(h)),
                  win(CONV_W + 1, lambda h: n_xs + SSD_GROUPS + grp(h)),
                  pl.BlockSpec((3, SSD_HEADS, B), lambda h: (0, 0, 0)),
                  pl.BlockSpec((SSD_GW, B), lambda h: (grp(h), 0))],
        out_specs=[pl.BlockSpec((ls, SSD_GW, B), lambda h: (0, grp(h), 0)), st_spec],
        out_shape=[jax.ShapeDtypeStruct((ls, SSD_W, B), F32), jax.ShapeDtypeStruct(st_t.shape, F32)],
        scratch_shapes=[pltpu.VMEM((ls, SSD_GW, B), F32)],
        input_output_aliases={} if carry else {8: 1},
        compiler_params=_params(dimension_semantics=("arbitrary",)), name="ssd_mixer",
    )(xbc_t, xbc_t, xbc_t, z_t, dt_t, cst_t, cst_t, cst_t, st_t, w_t, w_t, w_t, prm_t, nrm_t)
    y = jnp.transpose(y_t, (2, 0, 1)).astype(BF16)
    return y, jnp.transpose(st_new, (0, 4, 1, 2, 3))


def _lru_kernel(z_ref, x_ref, cst_ref, st_ref, cw_ref, cb_ref, wbd_ref, prm_ref, o_ref, so_ref, halo_ref, carry_ref,
                *, bb, tb, lvalid, ltotal):
    t = pl.program_id(1)

    @pl.when(t == 0)
    def _():
        carry_ref[...] = st_ref[...]
        halo_ref[:, 0:5, :] = jnp.zeros((bb, 5, LRU_W), F32)
        halo_ref[:, 5:8, :] = cst_ref[...]

    cw = cw_ref[...]
    cb = cb_ref[...]
    b_a = prm_ref[0:1, :]
    b_x = prm_ref[1:2, :]
    sp = jax.nn.softplus(-prm_ref[2:3, :])
    row = _iota2((tb, 1), 0)
    last_t, last_r = (lvalid - 1) // tb, (lvalid - 1) % tb

    def per_batch(bi, carry):
        x = x_ref[bi]
        xc = _conv_rows(halo_ref[bi], x, cw, cb)
        halo_ref[bi] = x[tb - SUBLANES:tb, :]
        xb = xc.astype(BF16)
        rs, is_ = [], []
        for cg in range(LRU_W // LANES):
            ri = _mm(xb[:, cg * LANES:(cg + 1) * LANES], wbd_ref[cg])
            rs.append(ri[:, :LANES])
            is_.append(ri[:, LANES:])
        r = jax.nn.sigmoid(jnp.concatenate(rs, axis=1) + b_a)
        i = jax.nn.sigmoid(jnp.concatenate(is_, axis=1) + b_x)
        log_a = (-LRU_C) * r * sp
        a = jnp.exp(log_a)
        om = 1.0 - jnp.exp(2.0 * log_a)
        b = jnp.where(om > 0.0, om * lax.rsqrt(om), 0.0) * (i * xc)
        d = 1
        while d < tb:
            m = row >= d
            a_s = pltpu.roll(a, d, axis=0)
            b_s = pltpu.roll(b, d, axis=0)
            b = jnp.where(m, a * b_s + b, b)
            a = jnp.where(m, a * a_s, a)
            d *= 2
        hs = a * carry_ref[bi] + b
        carry_ref[bi] = hs[tb - 1:tb, :]

        @pl.when(t == last_t)
        def _():
            so_ref[bi] = hs[last_r:last_r + 1, :]

        o_ref[bi] = (hs * _silu(z_ref[bi])).astype(o_ref.dtype)
        return carry

    lax.fori_loop(0, bb, per_batch, 0)


def _lru_sample_kernel(z_ref, x_ref, cst_ref, st_ref, cw_ref, cb_ref, wbd_ref, prm_ref, o_ref, so_ref, *, ls):
    cw = cw_ref[...]
    cb = cb_ref[...]
    b_a = prm_ref[0:1, :]
    b_x = prm_ref[1:2, :]
    sp = jax.nn.softplus(-prm_ref[2:3, :])
    xx = [cst_ref[r] for r in range(CONV_W - 1)] + [x_ref[t] for t in range(ls)]
    h = st_ref[...]
    for t in range(ls):
        xc = sum(xx[t + s] * cw[s:s + 1] for s in range(CONV_W)) + cb
        xb = xc.astype(BF16)
        ri = [_mm(xb[:, cg * LANES:(cg + 1) * LANES], wbd_ref[cg]) for cg in range(LRU_W // LANES)]
        r = jax.nn.sigmoid(jnp.concatenate([x[:, :LANES] for x in ri], axis=1) + b_a)
        i = jax.nn.sigmoid(jnp.concatenate([x[:, LANES:] for x in ri], axis=1) + b_x)
        log_a = (-LRU_C) * r * sp
        om = 1.0 - jnp.exp(2.0 * log_a)
        h = jnp.exp(log_a) * h + jnp.where(om > 0.0, om * lax.rsqrt(om), 0.0) * (i * xc)
        o_ref[t] = (h * _silu(z_ref[t])).astype(o_ref.dtype)
    so_ref[...] = h


def _lru_mixer_sample(um, cst_all, st_all, j, cw, cb, wbd, prm):
    B, ls, _ = um.shape
    time_major = lambda x: jnp.transpose(x, (1, 0, 2))
    full = lambda *shape: pl.BlockSpec(shape, lambda i: (0,) * len(shape))
    y_t, st_new = pl.pallas_call(
        functools.partial(_lru_sample_kernel, ls=ls),
        grid=(1,),
        in_specs=[full(ls, B, LRU_W), full(ls, B, LRU_W), full(CONV_W - 1, B, LRU_W), full(B, LRU_W),
                  full(CONV_W, LRU_W), full(1, LRU_W), full(LRU_W // LANES, LANES, 2 * LANES), full(3, LRU_W)],
        out_specs=[full(ls, B, LRU_W), full(B, LRU_W)],
        out_shape=[jax.ShapeDtypeStruct((ls, B, LRU_W), BF16), jax.ShapeDtypeStruct((B, LRU_W), F32)],
        compiler_params=_params(dimension_semantics=("arbitrary",)),
        name="lru_mixer",
    )(time_major(um[:, :, CD_ZLRU:CD_ZLRU + LRU_W]), time_major(um[:, :, CD_XLRU:CD_XLRU + LRU_W]),
      time_major(cst_all[j]), st_all[j].reshape(B, LRU_W), cw, cb, wbd, prm)
    return time_major(y_t), st_new.reshape(B, 1, LRU_W)


def _lru_mixer(um, cst_all, st_all, j, cw, cb, wbd, prm, *, tb, sample):
    B, L, _ = um.shape
    if sample:
        return _lru_mixer_sample(um, cst_all, st_all, j, cw, cb, wbd, prm)
    bb = 1
    kern = functools.partial(_lru_kernel, bb=bb, tb=tb, lvalid=L, ltotal=L)
    const = lambda b, t: (0, 0)
    return pl.pallas_call(
        kern,
        grid=(B // bb, L // tb),
        in_specs=[pl.BlockSpec((bb, tb, LRU_W), lambda b, t: (b, t, CD_ZLRU // LRU_W)),
                  pl.BlockSpec((bb, tb, LRU_W), lambda b, t: (b, t, CD_XLRU // LRU_W)),
                  pl.BlockSpec((None, bb, CONV_W - 1, LRU_W), lambda b, t: (j, b, 0, 0)),
                  pl.BlockSpec((None, bb, 1, LRU_W), lambda b, t: (j, b, 0, 0)),
                  pl.BlockSpec((CONV_W, LRU_W), const),
                  pl.BlockSpec((1, LRU_W), const),
                  pl.BlockSpec((LRU_W // LANES, LANES, 2 * LANES), lambda b, t: (0, 0, 0)),
                  pl.BlockSpec((3, LRU_W), const)],
        out_specs=[pl.BlockSpec((bb, tb, LRU_W), lambda b, t: (b, t, 0)),
                   pl.BlockSpec((bb, 1, LRU_W), lambda b, t: (b, 0, 0))],
        out_shape=[jax.ShapeDtypeStruct((B, L, LRU_W), BF16),
                   jax.ShapeDtypeStruct((B, 1, LRU_W), F32)],
        scratch_shapes=[pltpu.VMEM((bb, SUBLANES, LRU_W), F32), pltpu.VMEM((bb, 1, LRU_W), F32)],
        compiler_params=_params(dimension_semantics=("parallel", "arbitrary")),
        name="lru_mixer",
    )(um, um, cst_all, st_all, cw, cb, wbd, prm)


def _lane_row(*pieces):
    v = jnp.concatenate([p.reshape(-1) for p in pieces])
    return jnp.pad(v, (0, LANES - v.shape[0])).reshape(1, LANES)


RP_TILE = 512
RP_TAIL = 32


def _repack_kernel(a_ref, b_ref, o_ref, *, plan):
    r = pl.program_id(1)
    lo = 0
    for n_tiles, _, shift in plan:
        @pl.when((r >= lo) & (r < lo + n_tiles))
        def _(shift=shift):
            if shift == 0:
                o_ref[...] = a_ref[...].astype(BF16)
            else:
                o_ref[0:RP_TILE - shift, :] = a_ref[shift:, :].astype(BF16)
                o_ref[RP_TILE - shift:, :] = b_ref[0:shift, :].astype(BF16)
        lo += n_tiles


def _repack_wt(wt, plan):
    n, _, D = wt.shape
    total = sum(p[0] for p in plan)

    def src_tile(r):
        lo, idx = 0, 0
        for n_tiles, start, _ in plan:
            idx = jnp.where((r >= lo) & (r < lo + n_tiles), start // RP_TILE + (r - lo), idx)
            lo += n_tiles
        return idx

    for _, start, shift in plan:
        assert start % RP_TILE == 0 and shift % 16 == 0 and shift <= RP_TAIL
    return pl.pallas_call(
        functools.partial(_repack_kernel, plan=plan),
        grid=(n, total),
        in_specs=[pl.BlockSpec((None, RP_TILE, D), lambda l, r: (l, src_tile(r), 0)),
                  pl.BlockSpec((None, RP_TAIL, D), lambda l, r: (l, (src_tile(r) + 1) * (RP_TILE // RP_TAIL), 0))],
        out_specs=pl.BlockSpec((None, RP_TILE, D), lambda l, r: (l, r, 0)),
        out_shape=jax.ShapeDtypeStruct((n, total * RP_TILE, D), BF16),
        compiler_params=_params(dimension_semantics=("parallel", "parallel")),
        name="repack_w",
    )(wt, wt)


def _rows_t(w, sizes, order, pad_to=None):
    wt = jnp.swapaxes(w, 1, 2)
    offs = [sum(sizes[:i]) for i in range(len(sizes))]
    out = jnp.concatenate([wt[:, offs[i]:offs[i] + sizes[i], :] for i in order], axis=1)
    if pad_to is not None:
        out = jnp.pad(out, ((0, 0), (0, pad_to - out.shape[1]), (0, 0)))
    return out


def _prep_weights(w):
    n_even = w['ab_w_in'].shape[0]
    n_odd = w['cd_w_in'].shape[0]
    D = w['ab_w_in'].shape[1]
    P = {'ab': [], 'cd': []}
    P['ab_w_main'] = _repack_wt(jnp.swapaxes(w['ab_w_in'], 1, 2), [(6, 0, 0), (6, 3072, 16), (2, 6144, 32)])
    P['ab_w_small'] = _rows_t(w['ab_w_in'], AB_SIZES, (1, 2, 7), LANES)
    P['ab_w_out'] = w['ab_w_out'].astype(BF16)
    for j in range(n_even):
        P['ab'].append(dict(
            dn_cw=w['dn_conv_w'][j],
            dn_prm=jnp.concatenate([_lane_row(jnp.zeros((DN_HEADS,), F32), w['dn_a_log'][j]),
                                    _lane_row(jnp.zeros((DN_HEADS,), F32), w['dn_dt_bias'][j])], axis=0),
            dn_norm=w['dn_norm'][j].reshape(1, DN_DV),
            gla_wa=jnp.pad(w['gla_wa2'][j], ((2 * DN_HEADS, LANES - 2 * DN_HEADS - GLA_RANK), (0, 0))).astype(BF16),
            gla_ba=w['gla_ba'][j].reshape(1, GLA_KW),
            gla_norm=w['gla_norm'][j].reshape(1, GLA_DV),
        ))
    P['cd_w_main'] = _repack_wt(jnp.swapaxes(w['cd_w_in'], 1, 2), [(2, 0, 0), (4, 2560, 16), (3, 1024, 0)])
    P['cd_w_small'] = _rows_t(w['cd_w_in'], CD_SIZES, (2,), LANES)
    P['cd_w_out'] = w['cd_w_out'].astype(BF16)
    eye2 = jnp.eye(2, dtype=F32)
    for j in range(n_odd):
        wa = w['lru_wa'][j].reshape(LRU_W // LANES, 2, LRU_BS, LRU_BS)
        wx = w['lru_wx'][j].reshape(LRU_W // LANES, 2, LRU_BS, LRU_BS)
        bd = lambda m: jnp.einsum('gbcd,be->gbced', m, eye2).reshape(LRU_W // LANES, LANES, LANES)
        P['cd'].append(dict(
            ssd_cw=w['ssd_conv_w'][j],
            ssd_cb=w['ssd_conv_b'][j].reshape(1, SSD_XBC),
            ssd_prm=jnp.concatenate([_lane_row(w['ssd_a_log'][j]), _lane_row(w['ssd_dt_bias'][j]),
                                     _lane_row(w['ssd_d'][j])], axis=0),
            ssd_norm=w['ssd_norm'][j].reshape(1, SSD_W),
            lru_cw=w['lru_conv_w'][j],
            lru_cb=w['lru_conv_b'][j].reshape(1, LRU_W),
            lru_wbd=jnp.concatenate([bd(wa), bd(wx)], axis=2).astype(BF16),
            lru_prm=jnp.stack([w['lru_ba'][j], w['lru_bx'][j], w['lru_lambda'][j]], axis=0),
        ))
    P['norm_g'] = w['norm_g'].reshape(-1, 1, D)
    P['ple_norm'] = w['ple_norm'].reshape(-1, 1, D)
    P['ple_w_gate'] = w['ple_w_gate'].astype(BF16)
    P['ple_w_proj'] = w['ple_w_proj'].astype(BF16)
    P['final_norm'] = w['final_norm'].reshape(1, D)
    return P


def _trunk(x, p, states, P, *, lpad, bb, tb, c_dn, c_gla):
    B, L, D = x.shape
    lvalid = L
    sample = lpad > 0
    pad_t = (lambda a: jnp.pad(a, ((0, 0), (0, lpad - L), (0, 0)))) if sample else (lambda a: a)
    depth = P['norm_g'].shape[0]
    st_dn, st_dnc, st_gla, st_ssd, st_ssdc, st_lru, st_lruc = states
    st_lru = st_lru.reshape(st_lru.shape[0], B, 1, LRU_W)
    dnc_new, ssdc_new, lru_new, lruc_new = [], [], [], []
    T = B * L
    h = x.reshape(T, D)
    p_all = p.reshape(depth, T, PLE_DIM)
    lo = lvalid - (CONV_W - 1)
    for li in range(depth):
        j = li // 2
        if li % 2 == 0:
            W = P['ab'][j]
            um, us = _inproj(h, P['norm_g'], li, P['ab_w_main'], P['ab_w_small'], j)
            um = um.reshape(B, L, AB_MAIN)
            us = us.reshape(B, L, LANES)
            um_p, us_p = pad_t(um), pad_t(us)
            oa, st_dn = _dn_mixer(um_p, us_p, st_dnc, st_dn, j, W['dn_cw'], W['dn_prm'], W['dn_norm'],
                                  bb=bb, tb=tb, C=c_dn, lvalid=lvalid)
            ob, st_gla = _gla_mixer(um_p, us_p, st_gla, j, W['gla_wa'], W['gla_ba'], W['gla_norm'],
                                    bb=bb, tb=tb, C=c_gla, lvalid=lvalid)
            oa, ob = oa[:, :L], ob[:, :L]
            dnc_new.append(um[:, lo:lvalid, AB_QKV:AB_QKV + DN_QKV])
        else:
            W = P['cd'][j]
            um, us = _inproj(h, P['norm_g'], li, P['cd_w_main'], P['cd_w_small'], j)
            um = um.reshape(B, L, CD_MAIN)
            us = us.reshape(B, L, LANES)
            oa, st_ssd = _ssd_mixer(um, us, st_ssdc, st_ssd, j, W['ssd_cw'], W['ssd_cb'], W['ssd_prm'],
                                    W['ssd_norm'], tb=tb, sample=sample)
            ob, lru_j = _lru_mixer(um, st_lruc, st_lru, j, W['lru_cw'], W['lru_cb'],
                                   W['lru_wbd'], W['lru_prm'], tb=tb, sample=sample)
            ssdc_new.append(um[:, lo:lvalid, CD_XBC:CD_XBC + SSD_XBC])
            lru_new.append(lru_j.reshape(B, LRU_W))
            lruc_new.append(um[:, lo:lvalid, CD_XLRU:CD_XLRU + LRU_W])
        h = _outproj(oa.reshape(T, -1), ob.reshape(T, -1), h, p_all, li, P['ab_w_out' if li % 2 == 0 else 'cd_w_out'], j,
                     P['ple_norm'], P['ple_w_gate'], P['ple_w_proj'], P['final_norm'], final=(li == depth - 1))
    new_states = [st_dn, jnp.stack(dnc_new), st_gla, st_ssd, jnp.stack(ssdc_new), jnp.stack(lru_new),
                  jnp.stack(lruc_new)]
    return h.reshape(B, L, D), new_states


def _pick_tb(L):
    tb = min(L, 512)
    assert L % tb == 0
    return tb


def kernel(x_prompt, x_sample, state_dn, state_dn_conv, state_gla, state_ssd, state_ssd_conv, state_lru, state_lru_conv, p_prompt, p_sample, norm_g, final_norm, ab_w_in, dn_conv_w, dn_a_log, dn_dt_bias, dn_norm, gla_wa2, gla_ba, gla_norm, ab_w_out, cd_w_in, ssd_conv_w, ssd_conv_b, ssd_a_log, ssd_dt_bias, ssd_d, ssd_norm, lru_conv_w, lru_conv_b, lru_wa, lru_ba, lru_wx, lru_bx, lru_lambda, cd_w_out, ple_w_proj, ple_norm, ple_w_gate):
    w = {
        'norm_g': norm_g, 'final_norm': final_norm,
        'ab_w_in': ab_w_in, 'dn_conv_w': dn_conv_w, 'dn_a_log': dn_a_log, 'dn_dt_bias': dn_dt_bias,
        'dn_norm': dn_norm, 'gla_wa2': gla_wa2, 'gla_ba': gla_ba, 'gla_norm': gla_norm, 'ab_w_out': ab_w_out,
        'cd_w_in': cd_w_in, 'ssd_conv_w': ssd_conv_w, 'ssd_conv_b': ssd_conv_b, 'ssd_a_log': ssd_a_log,
        'ssd_dt_bias': ssd_dt_bias, 'ssd_d': ssd_d, 'ssd_norm': ssd_norm,
        'lru_conv_w': lru_conv_w, 'lru_conv_b': lru_conv_b, 'lru_wa': lru_wa, 'lru_ba': lru_ba,
        'lru_wx': lru_wx, 'lru_bx': lru_bx, 'lru_lambda': lru_lambda, 'cd_w_out': cd_w_out,
        'ple_w_proj': ple_w_proj, 'ple_norm': ple_norm, 'ple_w_gate': ple_w_gate,
    }
    P = _prep_weights(w)
    n_even, n_odd = ab_w_in.shape[0], cd_w_in.shape[0]

    Bp, Lp, _ = x_prompt.shape
    assert Lp >= CONV_W - 1 and Lp % 64 == 0
    zero_states = (
        jnp.zeros((n_even, Bp, DN_HEADS, DN_DK, DN_DV), F32),
        jnp.zeros((n_even, Bp, CONV_W - 1, DN_QKV), F32),
        jnp.zeros((n_even, Bp, GLA_HEADS, GLA_DK, GLA_DV), F32),
        jnp.zeros((n_odd, Bp, SSD_HEADS, SSD_STATE, SSD_HEADDIM), F32),
        jnp.zeros((n_odd, Bp, CONV_W - 1, SSD_XBC), F32),
        jnp.zeros((n_odd, Bp, LRU_W), F32),
        jnp.zeros((n_odd, Bp, CONV_W - 1, LRU_W), F32),
    )
    y_p, st_p = _trunk(x_prompt, p_prompt, zero_states, P, lpad=0, bb=1, tb=_pick_tb(Lp), c_dn=64, c_gla=16)

    Bs, Ls, _ = x_sample.shape
    assert CONV_W - 1 <= Ls < SUBLANES
    sample_states = (state_dn, state_dn_conv, state_gla, state_ssd, state_ssd_conv, state_lru, state_lru_conv)
    bb_s = 8 if Bs % 8 == 0 else 1
    y_s, st_s = _trunk(x_sample, p_sample, sample_states, P, lpad=SUBLANES, bb=bb_s, tb=SUBLANES,
                       c_dn=SUBLANES, c_gla=SUBLANES)
    return (y_p, y_s, *st_p, *st_s)
```

```python
import functools

import jax
import jax.numpy as jnp
from jax import lax
from jax.experimental import pallas as pl
from jax.experimental.pallas import tpu as pltpu

F32 = jnp.float32
BF16 = jnp.bfloat16
HI = lax.Precision.HIGHEST

EPS = 1e-6
LOG2E = 1.4426950408889634
CONV_W = 4
PLE_DIM = 256
DN_HEADS, DN_DK, DN_DV = 8, 128, 128
DN_W = DN_HEADS * DN_DV
DN_QKV = 2 * DN_HEADS * DN_DK + DN_W
GLA_HEADS, GLA_DK, GLA_DV = 4, 128, 256
GLA_KW = GLA_HEADS * GLA_DK
GLA_W = GLA_HEADS * GLA_DV
GLA_RANK = 16
GLA_GATE_NORM = 16.0
AB_SIZES = (DN_QKV, DN_HEADS, DN_HEADS, DN_W, GLA_KW, GLA_KW, GLA_W, GLA_RANK, GLA_W)
SSD_HEADS, SSD_HEADDIM, SSD_GROUPS, SSD_STATE = 16, 64, 4, 64
SSD_W = SSD_HEADS * SSD_HEADDIM
SSD_XBC = SSD_W + 2 * SSD_GROUPS * SSD_STATE
LRU_W, LRU_BLOCKS = 1024, 16
LRU_BS = LRU_W // LRU_BLOCKS
LRU_C = 8.0
CD_SIZES = (SSD_W, SSD_XBC, SSD_HEADS, LRU_W, LRU_W)

LANES = 128
SUBLANES = 8
VMEM_LIMIT = 52 * 1024 * 1024

AB_QKV, AB_ZDN, AB_GQ, AB_GK, AB_GV, AB_ZGLA = 0, 3072, 4096, 4608, 5120, 6144
AB_MAIN = 7168
CD_Z, CD_ZLRU, CD_XLRU, CD_XBC = 0, 1024, 2048, 3072
CD_MAIN = 4608


def _silu(x):
    return x * jax.nn.sigmoid(x)


def _nt(a, b):
    return lax.dot_general(a, b, (((1,), (1,)), ((), ())), preferred_element_type=F32)


def _tn(a, b):
    return lax.dot_general(a, b, (((0,), (0,)), ((), ())), preferred_element_type=F32)


def _mm(a, b):
    return jnp.dot(a, b, preferred_element_type=F32)


def _mm_hi(a, b):
    return jnp.dot(a, b, preferred_element_type=F32, precision=HI)


def _iota2(shape, dim):
    return lax.broadcasted_iota(jnp.int32, shape, dim)


def _conv_rows(halo, x, w, bias=None):
    cat = jnp.concatenate([halo, x], axis=0)
    out = x * w[3:4]
    for s in (1, 2, 3):
        out = out + pltpu.roll(cat, s, axis=0)[SUBLANES:] * w[3 - s:4 - s]
    if bias is not None:
        out = out + bias
    return out


def _params(**kw):
    return pltpu.CompilerParams(vmem_limit_bytes=VMEM_LIMIT, **kw)


def _own_layer(st_ref, so_ref, carry):
    if carry is None:
        return st_ref, so_ref
    n, jl = carry
    for l in range(n):
        if l != jl:
            so_ref[l] = st_ref[l]
    return st_ref.at[jl], so_ref.at[jl]


def _state_specs(st_all, j, bb, carry):
    tail = st_all.shape[2:]
    zeros = (0,) * len(tail)
    if carry:
        return pl.BlockSpec((st_all.shape[0], bb) + tail, lambda b, *_: (0, b) + zeros)
    return pl.BlockSpec((None, bb) + tail, lambda b, *_: (j, b) + zeros)


def _inproj_kernel(h_ref, g_ref, wm_ref, ws_ref, um_ref, us_ref, xn_ref):
    @pl.when(pl.program_id(1) == 0)
    def _():
        x = h_ref[...]
        r = lax.rsqrt(jnp.mean(x * x, axis=-1, keepdims=True) + EPS)
        xn = (x * r * g_ref[...]).astype(BF16)
        xn_ref[...] = xn
        us_ref[...] = _nt(xn, ws_ref[...].astype(BF16))

    um_ref[...] = _nt(xn_ref[...], wm_ref[...])


def _inproj(h, g_all, li, wm_all, ws_all, j):
    T, D = h.shape
    Nm = wm_all.shape[1]
    tm = min(T, 1024)
    tn = 1024 if Nm % 1024 == 0 else 1536
    return pl.pallas_call(
        _inproj_kernel,
        grid=(T // tm, Nm // tn),
        in_specs=[pl.BlockSpec((tm, D), lambda i, c: (i, 0)),
                  pl.BlockSpec((None, 1, D), lambda i, c: (li, 0, 0)),
                  pl.BlockSpec((None, tn, D), lambda i, c: (j, c, 0)),
                  pl.BlockSpec((None, LANES, D), lambda i, c: (j, 0, 0))],
        out_specs=[pl.BlockSpec((tm, tn), lambda i, c: (i, c)),
                   pl.BlockSpec((tm, LANES), lambda i, c: (i, 0))],
        out_shape=[jax.ShapeDtypeStruct((T, Nm), F32), jax.ShapeDtypeStruct((T, LANES), F32)],
        scratch_shapes=[pltpu.VMEM((tm, D), BF16)],
        compiler_params=_params(dimension_semantics=("parallel", "arbitrary")),
        name="inproj",
    )(h, g_all, wm_all, ws_all)


def _outproj_kernel(oa_ref, ob_ref, h_ref, p_ref, wa_ref, wb_ref, pg_ref, wg_ref, wp_ref, fg_ref, out_ref, *, final):
    h1 = h_ref[...] + _mm(oa_ref[...], wa_ref[...]) + _mm(ob_ref[...], wb_ref[...])
    r = lax.rsqrt(jnp.mean(h1 * h1, axis=-1, keepdims=True) + EPS)
    hn = (h1 * r * pg_ref[...]).astype(BF16)
    gate = jax.nn.sigmoid(_mm(hn, wg_ref[...]))
    pp = _mm(p_ref[...].astype(BF16), wp_ref[...])
    h2 = h1 + gate * pp
    if final:
        r2 = lax.rsqrt(jnp.mean(h2 * h2, axis=-1, keepdims=True) + EPS)
        h2 = h2 * r2 * fg_ref[...]
    out_ref[...] = h2


def _outproj(oa, ob, h, p_all, li, w_out_all, j, pg_all, wg_all, wp_all, fg, final):
    T, D = h.shape
    Wa, Wb = oa.shape[1], ob.shape[1]
    assert Wa == Wb
    tm = min(T, 512)
    resident = lambda shape, idx: pl.BlockSpec(shape, lambda i: idx, pipeline_mode=pl.Buffered(1))
    return pl.pallas_call(
        functools.partial(_outproj_kernel, final=final),
        grid=(T // tm,),
        in_specs=[pl.BlockSpec((tm, Wa), lambda i: (i, 0)),
                  pl.BlockSpec((tm, Wb), lambda i: (i, 0)),
                  pl.BlockSpec((tm, D), lambda i: (i, 0)),
                  pl.BlockSpec((None, tm, PLE_DIM), lambda i: (li, i, 0)),
                  resident((None, Wa, D), (j, 0, 0)),
                  resident((None, Wb, D), (j, 1, 0)),
                  resident((None, 1, D), (li, 0, 0)),
                  resident((None, D, D), (li, 0, 0)),
                  resident((None, PLE_DIM, D), (li, 0, 0)),
                  resident((1, D), (0, 0))],
        out_specs=pl.BlockSpec((tm, D), lambda i: (i, 0)),
        out_shape=jax.ShapeDtypeStruct((T, D), F32),
        compiler_params=_params(dimension_semantics=("parallel",)),
        name="outproj",
    )(oa, ob, h, p_all, w_out_all, w_out_all, pg_all, wg_all, wp_all, fg)


R_BD = 128


def _bd_masks(C):
    ii = _iota2((R_BD, R_BD), 0)
    jj = _iota2((R_BD, R_BD), 1)
    sh = C.bit_length() - 1
    lower = ((ii >> sh) == (jj >> sh)) & (jj <= ii)
    neg = jnp.where(lower, 0.0, -1e30)
    eye = (ii == jj).astype(F32)
    return neg, 1.0 - eye, eye


def _dn_groups(Q, K, V, beta, gc, gr, neg, offdiag, eye, C):
    n = range(len(Q))
    Kb = [K[g].astype(BF16) for g in n]
    KK = [_nt(Kb[g], Kb[g]) for g in n]
    QK = [_nt(Q[g].astype(BF16), Kb[g]) for g in n]
    dm = [jnp.exp((gc[g] - gr[g]) + neg) for g in n]
    A = [(beta[g] * KK[g]) * (dm[g] * offdiag) for g in n]
    T = [eye - A[g] for g in n]
    X = A
    p = 2
    while p < C:
        Xb = [X[g].astype(BF16) for g in n]
        X = [_mm(Xb[g], Xb[g]) for g in n]
        T = [T[g] + _mm(T[g].astype(BF16), X[g].astype(BF16)) for g in n]
        p *= 2
    eg = [jnp.exp(gc[g]) for g in n]
    rhs = [jnp.concatenate([beta[g] * V[g], (beta[g] * eg[g]) * K[g]], axis=1).astype(BF16) for g in n]
    sol = [_mm(T[g].astype(BF16), rhs[g]) for g in n]
    return ([sol[g][:, :DN_DV] for g in n], [sol[g][:, DN_DV:] for g in n],
            [QK[g] * dm[g] for g in n], [Q[g] * eg[g] for g in n])


def _l2n(x):
    return x * lax.rsqrt(jnp.sum(x * x, axis=-1, keepdims=True) + EPS)


def _dn_apply(S, u0, wk, qkd, qg, K, gl, gc, C):
    n = range(len(S))
    nu = len(S[0])
    blk = lambda x, i: x[i * C:(i + 1) * C]
    r = [[None] * nu for _ in n]
    for i in range(nu):
        for g in n:
            lhs = jnp.concatenate([blk(wk[g], i), blk(qg[g], i)], axis=0).astype(BF16)
            r[g][i] = _mm(lhs, S[g][i].astype(BF16))
    Ub = [jnp.concatenate([blk(u0[g], i) - r[g][i][:C] for i in range(nu)], axis=0).astype(BF16) for g in n]
    O = [jnp.concatenate([r[g][i][C:] for i in range(nu)], axis=0) + _mm(qkd[g].astype(BF16), Ub[g]) for g in n]
    kend = [(K[g] * jnp.exp(gl[g] - gc[g])).astype(BF16) for g in n]
    egl = [jnp.exp(gl[g]) for g in n]
    S_new = [[None] * nu for _ in n]
    for i in range(nu):
        for g in n:
            S_new[g][i] = egl[g][i * C:i * C + 1, :] * S[g][i] + _tn(blk(kend[g], i), blk(Ub[g], i))
    return O, S_new


def _dn_prompt_kernel(qkv_ref, z_ref, us_ref, cst_ref, st_ref, cw_ref, prm_ref, nrm_ref, o_ref, so_ref, halo_ref,
                      *, tb, C, nc):
    t = pl.program_id(1)
    hg = R_BD // C
    CR = nc * C

    @pl.when(t == 0)
    def _():
        so_ref[...] = st_ref[...]
        halo_ref[0:5, :] = jnp.zeros((5, DN_QKV), F32)
        halo_ref[5:8, :] = cst_ref[...]

    neg, offdiag, eye = _bd_masks(C)
    ii = _iota2((CR, CR), 0)
    jj = _iota2((CR, CR), 1)
    sh = C.bit_length() - 1
    tril = (((ii >> sh) == (jj >> sh)) & (jj <= ii)).astype(F32)
    lane = _iota2((1, LANES), 1)
    gmask = (lane >= DN_HEADS) & (lane < 2 * DN_HEADS)
    neg_a = -jnp.exp(prm_ref[0:1, :])
    bias = prm_ref[1:2, :]
    cw = cw_ref[...]
    nrm = nrm_ref[...]
    head_sets = [range(gi * hg, (gi + 1) * hg) for gi in range(DN_HEADS // hg)]
    groups = [(ci, hs) for ci in range(nc) for hs in head_sets]
    rws = lambda ci: slice(ci * C, (ci + 1) * C)

    def step(cp, carry):
        r0 = pl.multiple_of(cp * CR, CR)
        x = qkv_ref[pl.ds(r0, CR), :]
        prev = qkv_ref[pl.ds(pl.multiple_of(jnp.maximum(r0 - SUBLANES, 0), SUBLANES), SUBLANES), :]
        halo = jnp.where(cp == 0, halo_ref[...], prev)
        act = _silu(_conv_rows(halo, x, cw))
        us = us_ref[pl.ds(r0, CR), :]
        g = jnp.where(gmask, neg_a * jax.nn.softplus(us + bias), 0.0)
        beta_all = jax.nn.sigmoid(us)
        G = _mm_hi(tril, g)
        GT = G.T
        z = z_ref[pl.ds(r0, CR), :]
        cat0 = lambda f: [jnp.concatenate([f(ci, h) for h in hs], axis=0) for ci, hs in groups]
        Q = [_l2n(a) * (DN_DK ** -0.5) for a in cat0(lambda ci, h: act[rws(ci), h * DN_DK:(h + 1) * DN_DK])]
        K = [_l2n(a) for a in cat0(lambda ci, h: act[rws(ci), DN_W + h * DN_DK:DN_W + (h + 1) * DN_DK])]
        V = cat0(lambda ci, h: act[rws(ci), 2 * DN_W + h * DN_DV:2 * DN_W + (h + 1) * DN_DV])
        beta = cat0(lambda ci, h: beta_all[rws(ci), h:h + 1])
        gc = cat0(lambda ci, h: G[rws(ci), DN_HEADS + h:DN_HEADS + h + 1])
        gl = cat0(lambda ci, h: jnp.broadcast_to(
            G[(ci + 1) * C - 1:(ci + 1) * C, DN_HEADS + h:DN_HEADS + h + 1], (C, 1)))
        gr = [jnp.concatenate([GT[DN_HEADS + h:DN_HEADS + h + 1, rws(ci)] for h in hs], axis=1)
              for ci, hs in groups]
        u0, wk, qkd, qg = _dn_groups(Q, K, V, beta, gc, gr, neg, offdiag, eye, C)
        Z = cat0(lambda ci, h: z[rws(ci), h * DN_DV:(h + 1) * DN_DV])
        S = [[so_ref[h] for h in hs] for hs in head_sets]
        ng = len(head_sets)
        for ci in range(nc):
            pick = lambda xs: xs[ci * ng:(ci + 1) * ng]
            O, S = _dn_apply(S, pick(u0), pick(wk), pick(qkd), pick(qg), pick(K), pick(gl), pick(gc), C)
            for gi, hs in enumerate(head_sets):
                o = O[gi] * lax.rsqrt(jnp.mean(O[gi] * O[gi], axis=-1, keepdims=True) + EPS) * nrm
                out = (o * _silu(Z[ci * ng + gi])).astype(o_ref.dtype)
                for i, h in enumerate(hs):
                    o_ref[pl.ds(r0 + ci * C, C), h * DN_DV:(h + 1) * DN_DV] = out[i * C:(i + 1) * C]
        for gi, hs in enumerate(head_sets):
            for i, h in enumerate(hs):
                so_ref[h] = S[gi][i]
        return carry

    lax.fori_loop(0, tb // CR, step, 0)
    halo_ref[...] = qkv_ref[tb - SUBLANES:tb, :]


def _dn_sample_kernel(qkv_ref, z_ref, us_ref, cst_ref, st_ref, cw_ref, prm_ref, nrm_ref, o_ref, so_ref, halo_ref,
                      *, bb, lvalid, carry):
    st_ref, so_ref = _own_layer(st_ref, so_ref, carry)
    C = SUBLANES
    nb = R_BD // (DN_HEADS * C)
    rows_all = bb * C

    halo_ref[:, 0:5, :] = jnp.zeros((bb, 5, DN_QKV), F32)
    halo_ref[:, 5:8, :] = cst_ref[...]
    cw = cw_ref[...]
    nrm = nrm_ref[...]
    acts = [_silu(_conv_rows(halo_ref[b], qkv_ref[b], cw)) for b in range(bb)]

    neg, offdiag, eye = _bd_masks(C)
    lane = _iota2((1, LANES), 1)
    gmask = (lane >= DN_HEADS) & (lane < 2 * DN_HEADS)
    neg_a = -jnp.exp(prm_ref[0:1, :])
    bias = prm_ref[1:2, :]
    us = us_ref[...].reshape(rows_all, LANES)
    valid = (_iota2((rows_all, 1), 0) & (C - 1)) < lvalid
    g = jnp.where(gmask & valid, neg_a * jax.nn.softplus(us + bias), 0.0)
    beta_all = jnp.where(valid, jax.nn.sigmoid(us), 0.0)
    ii = _iota2((rows_all, rows_all), 0)
    jj = _iota2((rows_all, rows_all), 1)
    sh = C.bit_length() - 1
    same = (ii >> sh) == (jj >> sh)
    G_all = _mm_hi((same & (jj <= ii)).astype(F32), g)
    GL_all = _mm_hi(same.astype(F32), g)
    GT_all = G_all.T

    groups = [[(h, gi * nb + bl) for h in range(DN_HEADS) for bl in range(nb)]
              for gi in range(bb // nb)]
    rows = [slice(gi * nb * C, (gi + 1) * nb * C) for gi in range(bb // nb)]
    cat0 = lambda f: [jnp.concatenate([f(h, b) for h, b in us_], axis=0) for us_ in groups]
    cath = lambda f: [jnp.concatenate([f(h, rs) for h in range(DN_HEADS)], axis=0) for rs in rows]
    Q = [_l2n(x) * (DN_DK ** -0.5) for x in cat0(lambda h, b: acts[b][:, h * DN_DK:(h + 1) * DN_DK])]
    K = [_l2n(x) for x in cat0(lambda h, b: acts[b][:, DN_W + h * DN_DK:DN_W + (h + 1) * DN_DK])]
    V = cat0(lambda h, b: acts[b][:, 2 * DN_W + h * DN_DV:2 * DN_W + (h + 1) * DN_DV])
    beta = cath(lambda h, rs: beta_all[rs, h:h + 1])
    gc = cath(lambda h, rs: G_all[rs, DN_HEADS + h:DN_HEADS + h + 1])
    gl = cath(lambda h, rs: GL_all[rs, DN_HEADS + h:DN_HEADS + h + 1])
    gr = [jnp.concatenate([GT_all[DN_HEADS + h:DN_HEADS + h + 1, rs] for h in range(DN_HEADS)], axis=1)
          for rs in rows]
    u0, wk, qkd, qg = _dn_groups(Q, K, V, beta, gc, gr, neg, offdiag, eye, C)
    S = [[st_ref[b, h] for h, b in us_] for us_ in groups]
    O, S_new = _dn_apply(S, u0, wk, qkd, qg, K, gl, gc, C)
    Z = cat0(lambda h, b: z_ref[b][:, h * DN_DV:(h + 1) * DN_DV])
    for g, us_ in enumerate(groups):
        o = O[g] * lax.rsqrt(jnp.mean(O[g] * O[g], axis=-1, keepdims=True) + EPS) * nrm
        out = (o * _silu(Z[g])).astype(o_ref.dtype)
        for i, (h, b) in enumerate(us_):
            so_ref[b, h] = S_new[g][i]
            o_ref[b, :, h * DN_DV:(h + 1) * DN_DV] = out[i * C:(i + 1) * C]


def _dn_mixer(um, us, cst_all, st_all, j, cw, prm, nrm, *, bb, tb, C, lvalid):
    B, L, _ = um.shape
    const = lambda *_: (0, 0)
    out_shape = [jax.ShapeDtypeStruct((B, L, DN_W), BF16), jax.ShapeDtypeStruct(st_all.shape, F32)]
    carry = None
    if lvalid == L:
        assert bb == 1
        kern = functools.partial(_dn_prompt_kernel, tb=tb, C=C, nc=4)
        grid = (B, L // tb)
        in_specs = [pl.BlockSpec((None, tb, DN_QKV), lambda b, t: (b, t, AB_QKV // DN_QKV)),
                    pl.BlockSpec((None, tb, DN_W), lambda b, t: (b, t, AB_ZDN // DN_W)),
                    pl.BlockSpec((None, tb, LANES), lambda b, t: (b, t, 0)),
                    pl.BlockSpec((None, None, CONV_W - 1, DN_QKV), lambda b, t: (j, b, 0, 0)),
                    pl.BlockSpec((None, None, DN_HEADS, DN_DK, DN_DV), lambda b, t: (j, b, 0, 0, 0)),
                    pl.BlockSpec((CONV_W, DN_QKV), const),
                    pl.BlockSpec((2, LANES), const),
                    pl.BlockSpec((1, DN_DV), const)]
        out_specs = [pl.BlockSpec((None, tb, DN_W), lambda b, t: (b, t, 0)),
                     pl.BlockSpec((None, None, DN_HEADS, DN_DK, DN_DV), lambda b, t: (j, b, 0, 0, 0))]
        scratch = [pltpu.VMEM((SUBLANES, DN_QKV), F32)]
        sem = ("parallel", "arbitrary")
    else:
        assert L == SUBLANES and C == SUBLANES
        carry = (st_all.shape[0], j) if j == 0 else None
        kern = functools.partial(_dn_sample_kernel, bb=bb, lvalid=lvalid, carry=carry)
        grid = (B // bb,)
        in_specs = [pl.BlockSpec((bb, L, DN_QKV), lambda b: (b, 0, AB_QKV // DN_QKV)),
                    pl.BlockSpec((bb, L, DN_W), lambda b: (b, 0, AB_ZDN // DN_W)),
                    pl.BlockSpec((bb, L, LANES), lambda b: (b, 0, 0)),
                    pl.BlockSpec((None, bb, CONV_W - 1, DN_QKV), lambda b: (j, b, 0, 0)),
                    _state_specs(st_all, j, bb, carry),
                    pl.BlockSpec((CONV_W, DN_QKV), const),
                    pl.BlockSpec((2, LANES), const),
                    pl.BlockSpec((1, DN_DV), const)]
        out_specs = [pl.BlockSpec((bb, L, DN_W), lambda b: (b, 0, 0)), _state_specs(st_all, j, bb, carry)]
        scratch = [pltpu.VMEM((bb, SUBLANES, DN_QKV), F32)]
        sem = ("parallel",)
    return pl.pallas_call(
        kern, grid=grid, in_specs=in_specs, out_specs=out_specs, out_shape=out_shape,
        scratch_shapes=scratch, input_output_aliases={} if carry else {4: 1},
        compiler_params=_params(dimension_semantics=sem), name="dn_mixer",
    )(um, um, us, cst_all, st_all, cw, prm, nrm)


def _gla_sample_kernel(q_ref, k_ref, v_ref, z_ref, us_ref, st_ref, wa_ref, ba_ref, nrm_ref, o_ref, so_ref,
                       *, bb, lvalid, carry):
    st_ref, so_ref = _own_layer(st_ref, so_ref, carry)
    C = SUBLANES
    R = bb * C
    ii = _iota2((C, C), 0)
    jj = _iota2((C, C), 1)
    nrm = nrm_ref[...]
    valid = (_iota2((R, 1), 0) & (C - 1)) < lvalid
    us = us_ref[...].reshape(R, LANES)
    glog = jax.nn.log_sigmoid(_mm(us.astype(BF16), wa_ref[...]) + ba_ref[...]) * (1.0 / GLA_GATE_NORM)
    glog = jnp.where(valid, glog, 0.0)
    ri = _iota2((R, R), 0)
    ci = _iota2((R, R), 1)
    sh = C.bit_length() - 1
    G = _mm_hi((((ri >> sh) == (ci >> sh)) & (ci <= ri)).astype(F32), glog)
    q = q_ref[...].reshape(R, GLA_KW) * (GLA_DK ** -0.5)
    k = jnp.where(valid, k_ref[...].reshape(R, GLA_KW), 0.0)
    v = jnp.where(valid, v_ref[...].reshape(R, GLA_W), 0.0)
    z = z_ref[...].reshape(R, GLA_W)
    units = [(b, h) for b in range(bb) for h in range(GLA_HEADS)]
    U = range(len(units))
    tile = lambda x, b, h, w: x[b * C:(b + 1) * C, h * w:(h + 1) * w]
    Gu = [tile(G, b, h, GLA_DK) for b, h in units]
    qu = [tile(q, b, h, GLA_DK) for b, h in units]
    ku = [tile(k, b, h, GLA_DK) for b, h in units]
    vu = [tile(v, b, h, GLA_DV).astype(BF16) for b, h in units]
    AT = [jnp.zeros((C, C), F32) for _ in U]
    for i in range(C):
        for u in U:
            e = jnp.exp(jnp.minimum(Gu[u][i:i + 1, :] - Gu[u], 0.0))
            col = jnp.sum(ku[u] * e * qu[u][i:i + 1, :], axis=1, keepdims=True)
            AT[u] = AT[u] + jnp.where(jj == i, col, 0.0)
    AT = [jnp.where(ii <= jj, AT[u], 0.0).astype(BF16) for u in U]
    S = [st_ref[b, h] for b, h in units]
    o = [_mm((qu[u] * jnp.exp(Gu[u])).astype(BF16), S[u].astype(BF16)) + _tn(AT[u], vu[u]) for u in U]
    kend = [(ku[u] * jnp.exp(Gu[u][C - 1:C, :] - Gu[u])).astype(BF16) for u in U]
    gend = [jnp.exp(Gu[u].T[:, C - 1:C]) for u in U]
    for u, (b, h) in enumerate(units):
        so_ref[b, h] = gend[u] * S[u] + _tn(kend[u], vu[u])
    o = [o[u] * lax.rsqrt(jnp.mean(o[u] * o[u], axis=-1, keepdims=True) + EPS) * nrm for u in U]
    o = [o[u] * _silu(tile(z, b, h, GLA_DV)) for u, (b, h) in enumerate(units)]
    rows = [jnp.concatenate(o[b * GLA_HEADS:(b + 1) * GLA_HEADS], axis=1) for b in range(bb)]
    o_ref[...] = jnp.concatenate(rows, axis=0).astype(o_ref.dtype).reshape(bb, C, GLA_W)


def _gla_prompt_kernel(q_ref, k_ref, v_ref, z_ref, us_ref, st_ref, wa_ref, ba_ref, nrm_ref, o_ref, so_ref, g_ref,
                       *, tb, C):
    t = pl.program_id(1)
    H = range(GLA_HEADS)
    nch = tb // C

    @pl.when(t == 0)
    def _():
        so_ref[...] = st_ref[...]

    glog = jax.nn.log_sigmoid(_mm(us_ref[...].astype(BF16), wa_ref[...]) + ba_ref[...]) * (1.0 / GLA_GATE_NORM)
    step_in_chunk = _iota2((tb, 1), 0) & (C - 1)
    d = 1
    while d < C:
        glog = glog + jnp.where(step_in_chunk >= d, pltpu.roll(glog, d, axis=0), 0.0)
        d *= 2
    g_ref[...] = glog * LOG2E

    ii = _iota2((C, C), 0)
    jj = _iota2((C, C), 1)
    nrm = nrm_ref[...]
    hs = lambda x, h, w: x[:, h * w:(h + 1) * w]

    def scores(c):
        rows = pl.ds(pl.multiple_of(c * C, C), C)
        G = g_ref[rows, :]
        q = q_ref[rows, :] * (GLA_DK ** -0.5)
        k = k_ref[rows, :]
        nt = C // SUBLANES
        Gt = [[hs(G, h, GLA_DK)[t * SUBLANES:(t + 1) * SUBLANES] for t in range(nt)] for h in H]
        kt = [[hs(k, h, GLA_DK)[t * SUBLANES:(t + 1) * SUBLANES] for t in range(nt)] for h in H]
        acc = [[jnp.zeros((SUBLANES, C), F32) for _ in range(nt)] for _ in H]
        for i in range(C):
            for t in range(i // SUBLANES + 1):
                for h in H:
                    e = jnp.exp2(jnp.minimum(hs(G, h, GLA_DK)[i:i + 1, :] - Gt[h][t], 0.0))
                    col = jnp.sum(kt[h][t] * e * hs(q, h, GLA_DK)[i:i + 1, :], axis=1, keepdims=True)
                    acc[h][t] = acc[h][t] + jnp.where(jj[:SUBLANES] == i, col, 0.0)
        AT = [jnp.where(ii <= jj, jnp.concatenate(acc[h], axis=0), 0.0).astype(BF16) for h in H]
        gend = [jnp.exp2(hs(G, h, GLA_DK).T[:, C - 1:C]) for h in H]
        return AT, gend

    def finish(o, c):
        rows = pl.ds(pl.multiple_of(c * C, C), C)
        z = z_ref[rows, :]
        o = [o[h] * lax.rsqrt(jnp.mean(o[h] * o[h], axis=-1, keepdims=True) + EPS) * nrm for h in H]
        o = [o[h] * _silu(hs(z, h, GLA_DV)) for h in H]
        o_ref[rows, :] = jnp.concatenate(o, axis=1).astype(o_ref.dtype)

    def body(c, carry):
        AT, gend, o_prev = carry
        nxt = scores(jnp.minimum(c + 1, nch - 1))
        rows = pl.ds(pl.multiple_of(c * C, C), C)
        G = g_ref[rows, :]
        q = q_ref[rows, :] * (GLA_DK ** -0.5)
        k = k_ref[rows, :]
        v = v_ref[rows, :]
        S = [so_ref[h] for h in H]
        vh = [hs(v, h, GLA_DV).astype(BF16) for h in H]
        o = [_mm((hs(q, h, GLA_DK) * jnp.exp2(hs(G, h, GLA_DK))).astype(BF16), S[h].astype(BF16)) + _tn(AT[h], vh[h])
             for h in H]
        kend = [(hs(k, h, GLA_DK) * jnp.exp2(hs(G, h, GLA_DK)[C - 1:C, :] - hs(G, h, GLA_DK))).astype(BF16) for h in H]
        for h in H:
            so_ref[h] = gend[h] * S[h] + _tn(kend[h], vh[h])
        finish(o_prev, jnp.maximum(c - 1, 0))
        return nxt + (o,)

    AT0, gend0 = scores(0)
    _, _, o_last = lax.fori_loop(0, nch, body, (AT0, gend0, [jnp.zeros((C, GLA_DV), F32) for _ in H]))
    finish(o_last, nch - 1)


def _gla_mixer(um, us, st_all, j, wa, ba, nrm, *, bb, tb, C, lvalid):
    B, L, _ = um.shape
    const = lambda *_: (0, 0)
    carry = None
    if lvalid == L:
        assert bb == 1
        kern = functools.partial(_gla_prompt_kernel, tb=tb, C=C)
        sq = None
        scratch = [pltpu.VMEM((tb, GLA_KW), F32)]
        st_spec = pl.BlockSpec((None, None, GLA_HEADS, GLA_DK, GLA_DV), lambda b, t: (j, b, 0, 0, 0))
    else:
        assert L == SUBLANES and tb == L
        carry = (st_all.shape[0], j) if j == 0 else None
        kern = functools.partial(_gla_sample_kernel, bb=bb, lvalid=lvalid, carry=carry)
        sq = bb
        scratch = []
        st_spec = _state_specs(st_all, j, bb, carry)
    return pl.pallas_call(
        kern,
        grid=(B // bb, L // tb),
        in_specs=[pl.BlockSpec((sq, tb, GLA_KW), lambda b, t: (b, t, AB_GQ // GLA_KW)),
                  pl.BlockSpec((sq, tb, GLA_KW), lambda b, t: (b, t, AB_GK // GLA_KW)),
                  pl.BlockSpec((sq, tb, GLA_W), lambda b, t: (b, t, AB_GV // GLA_W)),
                  pl.BlockSpec((sq, tb, GLA_W), lambda b, t: (b, t, AB_ZGLA // GLA_W)),
                  pl.BlockSpec((sq, tb, LANES), lambda b, t: (b, t, 0)),
                  st_spec,
                  pl.BlockSpec((LANES, GLA_KW), const),
                  pl.BlockSpec((1, GLA_KW), const),
                  pl.BlockSpec((1, GLA_DV), const)],
        out_specs=[pl.BlockSpec((sq, tb, GLA_W), lambda b, t: (b, t, 0)), st_spec],
        out_shape=[jax.ShapeDtypeStruct((B, L, GLA_W), BF16), jax.ShapeDtypeStruct(st_all.shape, F32)],
        scratch_shapes=scratch,
        input_output_aliases={} if carry else {5: 1},
        compiler_params=_params(dimension_semantics=("parallel", "arbitrary")),
        name="gla_mixer",
    )(um, um, um, um, us, st_all, wa, ba, nrm)


SSD_GW = SSD_W // SSD_GROUPS
SSD_REP = SSD_HEADS // SSD_GROUPS
SSD_ROWS = 64


def _lane_expand(x, h0, width):
    return jnp.concatenate([jnp.broadcast_to(x[:, h0 + j:h0 + j + 1], (x.shape[0], width)) for j in range(SSD_REP)],
                           axis=1)


def _ssd_masks(C):
    ii = _iota2((SSD_ROWS, SSD_GW), 0)
    jj = _iota2((SSD_ROWS, SSD_GW), 1) & (SSD_ROWS - 1)
    sh = C.bit_length() - 1
    negc = jnp.where(((ii >> sh) == (jj >> sh)) & (jj <= ii), 0.0, -1e30)
    ri = _iota2((SSD_GW, SSD_GW), 0)
    ci = _iota2((SSD_GW, SSD_GW), 1)
    hsh = SSD_HEADDIM.bit_length() - 1
    bd = ((ri >> hsh) == (ci >> hsh)).astype(F32)
    return negc, bd


def _ssd_intra(act, G, GT, dt, gi, negc, bd):
    h0 = gi * SSD_REP
    xs = act[:, gi * SSD_GW:(gi + 1) * SSD_GW]
    bm = act[:, SSD_W + gi * SSD_STATE:SSD_W + (gi + 1) * SSD_STATE].astype(BF16)
    cm = act[:, SSD_W + (SSD_GROUPS + gi) * SSD_STATE:SSD_W + (SSD_GROUPS + gi + 1) * SSD_STATE].astype(BF16)
    gcc = _lane_expand(G, h0, SSD_HEADDIM)
    grc = jnp.concatenate([GT[h0 + j:h0 + j + 1, :] for j in range(SSD_REP)], axis=1)
    vcat = xs * _lane_expand(dt, h0, SSD_HEADDIM)
    cbm = _nt(cm, bm)
    acat = (jnp.concatenate([cbm] * SSD_REP, axis=1) * jnp.exp((gcc - grc) + negc)).astype(BF16)
    vbd = (jnp.concatenate([vcat] * SSD_REP, axis=0) * bd).astype(BF16)
    return _mm(acat, vbd), xs, vcat, gcc, bm, cm


def _ssd_finish(o, xs, z, dsk, nrm):
    y = (o + dsk * xs) * _silu(z)
    return y * lax.rsqrt(jnp.mean(y * y, axis=-1, keepdims=True) + EPS) * nrm


def _ssd_prompt_kernel(z_ref, xbc_ref, us_ref, cst_ref, st_ref, cw_ref, cb_ref, prm_ref, nrm_ref, o_ref, so_ref,
                       halo_ref, scat_ref, *, tb):
    t = pl.program_id(1)
    C = SSD_ROWS
    cat_heads = lambda f, gi: jnp.concatenate([f(gi * SSD_REP + j) for j in range(SSD_REP)], axis=1)

    @pl.when(t == 0)
    def _():
        for gi in range(SSD_GROUPS):
            scat_ref[gi] = cat_heads(lambda h: st_ref[h], gi)
        halo_ref[0:5, :] = jnp.zeros((5, SSD_XBC), F32)
        halo_ref[5:8, :] = cst_ref[...]

    negc, bd = _ssd_masks(C)
    tril = (_iota2((C, C), 1) <= _iota2((C, C), 0)).astype(F32)
    lane = _iota2((1, LANES), 1)
    hmask = lane < SSD_HEADS
    neg_a = -jnp.exp(prm_ref[0:1, :])
    bias = prm_ref[1:2, :]
    dsk = [_lane_expand(prm_ref[2:3, :], gi * SSD_REP, SSD_HEADDIM) for gi in range(SSD_GROUPS)]
    cw = cw_ref[...]
    cb = cb_ref[...]
    nrm = nrm_ref[...]
    groups = range(SSD_GROUPS)

    def chunk(c, carry):
        r0 = pl.multiple_of(c * C, C)
        x = xbc_ref[pl.ds(r0, C), :]
        prev = xbc_ref[pl.ds(pl.multiple_of(jnp.maximum(r0 - SUBLANES, 0), SUBLANES), SUBLANES), :]
        halo = jnp.where(c == 0, halo_ref[...], prev)
        act = _silu(_conv_rows(halo, x, cw, cb))
        us = us_ref[pl.ds(r0, C), :]
        dt = jnp.where(hmask, jax.nn.softplus(us + bias), 0.0)
        G = _mm_hi(tril, neg_a * dt)
        GT = G.T
        z = z_ref[pl.ds(r0, C), :]
        parts = [_ssd_intra(act, G, GT, dt, gi, negc, bd) for gi in groups]
        S = [scat_ref[gi] for gi in groups]
        inter = [_mm(parts[gi][5], S[gi].astype(BF16)) for gi in groups]
        for gi in groups:
            o_in, xs, vcat, gcc, bm, _ = parts[gi]
            glc = gcc[C - 1:C, :]
            w = (vcat * jnp.exp(glc - gcc)).astype(BF16)
            scat_ref[gi] = jnp.exp(glc) * S[gi] + _tn(bm, w)
            y = _ssd_finish(jnp.exp(gcc) * inter[gi] + o_in, xs, z[:, gi * SSD_GW:(gi + 1) * SSD_GW], dsk[gi],
                            nrm[:, gi * SSD_GW:(gi + 1) * SSD_GW])
            o_ref[pl.ds(r0, C), gi * SSD_GW:(gi + 1) * SSD_GW] = y.astype(o_ref.dtype)
        return carry

    lax.fori_loop(0, tb // C, chunk, 0)
    halo_ref[...] = xbc_ref[tb - SUBLANES:tb, :]

    @pl.when(t == pl.num_programs(1) - 1)
    def _():
        for gi in groups:
            s = scat_ref[gi]
            for j in range(SSD_REP):
                so_ref[gi * SSD_REP + j] = s[:, j * SSD_HEADDIM:(j + 1) * SSD_HEADDIM]


def _ssd_sample_kernel(xs_ref, bm_ref, cm_ref, z_ref, dt_ref, cxs_ref, cbm_ref, ccm_ref, st_ref, wxs_ref, wbm_ref,
                       wcm_ref, prm_ref, nrm_ref, o_ref, so_ref, acc_ref, *, ls, carry):
    st_ref, so_ref = _own_layer(st_ref, so_ref, carry)
    h = pl.program_id(0)
    hj = h % SSD_REP

    def conv(x_ref, c_ref, w_ref):
        xx = [c_ref[r] for r in range(CONV_W - 1)] + [x_ref[t] for t in range(ls)]
        return [_silu(sum(xx[t + s] * w_ref[s] for s in range(CONV_W)) + w_ref[CONV_W]) for t in range(ls)]

    xs = conv(xs_ref, cxs_ref, wxs_ref)
    bm = conv(bm_ref, cbm_ref, wbm_ref)
    cm = conv(cm_ref, ccm_ref, wcm_ref)
    row = lambda ref, k: ref[k, pl.ds(h, 1), :]
    dt = [jax.nn.softplus(dt_ref[t, pl.ds(h, 1), :] + row(prm_ref, 1)) for t in range(ls)]
    neg_a = -jnp.exp(row(prm_ref, 0))
    G = []
    for t in range(ls):
        G.append(neg_a * dt[t] + (G[-1] if G else 0.0))
    v = [xs[t] * dt[t] for t in range(ls)]
    y = []
    for t in range(ls):
        acc = None
        for u in range(t + 1):
            coef = jnp.sum(cm[t] * bm[u], axis=0, keepdims=True) * jnp.exp(G[t] - G[u])
            acc = coef * v[u] if acc is None else acc + coef * v[u]
        y.append(acc)
    eg = [jnp.exp(G[t]) for t in range(ls)]
    wk = [bm[u] * jnp.exp(G[ls - 1] - G[u]) for u in range(ls)]
    inter = [None] * ls
    for n in range(SSD_STATE):
        S = st_ref[0, n]
        for t in range(ls):
            term = cm[t][n:n + 1, :] * S
            inter[t] = term if inter[t] is None else inter[t] + term
        s_new = eg[ls - 1] * S
        for u in range(ls):
            s_new = s_new + wk[u][n:n + 1, :] * v[u]
        so_ref[0, n] = s_new
    dsk = row(prm_ref, 2)
    for t in range(ls):
        acc_ref[t, pl.ds(pl.multiple_of(hj * SSD_HEADDIM, SSD_HEADDIM), SSD_HEADDIM), :] = (
            (y[t] + eg[t] * inter[t] + dsk * xs[t]) * _silu(z_ref[t]))

    @pl.when(hj == SSD_REP - 1)
    def _():
        for t in range(ls):
            yg = acc_ref[t]
            r = lax.rsqrt(jnp.mean(yg * yg, axis=0, keepdims=True) + EPS)
            o_ref[t] = (yg * r * nrm_ref[...]).astype(o_ref.dtype)


def _ssd_mixer(um, us, cst_all, st_all, j, cw, cb, prm, nrm, *, tb, sample):
    B, L, _ = um.shape
    if sample:
        return _ssd_mixer_sample(um, us, cst_all, st_all, j, cw, cb, prm, nrm)
    const = lambda *_: (0, 0)
    out_shape = [jax.ShapeDtypeStruct((B, L, SSD_W), BF16), jax.ShapeDtypeStruct(st_all.shape, F32)]
    st_blk = (SSD_HEADS, SSD_STATE, SSD_HEADDIM)
    assert tb % SSD_ROWS == 0
    return pl.pallas_call(
        functools.partial(_ssd_prompt_kernel, tb=tb),
        grid=(B, L // tb),
        in_specs=[pl.BlockSpec((None, tb, SSD_W), lambda b, t: (b, t, CD_Z // SSD_W)),
                  pl.BlockSpec((None, tb, SSD_XBC), lambda b, t: (b, t, CD_XBC // SSD_XBC)),
                  pl.BlockSpec((None, tb, LANES), lambda b, t: (b, t, 0)),
                  pl.BlockSpec((None, None, CONV_W - 1, SSD_XBC), lambda b, t: (j, b, 0, 0)),
                  pl.BlockSpec((None, None) + st_blk, lambda b, t: (j, b, 0, 0, 0)),
                  pl.BlockSpec((CONV_W, SSD_XBC), const), pl.BlockSpec((1, SSD_XBC), const),
                  pl.BlockSpec((3, LANES), const), pl.BlockSpec((1, SSD_W), const)],
        out_specs=[pl.BlockSpec((None, tb, SSD_W), lambda b, t: (b, t, 0)),
                   pl.BlockSpec((None, None) + st_blk, lambda b, t: (j, b, 0, 0, 0))],
        out_shape=out_shape,
        scratch_shapes=[pltpu.VMEM((SUBLANES, SSD_XBC), F32), pltpu.VMEM((SSD_GROUPS, SSD_STATE, SSD_GW), F32)],
        input_output_aliases={4: 1},
        compiler_params=_params(dimension_semantics=("parallel", "arbitrary")), name="ssd_mixer",
    )(um, um, us, cst_all, st_all, cw, cb, prm, nrm)


def _ssd_mixer_sample(um, us, cst_all, st_all, j, cw, cb, prm, nrm):
    B, ls, _ = um.shape
    n_layers = st_all.shape[0]
    chan_major = lambda x: jnp.transpose(x, (1, 2, 0))
    xbc_t = chan_major(um[:, :, CD_XBC:CD_XBC + SSD_XBC])
    z_t = chan_major(um[:, :, CD_Z:CD_Z + SSD_W])
    dt_t = chan_major(us[:, :, :SSD_HEADS])
    cst_t = chan_major(cst_all[j])
    st_t = jnp.transpose(st_all, (0, 2, 3, 4, 1))
    lanes = lambda x: jnp.broadcast_to(x[..., None], x.shape + (B,))
    w_t = lanes(jnp.concatenate([cw, cb], axis=0))
    prm_t = lanes(prm[:, :SSD_HEADS])
    nrm_t = lanes(nrm.reshape(SSD_W))
    carry = (n_layers, j) if j == 0 else None
    hd = SSD_HEADDIM
    n_xs = SSD_W // hd
    grp = lambda h: h // SSD_REP
    win = lambda rows, f: pl.BlockSpec((rows, hd, B), lambda h: (0, f(h), 0))
    if carry:
        st_spec = pl.BlockSpec((n_layers, 1, SSD_STATE, hd, B), lambda h: (0, h, 0, 0, 0))
    else:
        st_spec = pl.BlockSpec((None, 1, SSD_STATE, hd, B), lambda h: (j, h, 0, 0, 0))
    y_t, st_new = pl.pallas_call(
        functools.partial(_ssd_sample_kernel, ls=ls, carry=carry),
        grid=(SSD_HEADS,),
        in_specs=[win(ls, lambda h: h), win(ls, lambda h: n_xs + grp(h)), win(ls, lambda h: n_xs + SSD_GROUPS + grp(h)),
                  win(ls, lambda h: h),
                  pl.BlockSpec((ls, SSD_HEADS, B), lambda h: (0, 0, 0)),
                  win(CONV_W - 1, lambda h: h), win(CONV_W - 1, lambda h: n_xs + grp(h)),
                  win(CONV_W - 1, lambda h: n_xs + SSD_GROUPS + grp(h)),
                  st_spec,
                  win(CONV_W + 1, lambda h: h), win(CONV_W + 1, lambda h: n_xs + grp(h)),
                  win(CONV_W + 1, lambda h: n_xs + SSD_GROUPS + grp(h)),
                  pl.BlockSpec((3, SSD_HEADS, B), lambda h: (0, 0, 0)),
                  pl.BlockSpec((SSD_GW, B), lambda h: (grp(h), 0))],
        out_specs=[pl.BlockSpec((ls, SSD_GW, B), lambda h: (0, grp(h), 0)), st_spec],
        out_shape=[jax.ShapeDtypeStruct((ls, SSD_W, B), F32), jax.ShapeDtypeStruct(st_t.shape, F32)],
        scratch_shapes=[pltpu.VMEM((ls, SSD_GW, B), F32)],
        input_output_aliases={} if carry else {8: 1},
        compiler_params=_params(dimension_semantics=("arbitrary",)), name="ssd_mixer",
    )(xbc_t, xbc_t, xbc_t, z_t, dt_t, cst_t, cst_t, cst_t, st_t, w_t, w_t, w_t, prm_t, nrm_t)
    y = jnp.transpose(y_t, (2, 0, 1)).astype(BF16)
    return y, jnp.transpose(st_new, (0, 4, 1, 2, 3))


def _lru_kernel(z_ref, x_ref, cst_ref, st_ref, cw_ref, cb_ref, wbd_ref, prm_ref, o_ref, so_ref, halo_ref, carry_ref,
                *, bb, tb, lvalid, ltotal):
    t = pl.program_id(1)

    @pl.when(t == 0)
    def _():
        carry_ref[...] = st_ref[...]
        halo_ref[:, 0:5, :] = jnp.zeros((bb, 5, LRU_W), F32)
        halo_ref[:, 5:8, :] = cst_ref[...]

    cw = cw_ref[...]
    cb = cb_ref[...]
    b_a = prm_ref[0:1, :]
    b_x = prm_ref[1:2, :]
    sp = jax.nn.softplus(-prm_ref[2:3, :])
    row = _iota2((tb, 1), 0)
    last_t, last_r = (lvalid - 1) // tb, (lvalid - 1) % tb

    def per_batch(bi, carry):
        x = x_ref[bi]
        xc = _conv_rows(halo_ref[bi], x, cw, cb)
        halo_ref[bi] = x[tb - SUBLANES:tb, :]
        xb = xc.astype(BF16)
        rs, is_ = [], []
        for cg in range(LRU_W // LANES):
            ri = _mm(xb[:, cg * LANES:(cg + 1) * LANES], wbd_ref[cg])
            rs.append(ri[:, :LANES])
            is_.append(ri[:, LANES:])
        r = jax.nn.sigmoid(jnp.concatenate(rs, axis=1) + b_a)
        i = jax.nn.sigmoid(jnp.concatenate(is_, axis=1) + b_x)
        log_a = (-LRU_C) * r * sp
        a = jnp.exp(log_a)
        om = 1.0 - jnp.exp(2.0 * log_a)
        b = jnp.where(om > 0.0, om * lax.rsqrt(om), 0.0) * (i * xc)
        d = 1
        while d < tb:
            m = row >= d
            a_s = pltpu.roll(a, d, axis=0)
            b_s = pltpu.roll(b, d, axis=0)
            b = jnp.where(m, a * b_s + b, b)
            a = jnp.where(m, a * a_s, a)
            d *= 2
        hs = a * carry_ref[bi] + b
        carry_ref[bi] = hs[tb - 1:tb, :]

        @pl.when(t == last_t)
        def _():
            so_ref[bi] = hs[last_r:last_r + 1, :]

        o_ref[bi] = (hs * _silu(z_ref[bi])).astype(o_ref.dtype)
        return carry

    lax.fori_loop(0, bb, per_batch, 0)


def _lru_sample_kernel(z_ref, x_ref, cst_ref, st_ref, cw_ref, cb_ref, wbd_ref, prm_ref, o_ref, so_ref, *, ls):
    cw = cw_ref[...]
    cb = cb_ref[...]
    b_a = prm_ref[0:1, :]
    b_x = prm_ref[1:2, :]
    sp = jax.nn.softplus(-prm_ref[2:3, :])
    xx = [cst_ref[r] for r in range(CONV_W - 1)] + [x_ref[t] for t in range(ls)]
    h = st_ref[...]
    for t in range(ls):
        xc = sum(xx[t + s] * cw[s:s + 1] for s in range(CONV_W)) + cb
        xb = xc.astype(BF16)
        ri = [_mm(xb[:, cg * LANES:(cg + 1) * LANES], wbd_ref[cg]) for cg in range(LRU_W // LANES)]
        r = jax.nn.sigmoid(jnp.concatenate([x[:, :LANES] for x in ri], axis=1) + b_a)
        i = jax.nn.sigmoid(jnp.concatenate([x[:, LANES:] for x in ri], axis=1) + b_x)
        log_a = (-LRU_C) * r * sp
        om = 1.0 - jnp.exp(2.0 * log_a)
        h = jnp.exp(log_a) * h + jnp.where(om > 0.0, om * lax.rsqrt(om), 0.0) * (i * xc)
        o_ref[t] = (h * _silu(z_ref[t])).astype(o_ref.dtype)
    so_ref[...] = h


def _lru_mixer_sample(um, cst_all, st_all, j, cw, cb, wbd, prm):
    B, ls, _ = um.shape
    time_major = lambda x: jnp.transpose(x, (1, 0, 2))
    full = lambda *shape: pl.BlockSpec(shape, lambda i: (0,) * len(shape))
    y_t, st_new = pl.pallas_call(
        functools.partial(_lru_sample_kernel, ls=ls),
        grid=(1,),
        in_specs=[full(ls, B, LRU_W), full(ls, B, LRU_W), full(CONV_W - 1, B, LRU_W), full(B, LRU_W),
                  full(CONV_W, LRU_W), full(1, LRU_W), full(LRU_W // LANES, LANES, 2 * LANES), full(3, LRU_W)],
        out_specs=[full(ls, B, LRU_W), full(B, LRU_W)],
        out_shape=[jax.ShapeDtypeStruct((ls, B, LRU_W), BF16), jax.ShapeDtypeStruct((B, LRU_W), F32)],
        compiler_params=_params(dimension_semantics=("arbitrary",)),
        name="lru_mixer",
    )(time_major(um[:, :, CD_ZLRU:CD_ZLRU + LRU_W]), time_major(um[:, :, CD_XLRU:CD_XLRU + LRU_W]),
      time_major(cst_all[j]), st_all[j].reshape(B, LRU_W), cw, cb, wbd, prm)
    return time_major(y_t), st_new.reshape(B, 1, LRU_W)


def _lru_mixer(um, cst_all, st_all, j, cw, cb, wbd, prm, *, tb, sample):
    B, L, _ = um.shape
    if sample:
        return _lru_mixer_sample(um, cst_all, st_all, j, cw, cb, wbd, prm)
    bb = 1
    kern = functools.partial(_lru_kernel, bb=bb, tb=tb, lvalid=L, ltotal=L)
    const = lambda b, t: (0, 0)
    return pl.pallas_call(
        kern,
        grid=(B // bb, L // tb),
        in_specs=[pl.BlockSpec((bb, tb, LRU_W), lambda b, t: (b, t, CD_ZLRU // LRU_W)),
                  pl.BlockSpec((bb, tb, LRU_W), lambda b, t: (b, t, CD_XLRU // LRU_W)),
                  pl.BlockSpec((None, bb, CONV_W - 1, LRU_W), lambda b, t: (j, b, 0, 0)),
                  pl.BlockSpec((None, bb, 1, LRU_W), lambda b, t: (j, b, 0, 0)),
                  pl.BlockSpec((CONV_W, LRU_W), const),
                  pl.BlockSpec((1, LRU_W), const),
                  pl.BlockSpec((LRU_W // LANES, LANES, 2 * LANES), lambda b, t: (0, 0, 0)),
                  pl.BlockSpec((3, LRU_W), const)],
        out_specs=[pl.BlockSpec((bb, tb, LRU_W), lambda b, t: (b, t, 0)),
                   pl.BlockSpec((bb, 1, LRU_W), lambda b, t: (b, 0, 0))],
        out_shape=[jax.ShapeDtypeStruct((B, L, LRU_W), BF16),
                   jax.ShapeDtypeStruct((B, 1, LRU_W), F32)],
        scratch_shapes=[pltpu.VMEM((bb, SUBLANES, LRU_W), F32), pltpu.VMEM((bb, 1, LRU_W), F32)],
        compiler_params=_params(dimension_semantics=("parallel", "arbitrary")),
        name="lru_mixer",
    )(um, um, cst_all, st_all, cw, cb, wbd, prm)


def _lane_row(*pieces):
    v = jnp.concatenate([p.reshape(-1) for p in pieces])
    return jnp.pad(v, (0, LANES - v.shape[0])).reshape(1, LANES)


RP_TILE = 512
RP_TAIL = 32


def _repack_kernel(a_ref, b_ref, o_ref, *, plan):
    r = pl.program_id(1)
    lo = 0
    for n_tiles, _, shift in plan:
        @pl.when((r >= lo) & (r < lo + n_tiles))
        def _(shift=shift):
            if shift == 0:
                o_ref[...] = a_ref[...].astype(BF16)
            else:
                o_ref[0:RP_TILE - shift, :] = a_ref[shift:, :].astype(BF16)
                o_ref[RP_TILE - shift:, :] = b_ref[0:shift, :].astype(BF16)
        lo += n_tiles


def _repack_wt(wt, plan):
    n, _, D = wt.shape
    total = sum(p[0] for p in plan)

    def src_tile(r):
        lo, idx = 0, 0
        for n_tiles, start, _ in plan:
            idx = jnp.where((r >= lo) & (r < lo + n_tiles), start // RP_TILE + (r - lo), idx)
            lo += n_tiles
        return idx

    for _, start, shift in plan:
        assert start % RP_TILE == 0 and shift % 16 == 0 and shift <= RP_TAIL
    return pl.pallas_call(
        functools.partial(_repack_kernel, plan=plan),
        grid=(n, total),
        in_specs=[pl.BlockSpec((None, RP_TILE, D), lambda l, r: (l, src_tile(r), 0)),
                  pl.BlockSpec((None, RP_TAIL, D), lambda l, r: (l, (src_tile(r) + 1) * (RP_TILE // RP_TAIL), 0))],
        out_specs=pl.BlockSpec((None, RP_TILE, D), lambda l, r: (l, r, 0)),
        out_shape=jax.ShapeDtypeStruct((n, total * RP_TILE, D), BF16),
        compiler_params=_params(dimension_semantics=("parallel", "parallel")),
        name="repack_w",
    )(wt, wt)


def _rows_t(w, sizes, order, pad_to=None):
    wt = jnp.swapaxes(w, 1, 2)
    offs = [sum(sizes[:i]) for i in range(len(sizes))]
    out = jnp.concatenate([wt[:, offs[i]:offs[i] + sizes[i], :] for i in order], axis=1)
    if pad_to is not None:
        out = jnp.pad(out, ((0, 0), (0, pad_to - out.shape[1]), (0, 0)))
    return out


def _prep_weights(w):
    n_even = w['ab_w_in'].shape[0]
    n_odd = w['cd_w_in'].shape[0]
    D = w['ab_w_in'].shape[1]
    P = {'ab': [], 'cd': []}
    P['ab_w_main'] = _repack_wt(jnp.swapaxes(w['ab_w_in'], 1, 2), [(6, 0, 0), (6, 3072, 16), (2, 6144, 32)])
    P['ab_w_small'] = _rows_t(w['ab_w_in'], AB_SIZES, (1, 2, 7), LANES)
    P['ab_w_out'] = w['ab_w_out'].astype(BF16)
    for j in range(n_even):
        P['ab'].append(dict(
            dn_cw=w['dn_conv_w'][j],
            dn_prm=jnp.concatenate([_lane_row(jnp.zeros((DN_HEADS,), F32), w['dn_a_log'][j]),
                                    _lane_row(jnp.zeros((DN_HEADS,), F32), w['dn_dt_bias'][j])], axis=0),
            dn_norm=w['dn_norm'][j].reshape(1, DN_DV),
            gla_wa=jnp.pad(w['gla_wa2'][j], ((2 * DN_HEADS, LANES - 2 * DN_HEADS - GLA_RANK), (0, 0))).astype(BF16),
            gla_ba=w['gla_ba'][j].reshape(1, GLA_KW),
            gla_norm=w['gla_norm'][j].reshape(1, GLA_DV),
        ))
    P['cd_w_main'] = _repack_wt(jnp.swapaxes(w['cd_w_in'], 1, 2), [(2, 0, 0), (4, 2560, 16), (3, 1024, 0)])
    P['cd_w_small'] = _rows_t(w['cd_w_in'], CD_SIZES, (2,), LANES)
    P['cd_w_out'] = w['cd_w_out'].astype(BF16)
    eye2 = jnp.eye(2, dtype=F32)
    for j in range(n_odd):
        wa = w['lru_wa'][j].reshape(LRU_W // LANES, 2, LRU_BS, LRU_BS)
        wx = w['lru_wx'][j].reshape(LRU_W // LANES, 2, LRU_BS, LRU_BS)
        bd = lambda m: jnp.einsum('gbcd,be->gbced', m, eye2).reshape(LRU_W // LANES, LANES, LANES)
        P['cd'].append(dict(
            ssd_cw=w['ssd_conv_w'][j],
            ssd_cb=w['ssd_conv_b'][j].reshape(1, SSD_XBC),
            ssd_prm=jnp.concatenate([_lane_row(w['ssd_a_log'][j]), _lane_row(w['ssd_dt_bias'][j]),
                                     _lane_row(w['ssd_d'][j])], axis=0),
            ssd_norm=w['ssd_norm'][j].reshape(1, SSD_W),
            lru_cw=w['lru_conv_w'][j],
            lru_cb=w['lru_conv_b'][j].reshape(1, LRU_W),
            lru_wbd=jnp.concatenate([bd(wa), bd(wx)], axis=2).astype(BF16),
            lru_prm=jnp.stack([w['lru_ba'][j], w['lru_bx'][j], w['lru_lambda'][j]], axis=0),
        ))
    P['norm_g'] = w['norm_g'].reshape(-1, 1, D)
    P['ple_norm'] = w['ple_norm'].reshape(-1, 1, D)
    P['ple_w_gate'] = w['ple_w_gate'].astype(BF16)
    P['ple_w_proj'] = w['ple_w_proj'].astype(BF16)
    P['final_norm'] = w['final_norm'].reshape(1, D)
    return P


def _trunk(x, p, states, P, *, lpad, bb, tb, c_dn, c_gla):
    B, L, D = x.shape
    lvalid = L
    sample = lpad > 0
    pad_t = (lambda a: jnp.pad(a, ((0, 0), (0, lpad - L), (0, 0)))) if sample else (lambda a: a)
    depth = P['norm_g'].shape[0]
    st_dn, st_dnc, st_gla, st_ssd, st_ssdc, st_lru, st_lruc = states
    st_lru = st_lru.reshape(st_lru.shape[0], B, 1, LRU_W)
    dnc_new, ssdc_new, lru_new, lruc_new = [], [], [], []
    T = B * L
    h = x.reshape(T, D)
    p_all = p.reshape(depth, T, PLE_DIM)
    lo = lvalid - (CONV_W - 1)
    for li in range(depth):
        j = li // 2
        if li % 2 == 0:
            W = P['ab'][j]
            um, us = _inproj(h, P['norm_g'], li, P['ab_w_main'], P['ab_w_small'], j)
            um = um.reshape(B, L, AB_MAIN)
            us = us.reshape(B, L, LANES)
            um_p, us_p = pad_t(um), pad_t(us)
            oa, st_dn = _dn_mixer(um_p, us_p, st_dnc, st_dn, j, W['dn_cw'], W['dn_prm'], W['dn_norm'],
                                  bb=bb, tb=tb, C=c_dn, lvalid=lvalid)
            ob, st_gla = _gla_mixer(um_p, us_p, st_gla, j, W['gla_wa'], W['gla_ba'], W['gla_norm'],
                                    bb=bb, tb=tb, C=c_gla, lvalid=lvalid)
            oa, ob = oa[:, :L], ob[:, :L]
            dnc_new.append(um[:, lo:lvalid, AB_QKV:AB_QKV + DN_QKV])
        else:
            W = P['cd'][j]
            um, us = _inproj(h, P['norm_g'], li, P['cd_w_main'], P['cd_w_small'], j)
            um = um.reshape(B, L, CD_MAIN)
            us = us.reshape(B, L, LANES)
            oa, st_ssd = _ssd_mixer(um, us, st_ssdc, st_ssd, j, W['ssd_cw'], W['ssd_cb'], W['ssd_prm'],
                                    W['ssd_norm'], tb=tb, sample=sample)
            ob, lru_j = _lru_mixer(um, st_lruc, st_lru, j, W['lru_cw'], W['lru_cb'],
                                   W['lru_wbd'], W['lru_prm'], tb=tb, sample=sample)
            ssdc_new.append(um[:, lo:lvalid, CD_XBC:CD_XBC + SSD_XBC])
            lru_new.append(lru_j.reshape(B, LRU_W))
            lruc_new.append(um[:, lo:lvalid, CD_XLRU:CD_XLRU + LRU_W])
        h = _outproj(oa.reshape(T, -1), ob.reshape(T, -1), h, p_all, li, P['ab_w_out' if li % 2 == 0 else 'cd_w_out'], j,
                     P['ple_norm'], P['ple_w_gate'], P['ple_w_proj'], P['final_norm'], final=(li == depth - 1))
    new_states = [st_dn, jnp.stack(dnc_new), st_gla, st_ssd, jnp.stack(ssdc_new), jnp.stack(lru_new),
                  jnp.stack(lruc_new)]
    return h.reshape(B, L, D), new_states


def _pick_tb(L):
    tb = min(L, 512)
    assert L % tb == 0
    return tb


def kernel(x_prompt, x_sample, state_dn, state_dn_conv, state_gla, state_ssd, state_ssd_conv, state_lru, state_lru_conv, p_prompt, p_sample, norm_g, final_norm, ab_w_in, dn_conv_w, dn_a_log, dn_dt_bias, dn_norm, gla_wa2, gla_ba, gla_norm, ab_w_out, cd_w_in, ssd_conv_w, ssd_conv_b, ssd_a_log, ssd_dt_bias, ssd_d, ssd_norm, lru_conv_w, lru_conv_b, lru_wa, lru_ba, lru_wx, lru_bx, lru_lambda, cd_w_out, ple_w_proj, ple_norm, ple_w_gate):
    w = {
        'norm_g': norm_g, 'final_norm': final_norm,
        'ab_w_in': ab_w_in, 'dn_conv_w': dn_conv_w, 'dn_a_log': dn_a_log, 'dn_dt_bias': dn_dt_bias,
        'dn_norm': dn_norm, 'gla_wa2': gla_wa2, 'gla_ba': gla_ba, 'gla_norm': gla_norm, 'ab_w_out': ab_w_out,
        'cd_w_in': cd_w_in, 'ssd_conv_w': ssd_conv_w, 'ssd_conv_b': ssd_conv_b, 'ssd_a_log': ssd_a_log,
        'ssd_dt_bias': ssd_dt_bias, 'ssd_d': ssd_d, 'ssd_norm': ssd_norm,
        'lru_conv_w': lru_conv_w, 'lru_conv_b': lru_conv_b, 'lru_wa': lru_wa, 'lru_ba': lru_ba,
        'lru_wx': lru_wx, 'lru_bx': lru_bx, 'lru_lambda': lru_lambda, 'cd_w_out': cd_w_out,
        'ple_w_proj': ple_w_proj, 'ple_norm': ple_norm, 'ple_w_gate': ple_w_gate,
    }
    P = _prep_weights(w)
    n_even, n_odd = ab_w_in.shape[0], cd_w_in.shape[0]

    Bp, Lp, _ = x_prompt.shape
    assert Lp >= CONV_W - 1 and Lp % 64 == 0
    zero_states = (
        jnp.zeros((n_even, Bp, DN_HEADS, DN_DK, DN_DV), F32),
        jnp.zeros((n_even, Bp, CONV_W - 1, DN_QKV), F32),
        jnp.zeros((n_even, Bp, GLA_HEADS, GLA_DK, GLA_DV), F32),
        jnp.zeros((n_odd, Bp, SSD_HEADS, SSD_STATE, SSD_HEADDIM), F32),
        jnp.zeros((n_odd, Bp, CONV_W - 1, SSD_XBC), F32),
        jnp.zeros((n_odd, Bp, LRU_W), F32),
        jnp.zeros((n_odd, Bp, CONV_W - 1, LRU_W), F32),
    )
    y_p, st_p = _trunk(x_prompt, p_prompt, zero_states, P, lpad=0, bb=1, tb=_pick_tb(Lp), c_dn=64, c_gla=16)

    Bs, Ls, _ = x_sample.shape
    assert CONV_W - 1 <= Ls < SUBLANES
    sample_states = (state_dn, state_dn_conv, state_gla, state_ssd, state_ssd_conv, state_lru, state_lru_conv)
    bb_s = 8 if Bs % 8 == 0 else 1
    y_s, st_s = _trunk(x_sample, p_sample, sample_states, P, lpad=SUBLANES, bb=bb_s, tb=SUBLANES,
                       c_dn=SUBLANES, c_gla=SUBLANES)
    return (y_p, y_s, *st_p, *st_s)
```

```python
import functools

import jax
import jax.numpy as jnp
from jax import lax
from jax.experimental import pallas as pl
from jax.experimental.pallas import tpu as pltpu

F32 = jnp.float32
BF16 = jnp.bfloat16
HI = lax.Precision.HIGHEST

EPS = 1e-6
LOG2E = 1.4426950408889634
CONV_W = 4
PLE_DIM = 256
DN_HEADS, DN_DK, DN_DV = 8, 128, 128
DN_W = DN_HEADS * DN_DV
DN_QKV = 2 * DN_HEADS * DN_DK + DN_W
GLA_HEADS, GLA_DK, GLA_DV = 4, 128, 256
GLA_KW = GLA_HEADS * GLA_DK
GLA_W = GLA_HEADS * GLA_DV
GLA_RANK = 16
GLA_GATE_NORM = 16.0
AB_SIZES = (DN_QKV, DN_HEADS, DN_HEADS, DN_W, GLA_KW, GLA_KW, GLA_W, GLA_RANK, GLA_W)
SSD_HEADS, SSD_HEADDIM, SSD_GROUPS, SSD_STATE = 16, 64, 4, 64
SSD_W = SSD_HEADS * SSD_HEADDIM
SSD_XBC = SSD_W + 2 * SSD_GROUPS * SSD_STATE
LRU_W, LRU_BLOCKS = 1024, 16
LRU_BS = LRU_W // LRU_BLOCKS
LRU_C = 8.0
CD_SIZES = (SSD_W, SSD_XBC, SSD_HEADS, LRU_W, LRU_W)

LANES = 128
SUBLANES = 8
VMEM_LIMIT = 52 * 1024 * 1024

AB_QKV, AB_ZDN, AB_GQ, AB_GK, AB_GV, AB_ZGLA = 0, 3072, 4096, 4608, 5120, 6144
AB_MAIN = 7168
CD_Z, CD_ZLRU, CD_XLRU, CD_XBC = 0, 1024, 2048, 3072
CD_MAIN = 4608


def _silu(x):
    return x * jax.nn.sigmoid(x)


def _nt(a, b):
    return lax.dot_general(a, b, (((1,), (1,)), ((), ())), preferred_element_type=F32)


def _tn(a, b):
    return lax.dot_general(a, b, (((0,), (0,)), ((), ())), preferred_element_type=F32)


def _mm(a, b):
    return jnp.dot(a, b, preferred_element_type=F32)


def _mm_hi(a, b):
    return jnp.dot(a, b, preferred_element_type=F32, precision=HI)


def _iota2(shape, dim):
    return lax.broadcasted_iota(jnp.int32, shape, dim)


def _conv_rows(halo, x, w, bias=None):
    cat = jnp.concatenate([halo, x], axis=0)
    out = x * w[3:4]
    for s in (1, 2, 3):
        out = out + pltpu.roll(cat, s, axis=0)[SUBLANES:] * w[3 - s:4 - s]
    if bias is not None:
        out = out + bias
    return out


def _params(**kw):
    return pltpu.CompilerParams(vmem_limit_bytes=VMEM_LIMIT, **kw)


def _own_layer(st_ref, so_ref, carry):
    if carry is None:
        return st_ref, so_ref
    n, jl = carry
    for l in range(n):
        if l != jl:
            so_ref[l] = st_ref[l]
    return st_ref.at[jl], so_ref.at[jl]


def _state_specs(st_all, j, bb, carry):
    tail = st_all.shape[2:]
    zeros = (0,) * len(tail)
    if carry:
        return pl.BlockSpec((st_all.shape[0], bb) + tail, lambda b, *_: (0, b) + zeros)
    return pl.BlockSpec((None, bb) + tail, lambda b, *_: (j, b) + zeros)


def _inproj_kernel(h_ref, g_ref, wm_ref, ws_ref, um_ref, us_ref, xn_ref):
    @pl.when(pl.program_id(1) == 0)
    def _():
        x = h_ref[...]
        r = lax.rsqrt(jnp.mean(x * x, axis=-1, keepdims=True) + EPS)
        xn = (x * r * g_ref[...]).astype(BF16)
        xn_ref[...] = xn
        us_ref[...] = _nt(xn, ws_ref[...].astype(BF16))

    um_ref[...] = _nt(xn_ref[...], wm_ref[...])


def _inproj(h, g_all, li, wm_all, ws_all, j):
    T, D = h.shape
    Nm = wm_all.shape[1]
    tm = min(T, 1024)
    tn = 1024 if Nm % 1024 == 0 else 1536
    return pl.pallas_call(
        _inproj_kernel,
        grid=(T // tm, Nm // tn),
        in_specs=[pl.BlockSpec((tm, D), lambda i, c: (i, 0)),
                  pl.BlockSpec((None, 1, D), lambda i, c: (li, 0, 0)),
                  pl.BlockSpec((None, tn, D), lambda i, c: (j, c, 0)),
                  pl.BlockSpec((None, LANES, D), lambda i, c: (j, 0, 0))],
        out_specs=[pl.BlockSpec((tm, tn), lambda i, c: (i, c)),
                   pl.BlockSpec((tm, LANES), lambda i, c: (i, 0))],
        out_shape=[jax.ShapeDtypeStruct((T, Nm), F32), jax.ShapeDtypeStruct((T, LANES), F32)],
        scratch_shapes=[pltpu.VMEM((tm, D), BF16)],
        compiler_params=_params(dimension_semantics=("parallel", "arbitrary")),
        name="inproj",
    )(h, g_all, wm_all, ws_all)


def _outproj_kernel(oa_ref, ob_ref, h_ref, p_ref, wa_ref, wb_ref, pg_ref, wg_ref, wp_ref, fg_ref, out_ref, *, final):
    h1 = h_ref[...] + _mm(oa_ref[...], wa_ref[...]) + _mm(ob_ref[...], wb_ref[...])
    r = lax.rsqrt(jnp.mean(h1 * h1, axis=-1, keepdims=True) + EPS)
    hn = (h1 * r * pg_ref[...]).astype(BF16)
    gate = jax.nn.sigmoid(_mm(hn, wg_ref[...]))
    pp = _mm(p_ref[...].astype(BF16), wp_ref[...])
    h2 = h1 + gate * pp
    if final:
        r2 = lax.rsqrt(jnp.mean(h2 * h2, axis=-1, keepdims=True) + EPS)
        h2 = h2 * r2 * fg_ref[...]
    out_ref[...] = h2


def _outproj(oa, ob, h, p_all, li, w_out_all, j, pg_all, wg_all, wp_all, fg, final):
    T, D = h.shape
    Wa, Wb = oa.shape[1], ob.shape[1]
    assert Wa == Wb
    tm = min(T, 512)
    resident = lambda shape, idx: pl.BlockSpec(shape, lambda i: idx, pipeline_mode=pl.Buffered(1))
    return pl.pallas_call(
        functools.partial(_outproj_kernel, final=final),
        grid=(T // tm,),
        in_specs=[pl.BlockSpec((tm, Wa), lambda i: (i, 0)),
                  pl.BlockSpec((tm, Wb), lambda i: (i, 0)),
                  pl.BlockSpec((tm, D), lambda i: (i, 0)),
                  pl.BlockSpec((None, tm, PLE_DIM), lambda i: (li, i, 0)),
                  resident((None, Wa, D), (j, 0, 0)),
                  resident((None, Wb, D), (j, 1, 0)),
                  resident((None, 1, D), (li, 0, 0)),
                  resident((None, D, D), (li, 0, 0)),
                  resident((None, PLE_DIM, D), (li, 0, 0)),
                  resident((1, D), (0, 0))],
        out_specs=pl.BlockSpec((tm, D), lambda i: (i, 0)),
        out_shape=jax.ShapeDtypeStruct((T, D), F32),
        compiler_params=_params(dimension_semantics=("parallel",)),
        name="outproj",
    )(oa, ob, h, p_all, w_out_all, w_out_all, pg_all, wg_all, wp_all, fg)


R_BD = 128


def _bd_masks(C):
    ii = _iota2((R_BD, R_BD), 0)
    jj = _iota2((R_BD, R_BD), 1)
    sh = C.bit_length() - 1
    lower = ((ii >> sh) == (jj >> sh)) & (jj <= ii)
    neg = jnp.where(lower, 0.0, -1e30)
    eye = (ii == jj).astype(F32)
    return neg, 1.0 - eye, eye


def _dn_groups(Q, K, V, beta, gc, gr, neg, offdiag, eye, C):
    n = range(len(Q))
    Kb = [K[g].astype(BF16) for g in n]
    KK = [_nt(Kb[g], Kb[g]) for g in n]
    QK = [_nt(Q[g].astype(BF16), Kb[g]) for g in n]
    dm = [jnp.exp((gc[g] - gr[g]) + neg) for g in n]
    A = [(beta[g] * KK[g]) * (dm[g] * offdiag) for g in n]
    T = [eye - A[g] for g in n]
    X = A
    p = 2
    while p < C:
        Xb = [X[g].astype(BF16) for g in n]
        X = [_mm(Xb[g], Xb[g]) for g in n]
        T = [T[g] + _mm(T[g].astype(BF16), X[g].astype(BF16)) for g in n]
        p *= 2
    eg = [jnp.exp(gc[g]) for g in n]
    rhs = [jnp.concatenate([beta[g] * V[g], (beta[g] * eg[g]) * K[g]], axis=1).astype(BF16) for g in n]
    sol = [_mm(T[g].astype(BF16), rhs[g]) for g in n]
    return ([sol[g][:, :DN_DV] for g in n], [sol[g][:, DN_DV:] for g in n],
            [QK[g] * dm[g] for g in n], [Q[g] * eg[g] for g in n])


def _l2n(x):
    return x * lax.rsqrt(jnp.sum(x * x, axis=-1, keepdims=True) + EPS)


def _dn_apply(S, u0, wk, qkd, qg, K, gl, gc, C):
    n = range(len(S))
    nu = len(S[0])
    blk = lambda x, i: x[i * C:(i + 1) * C]
    r = [[None] * nu for _ in n]
    for i in range(nu):
        for g in n:
            lhs = jnp.concatenate([blk(wk[g], i), blk(qg[g], i)], axis=0).astype(BF16)
            r[g][i] = _mm(lhs, S[g][i].astype(BF16))
    Ub = [jnp.concatenate([blk(u0[g], i) - r[g][i][:C] for i in range(nu)], axis=0).astype(BF16) for g in n]
    O = [jnp.concatenate([r[g][i][C:] for i in range(nu)], axis=0) + _mm(qkd[g].astype(BF16), Ub[g]) for g in n]
    kend = [(K[g] * jnp.exp(gl[g] - gc[g])).astype(BF16) for g in n]
    egl = [jnp.exp(gl[g]) for g in n]
    S_new = [[None] * nu for _ in n]
    for i in range(nu):
        for g in n:
            S_new[g][i] = egl[g][i * C:i * C + 1, :] * S[g][i] + _tn(blk(kend[g], i), blk(Ub[g], i))
    return O, S_new


def _dn_prompt_kernel(qkv_ref, z_ref, us_ref, cst_ref, st_ref, cw_ref, prm_ref, nrm_ref, o_ref, so_ref, halo_ref,
                      *, tb, C, nc):
    t = pl.program_id(1)
    hg = R_BD // C
    CR = nc * C

    @pl.when(t == 0)
    def _():
        so_ref[...] = st_ref[...]
        halo_ref[0:5, :] = jnp.zeros((5, DN_QKV), F32)
        halo_ref[5:8, :] = cst_ref[...]

    neg, offdiag, eye = _bd_masks(C)
    ii = _iota2((CR, CR), 0)
    jj = _iota2((CR, CR), 1)
    sh = C.bit_length() - 1
    tril = (((ii >> sh) == (jj >> sh)) & (jj <= ii)).astype(F32)
    lane = _iota2((1, LANES), 1)
    gmask = (lane >= DN_HEADS) & (lane < 2 * DN_HEADS)
    neg_a = -jnp.exp(prm_ref[0:1, :])
    bias = prm_ref[1:2, :]
    cw = cw_ref[...]
    nrm = nrm_ref[...]
    head_sets = [range(gi * hg, (gi + 1) * hg) for gi in range(DN_HEADS // hg)]
    groups = [(ci, hs) for ci in range(nc) for hs in head_sets]
    rws = lambda ci: slice(ci * C, (ci + 1) * C)

    def step(cp, carry):
        r0 = pl.multiple_of(cp * CR, CR)
        x = qkv_ref[pl.ds(r0, CR), :]
        prev = qkv_ref[pl.ds(pl.multiple_of(jnp.maximum(r0 - SUBLANES, 0), SUBLANES), SUBLANES), :]
        halo = jnp.where(cp == 0, halo_ref[...], prev)
        act = _silu(_conv_rows(halo, x, cw))
        us = us_ref[pl.ds(r0, CR), :]
        g = jnp.where(gmask, neg_a * jax.nn.softplus(us + bias), 0.0)
        beta_all = jax.nn.sigmoid(us)
        G = _mm_hi(tril, g)
        GT = G.T
        z = z_ref[pl.ds(r0, CR), :]
        cat0 = lambda f: [jnp.concatenate([f(ci, h) for h in hs], axis=0) for ci, hs in groups]
        Q = [_l2n(a) * (DN_DK ** -0.5) for a in cat0(lambda ci, h: act[rws(ci), h * DN_DK:(h + 1) * DN_DK])]
        K = [_l2n(a) for a in cat0(lambda ci, h: act[rws(ci), DN_W + h * DN_DK:DN_W + (h + 1) * DN_DK])]
        V = cat0(lambda ci, h: act[rws(ci), 2 * DN_W + h * DN_DV:2 * DN_W + (h + 1) * DN_DV])
        beta = cat0(lambda ci, h: beta_all[rws(ci), h:h + 1])
        gc = cat0(lambda ci, h: G[rws(ci), DN_HEADS + h:DN_HEADS + h + 1])
        gl = cat0(lambda ci, h: jnp.broadcast_to(
            G[(ci + 1) * C - 1:(ci + 1) * C, DN_HEADS + h:DN_HEADS + h + 1], (C, 1)))
        gr = [jnp.concatenate([GT[DN_HEADS + h:DN_HEADS + h + 1, rws(ci)] for h in hs], axis=1)
              for ci, hs in groups]
        u0, wk, qkd, qg = _dn_groups(Q, K, V, beta, gc, gr, neg, offdiag, eye, C)
        Z = cat0(lambda ci, h: z[rws(ci), h * DN_DV:(h + 1) * DN_DV])
        S = [[so_ref[h] for h in hs] for hs in head_sets]
        ng = len(head_sets)
        for ci in range(nc):
            pick = lambda xs: xs[ci * ng:(ci + 1) * ng]
            O, S = _dn_apply(S, pick(u0), pick(wk), pick(qkd), pick(qg), pick(K), pick(gl), pick(gc), C)
            for gi, hs in enumerate(head_sets):
                o = O[gi] * lax.rsqrt(jnp.mean(O[gi] * O[gi], axis=-1, keepdims=True) + EPS) * nrm
                out = (o * _silu(Z[ci * ng + gi])).astype(o_ref.dtype)
                for i, h in enumerate(hs):
                    o_ref[pl.ds(r0 + ci * C, C), h * DN_DV:(h + 1) * DN_DV] = out[i * C:(i + 1) * C]
        for gi, hs in enumerate(head_sets):
            for i, h in enumerate(hs):
                so_ref[h] = S[gi][i]
        return carry

    lax.fori_loop(0, tb // CR, step, 0)
    halo_ref[...] = qkv_ref[tb - SUBLANES:tb, :]


def _dn_sample_kernel(qkv_ref, z_ref, us_ref, cst_ref, st_ref, cw_ref, prm_ref, nrm_ref, o_ref, so_ref, halo_ref,
                      *, bb, lvalid, carry):
    st_ref, so_ref = _own_layer(st_ref, so_ref, carry)
    C = SUBLANES
    nb = R_BD // (DN_HEADS * C)
    rows_all = bb * C

    halo_ref[:, 0:5, :] = jnp.zeros((bb, 5, DN_QKV), F32)
    halo_ref[:, 5:8, :] = cst_ref[...]
    cw = cw_ref[...]
    nrm = nrm_ref[...]
    acts = [_silu(_conv_rows(halo_ref[b], qkv_ref[b], cw)) for b in range(bb)]

    neg, offdiag, eye = _bd_masks(C)
    lane = _iota2((1, LANES), 1)
    gmask = (lane >= DN_HEADS) & (lane < 2 * DN_HEADS)
    neg_a = -jnp.exp(prm_ref[0:1, :])
    bias = prm_ref[1:2, :]
    us = us_ref[...].reshape(rows_all, LANES)
    valid = (_iota2((rows_all, 1), 0) & (C - 1)) < lvalid
    g = jnp.where(gmask & valid, neg_a * jax.nn.softplus(us + bias), 0.0)
    beta_all = jnp.where(valid, jax.nn.sigmoid(us), 0.0)
    ii = _iota2((rows_all, rows_all), 0)
    jj = _iota2((rows_all, rows_all), 1)
    sh = C.bit_length() - 1
    same = (ii >> sh) == (jj >> sh)
    G_all = _mm_hi((same & (jj <= ii)).astype(F32), g)
    GL_all = _mm_hi(same.astype(F32), g)
    GT_all = G_all.T

    groups = [[(h, gi * nb + bl) for h in range(DN_HEADS) for bl in range(nb)]
              for gi in range(bb // nb)]
    rows = [slice(gi * nb * C, (gi + 1) * nb * C) for gi in range(bb // nb)]
    cat0 = lambda f: [jnp.concatenate([f(h, b) for h, b in us_], axis=0) for us_ in groups]
    cath = lambda f: [jnp.concatenate([f(h, rs) for h in range(DN_HEADS)], axis=0) for rs in rows]
    Q = [_l2n(x) * (DN_DK ** -0.5) for x in cat0(lambda h, b: acts[b][:, h * DN_DK:(h + 1) * DN_DK])]
    K = [_l2n(x) for x in cat0(lambda h, b: acts[b][:, DN_W + h * DN_DK:DN_W + (h + 1) * DN_DK])]
    V = cat0(lambda h, b: acts[b][:, 2 * DN_W + h * DN_DV:2 * DN_W + (h + 1) * DN_DV])
    beta = cath(lambda h, rs: beta_all[rs, h:h + 1])
    gc = cath(lambda h, rs: G_all[rs, DN_HEADS + h:DN_HEADS + h + 1])
    gl = cath(lambda h, rs: GL_all[rs, DN_HEADS + h:DN_HEADS + h + 1])
    gr = [jnp.concatenate([GT_all[DN_HEADS + h:DN_HEADS + h + 1, rs] for h in range(DN_HEADS)], axis=1)
          for rs in rows]
    u0, wk, qkd, qg = _dn_groups(Q, K, V, beta, gc, gr, neg, offdiag, eye, C)
    S = [[st_ref[b, h] for h, b in us_] for us_ in groups]
    O, S_new = _dn_apply(S, u0, wk, qkd, qg, K, gl, gc, C)
    Z = cat0(lambda h, b: z_ref[b][:, h * DN_DV:(h + 1) * DN_DV])
    for g, us_ in enumerate(groups):
        o = O[g] * lax.rsqrt(jnp.mean(O[g] * O[g], axis=-1, keepdims=True) + EPS) * nrm
        out = (o * _silu(Z[g])).astype(o_ref.dtype)
        for i, (h, b) in enumerate(us_):
            so_ref[b, h] = S_new[g][i]
            o_ref[b, :, h * DN_DV:(h + 1) * DN_DV] = out[i * C:(i + 1) * C]


def _dn_mixer(um, us, cst_all, st_all, j, cw, prm, nrm, *, bb, tb, C, lvalid):
    B, L, _ = um.shape
    const = lambda *_: (0, 0)
    out_shape = [jax.ShapeDtypeStruct((B, L, DN_W), BF16), jax.ShapeDtypeStruct(st_all.shape, F32)]
    carry = None
    if lvalid == L:
        assert bb == 1
        kern = functools.partial(_dn_prompt_kernel, tb=tb, C=C, nc=4)
        grid = (B, L // tb)
        in_specs = [pl.BlockSpec((None, tb, DN_QKV), lambda b, t: (b, t, AB_QKV // DN_QKV)),
                    pl.BlockSpec((None, tb, DN_W), lambda b, t: (b, t, AB_ZDN // DN_W)),
                    pl.BlockSpec((None, tb, LANES), lambda b, t: (b, t, 0)),
                    pl.BlockSpec((None, None, CONV_W - 1, DN_QKV), lambda b, t: (j, b, 0, 0)),
                    pl.BlockSpec((None, None, DN_HEADS, DN_DK, DN_DV), lambda b, t: (j, b, 0, 0, 0)),
                    pl.BlockSpec((CONV_W, DN_QKV), const),
                    pl.BlockSpec((2, LANES), const),
                    pl.BlockSpec((1, DN_DV), const)]
        out_specs = [pl.BlockSpec((None, tb, DN_W), lambda b, t: (b, t, 0)),
                     pl.BlockSpec((None, None, DN_HEADS, DN_DK, DN_DV), lambda b, t: (j, b, 0, 0, 0))]
        scratch = [pltpu.VMEM((SUBLANES, DN_QKV), F32)]
        sem = ("parallel", "arbitrary")
    else:
        assert L == SUBLANES and C == SUBLANES
        carry = (st_all.shape[0], j) if j == 0 else None
        kern = functools.partial(_dn_sample_kernel, bb=bb, lvalid=lvalid, carry=carry)
        grid = (B // bb,)
        in_specs = [pl.BlockSpec((bb, L, DN_QKV), lambda b: (b, 0, AB_QKV // DN_QKV)),
                    pl.BlockSpec((bb, L, DN_W), lambda b: (b, 0, AB_ZDN // DN_W)),
                    pl.BlockSpec((bb, L, LANES), lambda b: (b, 0, 0)),
                    pl.BlockSpec((None, bb, CONV_W - 1, DN_QKV), lambda b: (j, b, 0, 0)),
                    _state_specs(st_all, j, bb, carry),
                    pl.BlockSpec((CONV_W, DN_QKV), const),
                    pl.BlockSpec((2, LANES), const),
                    pl.BlockSpec((1, DN_DV), const)]
        out_specs = [pl.BlockSpec((bb, L, DN_W), lambda b: (b, 0, 0)), _state_specs(st_all, j, bb, carry)]
        scratch = [pltpu.VMEM((bb, SUBLANES, DN_QKV), F32)]
        sem = ("parallel",)
    return pl.pallas_call(
        kern, grid=grid, in_specs=in_specs, out_specs=out_specs, out_shape=out_shape,
        scratch_shapes=scratch, input_output_aliases={} if carry else {4: 1},
        compiler_params=_params(dimension_semantics=sem), name="dn_mixer",
    )(um, um, us, cst_all, st_all, cw, prm, nrm)


def _gla_sample_kernel(q_ref, k_ref, v_ref, z_ref, us_ref, st_ref, wa_ref, ba_ref, nrm_ref, o_ref, so_ref,
                       *, bb, lvalid, carry):
    st_ref, so_ref = _own_layer(st_ref, so_ref, carry)
    C = SUBLANES
    R = bb * C
    ii = _iota2((C, C), 0)
    jj = _iota2((C, C), 1)
    nrm = nrm_ref[...]
    valid = (_iota2((R, 1), 0) & (C - 1)) < lvalid
    us = us_ref[...].reshape(R, LANES)
    glog = jax.nn.log_sigmoid(_mm(us.astype(BF16), wa_ref[...]) + ba_ref[...]) * (1.0 / GLA_GATE_NORM)
    glog = jnp.where(valid, glog, 0.0)
    ri = _iota2((R, R), 0)
    ci = _iota2((R, R), 1)
    sh = C.bit_length() - 1
    G = _mm_hi((((ri >> sh) == (ci >> sh)) & (ci <= ri)).astype(F32), glog)
    q = q_ref[...].reshape(R, GLA_KW) * (GLA_DK ** -0.5)
    k = jnp.where(valid, k_ref[...].reshape(R, GLA_KW), 0.0)
    v = jnp.where(valid, v_ref[...].reshape(R, GLA_W), 0.0)
    z = z_ref[...].reshape(R, GLA_W)
    units = [(b, h) for b in range(bb) for h in range(GLA_HEADS)]
    U = range(len(units))
    tile = lambda x, b, h, w: x[b * C:(b + 1) * C, h * w:(h + 1) * w]
    Gu = [tile(G, b, h, GLA_DK) for b, h in units]
    qu = [tile(q, b, h, GLA_DK) for b, h in units]
    ku = [tile(k, b, h, GLA_DK) for b, h in units]
    vu = [tile(v, b, h, GLA_DV).astype(BF16) for b, h in units]
    AT = [jnp.zeros((C, C), F32) for _ in U]
    for i in range(C):
        for u in U:
            e = jnp.exp(jnp.minimum(Gu[u][i:i + 1, :] - Gu[u], 0.0))
            col = jnp.sum(ku[u] * e * qu[u][i:i + 1, :], axis=1, keepdims=True)
            AT[u] = AT[u] + jnp.where(jj == i, col, 0.0)
    AT = [jnp.where(ii <= jj, AT[u], 0.0).astype(BF16) for u in U]
    S = [st_ref[b, h] for b, h in units]
    o = [_mm((qu[u] * jnp.exp(Gu[u])).astype(BF16), S[u].astype(BF16)) + _tn(AT[u], vu[u]) for u in U]
    kend = [(ku[u] * jnp.exp(Gu[u][C - 1:C, :] - Gu[u])).astype(BF16) for u in U]
    gend = [jnp.exp(Gu[u].T[:, C - 1:C]) for u in U]
    for u, (b, h) in enumerate(units):
        so_ref[b, h] = gend[u] * S[u] + _tn(kend[u], vu[u])
    o = [o[u] * lax.rsqrt(jnp.mean(o[u] * o[u], axis=-1, keepdims=True) + EPS) * nrm for u in U]
    o = [o[u] * _silu(tile(z, b, h, GLA_DV)) for u, (b, h) in enumerate(units)]
    rows = [jnp.concatenate(o[b * GLA_HEADS:(b + 1) * GLA_HEADS], axis=1) for b in range(bb)]
    o_ref[...] = jnp.concatenate(rows, axis=0).astype(o_ref.dtype).reshape(bb, C, GLA_W)


def _gla_prompt_kernel(q_ref, k_ref, v_ref, z_ref, us_ref, st_ref, wa_ref, ba_ref, nrm_ref, o_ref, so_ref, g_ref,
                       *, tb, C):
    t = pl.program_id(1)
    H = range(GLA_HEADS)
    nch = tb // C

    @pl.when(t == 0)
    def _():
        so_ref[...] = st_ref[...]

    glog = jax.nn.log_sigmoid(_mm(us_ref[...].astype(BF16), wa_ref[...]) + ba_ref[...]) * (1.0 / GLA_GATE_NORM)
    step_in_chunk = _iota2((tb, 1), 0) & (C - 1)
    d = 1
    while d < C:
        glog = glog + jnp.where(step_in_chunk >= d, pltpu.roll(glog, d, axis=0), 0.0)
        d *= 2
    g_ref[...] = glog * LOG2E

    ii = _iota2((C, C), 0)
    jj = _iota2((C, C), 1)
    nrm = nrm_ref[...]
    hs = lambda x, h, w: x[:, h * w:(h + 1) * w]

    def scores(c):
        rows = pl.ds(pl.multiple_of(c * C, C), C)
        G = g_ref[rows, :]
        q = q_ref[rows, :] * (GLA_DK ** -0.5)
        k = k_ref[rows, :]
        nt = C // SUBLANES
        Gt = [[hs(G, h, GLA_DK)[t * SUBLANES:(t + 1) * SUBLANES] for t in range(nt)] for h in H]
        kt = [[hs(k, h, GLA_DK)[t * SUBLANES:(t + 1) * SUBLANES] for t in range(nt)] for h in H]
        acc = [[jnp.zeros((SUBLANES, C), F32) for _ in range(nt)] for _ in H]
        for i in range(C):
            for t in range(i // SUBLANES + 1):
                for h in H:
                    e = jnp.exp2(jnp.minimum(hs(G, h, GLA_DK)[i:i + 1, :] - Gt[h][t], 0.0))
                    col = jnp.sum(kt[h][t] * e * hs(q, h, GLA_DK)[i:i + 1, :], axis=1, keepdims=True)
                    acc[h][t] = acc[h][t] + jnp.where(jj[:SUBLANES] == i, col, 0.0)
        AT = [jnp.where(ii <= jj, jnp.concatenate(acc[h], axis=0), 0.0).astype(BF16) for h in H]
        gend = [jnp.exp2(hs(G, h, GLA_DK).T[:, C - 1:C]) for h in H]
        return AT, gend

    def finish(o, c):
        rows = pl.ds(pl.multiple_of(c * C, C), C)
        z = z_ref[rows, :]
        o = [o[h] * lax.rsqrt(jnp.mean(o[h] * o[h], axis=-1, keepdims=True) + EPS) * nrm for h in H]
        o = [o[h] * _silu(hs(z, h, GLA_DV)) for h in H]
        o_ref[rows, :] = jnp.concatenate(o, axis=1).astype(o_ref.dtype)

    def body(c, carry):
        AT, gend, o_prev = carry
        nxt = scores(jnp.minimum(c + 1, nch - 1))
        rows = pl.ds(pl.multiple_of(c * C, C), C)
        G = g_ref[rows, :]
        q = q_ref[rows, :] * (GLA_DK ** -0.5)
        k = k_ref[rows, :]
        v = v_ref[rows, :]
        S = [so_ref[h] for h in H]
        vh = [hs(v, h, GLA_DV).astype(BF16) for h in H]
        o = [_mm((hs(q, h, GLA_DK) * jnp.exp2(hs(G, h, GLA_DK))).astype(BF16), S[h].astype(BF16)) + _tn(AT[h], vh[h])
             for h in H]
        kend = [(hs(k, h, GLA_DK) * jnp.exp2(hs(G, h, GLA_DK)[C - 1:C, :] - hs(G, h, GLA_DK))).astype(BF16) for h in H]
        for h in H:
            so_ref[h] = gend[h] * S[h] + _tn(kend[h], vh[h])
        finish(o_prev, jnp.maximum(c - 1, 0))
        return nxt + (o,)

    AT0, gend0 = scores(0)
    _, _, o_last = lax.fori_loop(0, nch, body, (AT0, gend0, [jnp.zeros((C, GLA_DV), F32) for _ in H]))
    finish(o_last, nch - 1)


def _gla_mixer(um, us, st_all, j, wa, ba, nrm, *, bb, tb, C, lvalid):
    B, L, _ = um.shape
    const = lambda *_: (0, 0)
    carry = None
    if lvalid == L:
        assert bb == 1
        kern = functools.partial(_gla_prompt_kernel, tb=tb, C=C)
        sq = None
        scratch = [pltpu.VMEM((tb, GLA_KW), F32)]
        st_spec = pl.BlockSpec((None, None, GLA_HEADS, GLA_DK, GLA_DV), lambda b, t: (j, b, 0, 0, 0))
    else:
        assert L == SUBLANES and tb == L
        carry = (st_all.shape[0], j) if j == 0 else None
        kern = functools.partial(_gla_sample_kernel, bb=bb, lvalid=lvalid, carry=carry)
        sq = bb
        scratch = []
        st_spec = _state_specs(st_all, j, bb, carry)
    return pl.pallas_call(
        kern,
        grid=(B // bb, L // tb),
        in_specs=[pl.BlockSpec((sq, tb, GLA_KW), lambda b, t: (b, t, AB_GQ // GLA_KW)),
                  pl.BlockSpec((sq, tb, GLA_KW), lambda b, t: (b, t, AB_GK // GLA_KW)),
                  pl.BlockSpec((sq, tb, GLA_W), lambda b, t: (b, t, AB_GV // GLA_W)),
                  pl.BlockSpec((sq, tb, GLA_W), lambda b, t: (b, t, AB_ZGLA // GLA_W)),
                  pl.BlockSpec((sq, tb, LANES), lambda b, t: (b, t, 0)),
                  st_spec,
                  pl.BlockSpec((LANES, GLA_KW), const),
                  pl.BlockSpec((1, GLA_KW), const),
                  pl.BlockSpec((1, GLA_DV), const)],
        out_specs=[pl.BlockSpec((sq, tb, GLA_W), lambda b, t: (b, t, 0)), st_spec],
        out_shape=[jax.ShapeDtypeStruct((B, L, GLA_W), BF16), jax.ShapeDtypeStruct(st_all.shape, F32)],
        scratch_shapes=scratch,
        input_output_aliases={} if carry else {5: 1},
        compiler_params=_params(dimension_semantics=("parallel", "arbitrary")),
        name="gla_mixer",
    )(um, um, um, um, us, st_all, wa, ba, nrm)


SSD_GW = SSD_W // SSD_GROUPS
SSD_REP = SSD_HEADS // SSD_GROUPS
SSD_ROWS = 64


def _lane_expand(x, h0, width):
    return jnp.concatenate([jnp.broadcast_to(x[:, h0 + j:h0 + j + 1], (x.shape[0], width)) for j in range(SSD_REP)],
                           axis=1)


def _ssd_masks(C):
    ii = _iota2((SSD_ROWS, SSD_GW), 0)
    jj = _iota2((SSD_ROWS, SSD_GW), 1) & (SSD_ROWS - 1)
    sh = C.bit_length() - 1
    negc = jnp.where(((ii >> sh) == (jj >> sh)) & (jj <= ii), 0.0, -1e30)
    ri = _iota2((SSD_GW, SSD_GW), 0)
    ci = _iota2((SSD_GW, SSD_GW), 1)
    hsh = SSD_HEADDIM.bit_length() - 1
    bd = ((ri >> hsh) == (ci >> hsh)).astype(F32)
    return negc, bd


def _ssd_intra(act, G, GT, dt, gi, negc, bd):
    h0 = gi * SSD_REP
    xs = act[:, gi * SSD_GW:(gi + 1) * SSD_GW]
    bm = act[:, SSD_W + gi * SSD_STATE:SSD_W + (gi + 1) * SSD_STATE].astype(BF16)
    cm = act[:, SSD_W + (SSD_GROUPS + gi) * SSD_STATE:SSD_W + (SSD_GROUPS + gi + 1) * SSD_STATE].astype(BF16)
    gcc = _lane_expand(G, h0, SSD_HEADDIM)
    grc = jnp.concatenate([GT[h0 + j:h0 + j + 1, :] for j in range(SSD_REP)], axis=1)
    vcat = xs * _lane_expand(dt, h0, SSD_HEADDIM)
    cbm = _nt(cm, bm)
    acat = (jnp.concatenate([cbm] * SSD_REP, axis=1) * jnp.exp((gcc - grc) + negc)).astype(BF16)
    vbd = (jnp.concatenate([vcat] * SSD_REP, axis=0) * bd).astype(BF16)
    return _mm(acat, vbd), xs, vcat, gcc, bm, cm


def _ssd_finish(o, xs, z, dsk, nrm):
    y = (o + dsk * xs) * _silu(z)
    return y * lax.rsqrt(jnp.mean(y * y, axis=-1, keepdims=True) + EPS) * nrm


def _ssd_prompt_kernel(z_ref, xbc_ref, us_ref, cst_ref, st_ref, cw_ref, cb_ref, prm_ref, nrm_ref, o_ref, so_ref,
                       halo_ref, scat_ref, *, tb):
    t = pl.program_id(1)
    C = SSD_ROWS
    cat_heads = lambda f, gi: jnp.concatenate([f(gi * SSD_REP + j) for j in range(SSD_REP)], axis=1)

    @pl.when(t == 0)
    def _():
        for gi in range(SSD_GROUPS):
            scat_ref[gi] = cat_heads(lambda h: st_ref[h], gi)
        halo_ref[0:5, :] = jnp.zeros((5, SSD_XBC), F32)
        halo_ref[5:8, :] = cst_ref[...]

    negc, bd = _ssd_masks(C)
    tril = (_iota2((C, C), 1) <= _iota2((C, C), 0)).astype(F32)
    lane = _iota2((1, LANES), 1)
    hmask = lane < SSD_HEADS
    neg_a = -jnp.exp(prm_ref[0:1, :])
    bias = prm_ref[1:2, :]
    dsk = [_lane_expand(prm_ref[2:3, :], gi * SSD_REP, SSD_HEADDIM) for gi in range(SSD_GROUPS)]
    cw = cw_ref[...]
    cb = cb_ref[...]
    nrm = nrm_ref[...]
    groups = range(SSD_GROUPS)

    def chunk(c, carry):
        r0 = pl.multiple_of(c * C, C)
        x = xbc_ref[pl.ds(r0, C), :]
        prev = xbc_ref[pl.ds(pl.multiple_of(jnp.maximum(r0 - SUBLANES, 0), SUBLANES), SUBLANES), :]
        halo = jnp.where(c == 0, halo_ref[...], prev)
        act = _silu(_conv_rows(halo, x, cw, cb))
        us = us_ref[pl.ds(r0, C), :]
        dt = jnp.where(hmask, jax.nn.softplus(us + bias), 0.0)
        G = _mm_hi(tril, neg_a * dt)
        GT = G.T
        z = z_ref[pl.ds(r0, C), :]
        parts = [_ssd_intra(act, G, GT, dt, gi, negc, bd) for gi in groups]
        S = [scat_ref[gi] for gi in groups]
        inter = [_mm(parts[gi][5], S[gi].astype(BF16)) for gi in groups]
        for gi in groups:
            o_in, xs, vcat, gcc, bm, _ = parts[gi]
            glc = gcc[C - 1:C, :]
            w = (vcat * jnp.exp(glc - gcc)).astype(BF16)
            scat_ref[gi] = jnp.exp(glc) * S[gi] + _tn(bm, w)
            y = _ssd_finish(jnp.exp(gcc) * inter[gi] + o_in, xs, z[:, gi * SSD_GW:(gi + 1) * SSD_GW], dsk[gi],
                            nrm[:, gi * SSD_GW:(gi + 1) * SSD_GW])
            o_ref[pl.ds(r0, C), gi * SSD_GW:(gi + 1) * SSD_GW] = y.astype(o_ref.dtype)
        return carry

    lax.fori_loop(0, tb // C, chunk, 0)
    halo_ref[...] = xbc_ref[tb - SUBLANES:tb, :]

    @pl.when(t == pl.num_programs(1) - 1)
    def _():
        for gi in groups:
            s = scat_ref[gi]
            for j in range(SSD_REP):
                so_ref[gi * SSD_REP + j] = s[:, j * SSD_HEADDIM:(j + 1) * SSD_HEADDIM]


def _ssd_sample_kernel(xs_ref, bm_ref, cm_ref, z_ref, dt_ref, cxs_ref, cbm_ref, ccm_ref, st_ref, wxs_ref, wbm_ref,
                       wcm_ref, prm_ref, nrm_ref, o_ref, so_ref, acc_ref, *, ls, carry):
    st_ref, so_ref = _own_layer(st_ref, so_ref, carry)
    h = pl.program_id(0)
    hj = h % SSD_REP

    def conv(x_ref, c_ref, w_ref):
        xx = [c_ref[r] for r in range(CONV_W - 1)] + [x_ref[t] for t in range(ls)]
        return [_silu(sum(xx[t + s] * w_ref[s] for s in range(CONV_W)) + w_ref[CONV_W]) for t in range(ls)]

    xs = conv(xs_ref, cxs_ref, wxs_ref)
    bm = conv(bm_ref, cbm_ref, wbm_ref)
    cm = conv(cm_ref, ccm_ref, wcm_ref)
    row = lambda ref, k: ref[k, pl.ds(h, 1), :]
    dt = [jax.nn.softplus(dt_ref[t, pl.ds(h, 1), :] + row(prm_ref, 1)) for t in range(ls)]
    neg_a = -jnp.exp(row(prm_ref, 0))
    G = []
    for t in range(ls):
        G.append(neg_a * dt[t] + (G[-1] if G else 0.0))
    v = [xs[t] * dt[t] for t in range(ls)]
    y = []
    for t in range(ls):
        acc = None
        for u in range(t + 1):
            coef = jnp.sum(cm[t] * bm[u], axis=0, keepdims=True) * jnp.exp(G[t] - G[u])
            acc = coef * v[u] if acc is None else acc + coef * v[u]
        y.append(acc)
    eg = [jnp.exp(G[t]) for t in range(ls)]
    wk = [bm[u] * jnp.exp(G[ls - 1] - G[u]) for u in range(ls)]
    inter = [None] * ls
    for n in range(SSD_STATE):
        S = st_ref[0, n]
        for t in range(ls):
            term = cm[t][n:n + 1, :] * S
            inter[t] = term if inter[t] is None else inter[t] + term
        s_new = eg[ls - 1] * S
        for u in range(ls):
            s_new = s_new + wk[u][n:n + 1, :] * v[u]
        so_ref[0, n] = s_new
    dsk = row(prm_ref, 2)
    for t in range(ls):
        acc_ref[t, pl.ds(pl.multiple_of(hj * SSD_HEADDIM, SSD_HEADDIM), SSD_HEADDIM), :] = (
            (y[t] + eg[t] * inter[t] + dsk * xs[t]) * _silu(z_ref[t]))

    @pl.when(hj == SSD_REP - 1)
    def _():
        for t in range(ls):
            yg = acc_ref[t]
            r = lax.rsqrt(jnp.mean(yg * yg, axis=0, keepdims=True) + EPS)
            o_ref[t] = (yg * r * nrm_ref[...]).astype(o_ref.dtype)


def _ssd_mixer(um, us, cst_all, st_all, j, cw, cb, prm, nrm, *, tb, sample):
    B, L, _ = um.shape
    if sample:
        return _ssd_mixer_sample(um, us, cst_all, st_all, j, cw, cb, prm, nrm)
    const = lambda *_: (0, 0)
    out_shape = [jax.ShapeDtypeStruct((B, L, SSD_W), BF16), jax.ShapeDtypeStruct(st_all.shape, F32)]
    st_blk = (SSD_HEADS, SSD_STATE, SSD_HEADDIM)
    assert tb % SSD_ROWS == 0
    return pl.pallas_call(
        functools.partial(_ssd_prompt_kernel, tb=tb),
        grid=(B, L // tb),
        in_specs=[pl.BlockSpec((None, tb, SSD_W), lambda b, t: (b, t, CD_Z // SSD_W)),
                  pl.BlockSpec((None, tb, SSD_XBC), lambda b, t: (b, t, CD_XBC // SSD_XBC)),
                  pl.BlockSpec((None, tb, LANES), lambda b, t: (b, t, 0)),
                  pl.BlockSpec((None, None, CONV_W - 1, SSD_XBC), lambda b, t: (j, b, 0, 0)),
                  pl.BlockSpec((None, None) + st_blk, lambda b, t: (j, b, 0, 0, 0)),
                  pl.BlockSpec((CONV_W, SSD_XBC), const), pl.BlockSpec((1, SSD_XBC), const),
                  pl.BlockSpec((3, LANES), const), pl.BlockSpec((1, SSD_W), const)],
        out_specs=[pl.BlockSpec((None, tb, SSD_W), lambda b, t: (b, t, 0)),
                   pl.BlockSpec((None, None) + st_blk, lambda b, t: (j, b, 0, 0, 0))],
        out_shape=out_shape,
        scratch_shapes=[pltpu.VMEM((SUBLANES, SSD_XBC), F32), pltpu.VMEM((SSD_GROUPS, SSD_STATE, SSD_GW), F32)],
        input_output_aliases={4: 1},
        compiler_params=_params(dimension_semantics=("parallel", "arbitrary")), name="ssd_mixer",
    )(um, um, us, cst_all, st_all, cw, cb, prm, nrm)


def _ssd_mixer_sample(um, us, cst_all, st_all, j, cw, cb, prm, nrm):
    B, ls, _ = um.shape
    n_layers = st_all.shape[0]
    chan_major = lambda x: jnp.transpose(x, (1, 2, 0))
    xbc_t = chan_major(um[:, :, CD_XBC:CD_XBC + SSD_XBC])
    z_t = chan_major(um[:, :, CD_Z:CD_Z + SSD_W])
    dt_t = chan_major(us[:, :, :SSD_HEADS])
    cst_t = chan_major(cst_all[j])
    st_t = jnp.transpose(st_all, (0, 2, 3, 4, 1))
    lanes = lambda x: jnp.broadcast_to(x[..., None], x.shape + (B,))
    w_t = lanes(jnp.concatenate([cw, cb], axis=0))
    prm_t = lanes(prm[:, :SSD_HEADS])
    nrm_t = lanes(nrm.reshape(SSD_W))
    carry = (n_layers, j) if j == 0 else None
    hd = SSD_HEADDIM
    n_xs = SSD_W // hd
    grp = lambda h: h // SSD_REP
    win = lambda rows, f: pl.BlockSpec((rows, hd, B), lambda h: (0, f(h), 0))
    if carry:
        st_spec = pl.BlockSpec((n_layers, 1, SSD_STATE, hd, B), lambda h: (0, h, 0, 0, 0))
    else:
        st_spec = pl.BlockSpec((None, 1, SSD_STATE, hd, B), lambda h: (j, h, 0, 0, 0))
    y_t, st_new = pl.pallas_call(
        functools.partial(_ssd_sample_kernel, ls=ls, carry=carry),
        grid=(SSD_HEADS,),
        in_specs=[win(ls, lambda h: h), win(ls, lambda h: n_xs + grp(h)), win(ls, lambda h: n_xs + SSD_GROUPS + grp(h)),
                  win(ls, lambda h: h),
                  pl.BlockSpec((ls, SSD_HEADS, B), lambda h: (0, 0, 0)),
                  win(CONV_W - 1, lambda h: h), win(CONV_W - 1, lambda h: n_xs + grp(h)),
                  win(CONV_W - 1, lambda h: n_xs + SSD_GROUPS + grp(h)),
                  st_spec,
                  win(CONV_W + 1, lambda h: h), win(CONV_W + 1, lambda h: n_xs + grp(h)),
                  win(CONV_W + 1, lambda h: n_xs + SSD_GROUPS + grp(h)),
                  pl.BlockSpec((3, SSD_HEADS, B), lambda h: (0, 0, 0)),
                  pl.BlockSpec((SSD_GW, B), lambda h: (grp(h), 0))],
        out_specs=[pl.BlockSpec((ls, SSD_GW, B), lambda h: (0, grp(h), 0)), st_spec],
        out_shape=[jax.ShapeDtypeStruct((ls, SSD_W, B), F32), jax.ShapeDtypeStruct(st_t.shape, F32)],
        scratch_shapes=[pltpu.VMEM((ls, SSD_GW, B), F32)],
        input_output_aliases={} if carry else {8: 1},
        compiler_params=_params(dimension_semantics=("arbitrary",)), name="ssd_mixer",
    )(xbc_t, xbc_t, xbc_t, z_t, dt_t, cst_t, cst_t, cst_t, st_t, w_t, w_t, w_t, prm_t, nrm_t)
    y = jnp.transpose(y_t, (2, 0, 1)).astype(BF16)
    return y, jnp.transpose(st_new, (0, 4, 1, 2, 3))


def _lru_kernel(z_ref, x_ref, cst_ref, st_ref, cw_ref, cb_ref, wbd_ref, prm_ref, o_ref, so_ref, halo_ref, carry_ref,
                *, bb, tb, lvalid, ltotal):
    t = pl.program_id(1)

    @pl.when(t == 0)
    def _():
        carry_ref[...] = st_ref[...]
        halo_ref[:, 0:5, :] = jnp.zeros((bb, 5, LRU_W), F32)
        halo_ref[:, 5:8, :] = cst_ref[...]

    cw = cw_ref[...]
    cb = cb_ref[...]
    b_a = prm_ref[0:1, :]
    b_x = prm_ref[1:2, :]
    sp = jax.nn.softplus(-prm_ref[2:3, :])
    row = _iota2((tb, 1), 0)
    last_t, last_r = (lvalid - 1) // tb, (lvalid - 1) % tb

    def per_batch(bi, carry):
        x = x_ref[bi]
        xc = _conv_rows(halo_ref[bi], x, cw, cb)
        halo_ref[bi] = x[tb - SUBLANES:tb, :]
        xb = xc.astype(BF16)
        rs, is_ = [], []
        for cg in range(LRU_W // LANES):
            ri = _mm(xb[:, cg * LANES:(cg + 1) * LANES], wbd_ref[cg])
            rs.append(ri[:, :LANES])
            is_.append(ri[:, LANES:])
        r = jax.nn.sigmoid(jnp.concatenate(rs, axis=1) + b_a)
        i = jax.nn.sigmoid(jnp.concatenate(is_, axis=1) + b_x)
        log_a = (-LRU_C) * r * sp
        a = jnp.exp(log_a)
        om = 1.0 - jnp.exp(2.0 * log_a)
        b = jnp.where(om > 0.0, om * lax.rsqrt(om), 0.0) * (i * xc)
        d = 1
        while d < min(tb, SUBLANES):
            m = (row & (SUBLANES - 1)) >= d
            a_s = pltpu.roll(a, d, axis=0)
            b_s = pltpu.roll(b, d, axis=0)
            b = jnp.where(m, a * b_s + b, b)
            a = jnp.where(m, a * a_s, a)
            d *= 2
        h_prev = carry_ref[bi]
        tiles = []
        for i in range(tb // SUBLANES):
            rows = slice(i * SUBLANES, (i + 1) * SUBLANES)
            tiles.append(a[rows] * h_prev + b[rows])
            h_prev = tiles[-1][SUBLANES - 1:SUBLANES, :]
        hs = jnp.concatenate(tiles, axis=0)
        carry_ref[bi] = h_prev

        @pl.when(t == last_t)
        def _():
            so_ref[bi] = hs[last_r:last_r + 1, :]

        o_ref[bi] = (hs * _silu(z_ref[bi])).astype(o_ref.dtype)
        return carry

    lax.fori_loop(0, bb, per_batch, 0)


def _lru_sample_kernel(z_ref, x_ref, cst_ref, st_ref, cw_ref, cb_ref, wbd_ref, prm_ref, o_ref, so_ref, *, ls):
    cw = cw_ref[...]
    cb = cb_ref[...]
    b_a = prm_ref[0:1, :]
    b_x = prm_ref[1:2, :]
    sp = jax.nn.softplus(-prm_ref[2:3, :])
    xx = [cst_ref[r] for r in range(CONV_W - 1)] + [x_ref[t] for t in range(ls)]
    h = st_ref[...]
    for t in range(ls):
        xc = sum(xx[t + s] * cw[s:s + 1] for s in range(CONV_W)) + cb
        xb = xc.astype(BF16)
        ri = [_mm(xb[:, cg * LANES:(cg + 1) * LANES], wbd_ref[cg]) for cg in range(LRU_W // LANES)]
        r = jax.nn.sigmoid(jnp.concatenate([x[:, :LANES] for x in ri], axis=1) + b_a)
        i = jax.nn.sigmoid(jnp.concatenate([x[:, LANES:] for x in ri], axis=1) + b_x)
        log_a = (-LRU_C) * r * sp
        om = 1.0 - jnp.exp(2.0 * log_a)
        h = jnp.exp(log_a) * h + jnp.where(om > 0.0, om * lax.rsqrt(om), 0.0) * (i * xc)
        o_ref[t] = (h * _silu(z_ref[t])).astype(o_ref.dtype)
    so_ref[...] = h


def _lru_mixer_sample(um, cst_all, st_all, j, cw, cb, wbd, prm):
    B, ls, _ = um.shape
    time_major = lambda x: jnp.transpose(x, (1, 0, 2))
    full = lambda *shape: pl.BlockSpec(shape, lambda i: (0,) * len(shape))
    y_t, st_new = pl.pallas_call(
        functools.partial(_lru_sample_kernel, ls=ls),
        grid=(1,),
        in_specs=[full(ls, B, LRU_W), full(ls, B, LRU_W), full(CONV_W - 1, B, LRU_W), full(B, LRU_W),
                  full(CONV_W, LRU_W), full(1, LRU_W), full(LRU_W // LANES, LANES, 2 * LANES), full(3, LRU_W)],
        out_specs=[full(ls, B, LRU_W), full(B, LRU_W)],
        out_shape=[jax.ShapeDtypeStruct((ls, B, LRU_W), BF16), jax.ShapeDtypeStruct((B, LRU_W), F32)],
        compiler_params=_params(dimension_semantics=("arbitrary",)),
        name="lru_mixer",
    )(time_major(um[:, :, CD_ZLRU:CD_ZLRU + LRU_W]), time_major(um[:, :, CD_XLRU:CD_XLRU + LRU_W]),
      time_major(cst_all[j]), st_all[j].reshape(B, LRU_W), cw, cb, wbd, prm)
    return time_major(y_t), st_new.reshape(B, 1, LRU_W)


def _lru_mixer(um, cst_all, st_all, j, cw, cb, wbd, prm, *, tb, sample):
    B, L, _ = um.shape
    if sample:
        return _lru_mixer_sample(um, cst_all, st_all, j, cw, cb, wbd, prm)
    bb = 1
    kern = functools.partial(_lru_kernel, bb=bb, tb=tb, lvalid=L, ltotal=L)
    const = lambda b, t: (0, 0)
    return pl.pallas_call(
        kern,
        grid=(B // bb, L // tb),
        in_specs=[pl.BlockSpec((bb, tb, LRU_W), lambda b, t: (b, t, CD_ZLRU // LRU_W)),
                  pl.BlockSpec((bb, tb, LRU_W), lambda b, t: (b, t, CD_XLRU // LRU_W)),
                  pl.BlockSpec((None, bb, CONV_W - 1, LRU_W), lambda b, t: (j, b, 0, 0)),
                  pl.BlockSpec((None, bb, 1, LRU_W), lambda b, t: (j, b, 0, 0)),
                  pl.BlockSpec((CONV_W, LRU_W), const),
                  pl.BlockSpec((1, LRU_W), const),
                  pl.BlockSpec((LRU_W // LANES, LANES, 2 * LANES), lambda b, t: (0, 0, 0)),
                  pl.BlockSpec((3, LRU_W), const)],
        out_specs=[pl.BlockSpec((bb, tb, LRU_W), lambda b, t: (b, t, 0)),
                   pl.BlockSpec((bb, 1, LRU_W), lambda b, t: (b, 0, 0))],
        out_shape=[jax.ShapeDtypeStruct((B, L, LRU_W), BF16),
                   jax.ShapeDtypeStruct((B, 1, LRU_W), F32)],
        scratch_shapes=[pltpu.VMEM((bb, SUBLANES, LRU_W), F32), pltpu.VMEM((bb, 1, LRU_W), F32)],
        compiler_params=_params(dimension_semantics=("parallel", "arbitrary")),
        name="lru_mixer",
    )(um, um, cst_all, st_all, cw, cb, wbd, prm)


def _lane_row(*pieces):
    v = jnp.concatenate([p.reshape(-1) for p in pieces])
    return jnp.pad(v, (0, LANES - v.shape[0])).reshape(1, LANES)


RP_TILE = 512
RP_TAIL = 32


def _repack_kernel(a_ref, b_ref, o_ref, *, plan):
    r = pl.program_id(1)
    lo = 0
    for n_tiles, _, shift in plan:
        @pl.when((r >= lo) & (r < lo + n_tiles))
        def _(shift=shift):
            if shift == 0:
                o_ref[...] = a_ref[...].astype(BF16)
            else:
                o_ref[0:RP_TILE - shift, :] = a_ref[shift:, :].astype(BF16)
                o_ref[RP_TILE - shift:, :] = b_ref[0:shift, :].astype(BF16)
        lo += n_tiles


def _repack_wt(wt, plan):
    n, _, D = wt.shape
    total = sum(p[0] for p in plan)

    def src_tile(r):
        lo, idx = 0, 0
        for n_tiles, start, _ in plan:
            idx = jnp.where((r >= lo) & (r < lo + n_tiles), start // RP_TILE + (r - lo), idx)
            lo += n_tiles
        return idx

    for _, start, shift in plan:
        assert start % RP_TILE == 0 and shift % 16 == 0 and shift <= RP_TAIL
    return pl.pallas_call(
        functools.partial(_repack_kernel, plan=plan),
        grid=(n, total),
        in_specs=[pl.BlockSpec((None, RP_TILE, D), lambda l, r: (l, src_tile(r), 0)),
                  pl.BlockSpec((None, RP_TAIL, D), lambda l, r: (l, (src_tile(r) + 1) * (RP_TILE // RP_TAIL), 0))],
        out_specs=pl.BlockSpec((None, RP_TILE, D), lambda l, r: (l, r, 0)),
        out_shape=jax.ShapeDtypeStruct((n, total * RP_TILE, D), BF16),
        compiler_params=_params(dimension_semantics=("parallel", "parallel")),
        name="repack_w",
    )(wt, wt)


def _rows_t(w, sizes, order, pad_to=None):
    wt = jnp.swapaxes(w, 1, 2)
    offs = [sum(sizes[:i]) for i in range(len(sizes))]
    out = jnp.concatenate([wt[:, offs[i]:offs[i] + sizes[i], :] for i in order], axis=1)
    if pad_to is not None:
        out = jnp.pad(out, ((0, 0), (0, pad_to - out.shape[1]), (0, 0)))
    return out


def _prep_weights(w):
    n_even = w['ab_w_in'].shape[0]
    n_odd = w['cd_w_in'].shape[0]
    D = w['ab_w_in'].shape[1]
    P = {'ab': [], 'cd': []}
    P['ab_w_main'] = _repack_wt(jnp.swapaxes(w['ab_w_in'], 1, 2), [(6, 0, 0), (6, 3072, 16), (2, 6144, 32)])
    P['ab_w_small'] = _rows_t(w['ab_w_in'], AB_SIZES, (1, 2, 7), LANES)
    P['ab_w_out'] = w['ab_w_out'].astype(BF16)
    for j in range(n_even):
        P['ab'].append(dict(
            dn_cw=w['dn_conv_w'][j],
            dn_prm=jnp.concatenate([_lane_row(jnp.zeros((DN_HEADS,), F32), w['dn_a_log'][j]),
                                    _lane_row(jnp.zeros((DN_HEADS,), F32), w['dn_dt_bias'][j])], axis=0),
            dn_norm=w['dn_norm'][j].reshape(1, DN_DV),
            gla_wa=jnp.pad(w['gla_wa2'][j], ((2 * DN_HEADS, LANES - 2 * DN_HEADS - GLA_RANK), (0, 0))).astype(BF16),
            gla_ba=w['gla_ba'][j].reshape(1, GLA_KW),
            gla_norm=w['gla_norm'][j].reshape(1, GLA_DV),
        ))
    P['cd_w_main'] = _repack_wt(jnp.swapaxes(w['cd_w_in'], 1, 2), [(2, 0, 0), (4, 2560, 16), (3, 1024, 0)])
    P['cd_w_small'] = _rows_t(w['cd_w_in'], CD_SIZES, (2,), LANES)
    P['cd_w_out'] = w['cd_w_out'].astype(BF16)
    eye2 = jnp.eye(2, dtype=F32)
    for j in range(n_odd):
        wa = w['lru_wa'][j].reshape(LRU_W // LANES, 2, LRU_BS, LRU_BS)
        wx = w['lru_wx'][j].reshape(LRU_W // LANES, 2, LRU_BS, LRU_BS)
        bd = lambda m: jnp.einsum('gbcd,be->gbced', m, eye2).reshape(LRU_W // LANES, LANES, LANES)
        P['cd'].append(dict(
            ssd_cw=w['ssd_conv_w'][j],
            ssd_cb=w['ssd_conv_b'][j].reshape(1, SSD_XBC),
            ssd_prm=jnp.concatenate([_lane_row(w['ssd_a_log'][j]), _lane_row(w['ssd_dt_bias'][j]),
                                     _lane_row(w['ssd_d'][j])], axis=0),
            ssd_norm=w['ssd_norm'][j].reshape(1, SSD_W),
            lru_cw=w['lru_conv_w'][j],
            lru_cb=w['lru_conv_b'][j].reshape(1, LRU_W),
            lru_wbd=jnp.concatenate([bd(wa), bd(wx)], axis=2).astype(BF16),
            lru_prm=jnp.stack([w['lru_ba'][j], w['lru_bx'][j], w['lru_lambda'][j]], axis=0),
        ))
    P['norm_g'] = w['norm_g'].reshape(-1, 1, D)
    P['ple_norm'] = w['ple_norm'].reshape(-1, 1, D)
    P['ple_w_gate'] = w['ple_w_gate'].astype(BF16)
    P['ple_w_proj'] = w['ple_w_proj'].astype(BF16)
    P['final_norm'] = w['final_norm'].reshape(1, D)
    return P


def _trunk(x, p, states, P, *, lpad, bb, tb, c_dn, c_gla):
    B, L, D = x.shape
    lvalid = L
    sample = lpad > 0
    pad_t = (lambda a: jnp.pad(a, ((0, 0), (0, lpad - L), (0, 0)))) if sample else (lambda a: a)
    depth = P['norm_g'].shape[0]
    st_dn, st_dnc, st_gla, st_ssd, st_ssdc, st_lru, st_lruc = states
    st_lru = st_lru.reshape(st_lru.shape[0], B, 1, LRU_W)
    dnc_new, ssdc_new, lru_new, lruc_new = [], [], [], []
    T = B * L
    h = x.reshape(T, D)
    p_all = p.reshape(depth, T, PLE_DIM)
    lo = lvalid - (CONV_W - 1)
    for li in range(depth):
        j = li // 2
        if li % 2 == 0:
            W = P['ab'][j]
            um, us = _inproj(h, P['norm_g'], li, P['ab_w_main'], P['ab_w_small'], j)
            um = um.reshape(B, L, AB_MAIN)
            us = us.reshape(B, L, LANES)
            um_p, us_p = pad_t(um), pad_t(us)
            oa, st_dn = _dn_mixer(um_p, us_p, st_dnc, st_dn, j, W['dn_cw'], W['dn_prm'], W['dn_norm'],
                                  bb=bb, tb=tb, C=c_dn, lvalid=lvalid)
            ob, st_gla = _gla_mixer(um_p, us_p, st_gla, j, W['gla_wa'], W['gla_ba'], W['gla_norm'],
                                    bb=bb, tb=tb, C=c_gla, lvalid=lvalid)
            oa, ob = oa[:, :L], ob[:, :L]
            dnc_new.append(um[:, lo:lvalid, AB_QKV:AB_QKV + DN_QKV])
        else:
            W = P['cd'][j]
            um, us = _inproj(h, P['norm_g'], li, P['cd_w_main'], P['cd_w_small'], j)
            um = um.reshape(B, L, CD_MAIN)
            us = us.reshape(B, L, LANES)
            oa, st_ssd = _ssd_mixer(um, us, st_ssdc, st_ssd, j, W['ssd_cw'], W['ssd_cb'], W['ssd_prm'],
                                    W['ssd_norm'], tb=tb, sample=sample)
            ob, lru_j = _lru_mixer(um, st_lruc, st_lru, j, W['lru_cw'], W['lru_cb'],
                                   W['lru_wbd'], W['lru_prm'], tb=tb, sample=sample)
            ssdc_new.append(um[:, lo:lvalid, CD_XBC:CD_XBC + SSD_XBC])
            lru_new.append(lru_j.reshape(B, LRU_W))
            lruc_new.append(um[:, lo:lvalid, CD_XLRU:CD_XLRU + LRU_W])
        h = _outproj(oa.reshape(T, -1), ob.reshape(T, -1), h, p_all, li, P['ab_w_out' if li % 2 == 0 else 'cd_w_out'], j,
                     P['ple_norm'], P['ple_w_gate'], P['ple_w_proj'], P['final_norm'], final=(li == depth - 1))
    new_states = [st_dn, jnp.stack(dnc_new), st_gla, st_ssd, jnp.stack(ssdc_new), jnp.stack(lru_new),
                  jnp.stack(lruc_new)]
    return h.reshape(B, L, D), new_states


def _pick_tb(L):
    tb = min(L, 512)
    assert L % tb == 0
    return tb


def kernel(x_prompt, x_sample, state_dn, state_dn_conv, state_gla, state_ssd, state_ssd_conv, state_lru, state_lru_conv, p_prompt, p_sample, norm_g, final_norm, ab_w_in, dn_conv_w, dn_a_log, dn_dt_bias, dn_norm, gla_wa2, gla_ba, gla_norm, ab_w_out, cd_w_in, ssd_conv_w, ssd_conv_b, ssd_a_log, ssd_dt_bias, ssd_d, ssd_norm, lru_conv_w, lru_conv_b, lru_wa, lru_ba, lru_wx, lru_bx, lru_lambda, cd_w_out, ple_w_proj, ple_norm, ple_w_gate):
    w = {
        'norm_g': norm_g, 'final_norm': final_norm,
        'ab_w_in': ab_w_in, 'dn_conv_w': dn_conv_w, 'dn_a_log': dn_a_log, 'dn_dt_bias': dn_dt_bias,
        'dn_norm': dn_norm, 'gla_wa2': gla_wa2, 'gla_ba': gla_ba, 'gla_norm': gla_norm, 'ab_w_out': ab_w_out,
        'cd_w_in': cd_w_in, 'ssd_conv_w': ssd_conv_w, 'ssd_conv_b': ssd_conv_b, 'ssd_a_log': ssd_a_log,
        'ssd_dt_bias': ssd_dt_bias, 'ssd_d': ssd_d, 'ssd_norm': ssd_norm,
        'lru_conv_w': lru_conv_w, 'lru_conv_b': lru_conv_b, 'lru_wa': lru_wa, 'lru_ba': lru_ba,
        'lru_wx': lru_wx, 'lru_bx': lru_bx, 'lru_lambda': lru_lambda, 'cd_w_out': cd_w_out,
        'ple_w_proj': ple_w_proj, 'ple_norm': ple_norm, 'ple_w_gate': ple_w_gate,
    }
    P = _prep_weights(w)
    n_even, n_odd = ab_w_in.shape[0], cd_w_in.shape[0]

    Bp, Lp, _ = x_prompt.shape
    assert Lp >= CONV_W - 1 and Lp % 64 == 0
    zero_states = (
        jnp.zeros((n_even, Bp, DN_HEADS, DN_DK, DN_DV), F32),
        jnp.zeros((n_even, Bp, CONV_W - 1, DN_QKV), F32),
        jnp.zeros((n_even, Bp, GLA_HEADS, GLA_DK, GLA_DV), F32),
        jnp.zeros((n_odd, Bp, SSD_HEADS, SSD_STATE, SSD_HEADDIM), F32),
        jnp.zeros((n_odd, Bp, CONV_W - 1, SSD_XBC), F32),
        jnp.zeros((n_odd, Bp, LRU_W), F32),
        jnp.zeros((n_odd, Bp, CONV_W - 1, LRU_W), F32),
    )
    y_p, st_p = _trunk(x_prompt, p_prompt, zero_states, P, lpad=0, bb=1, tb=_pick_tb(Lp), c_dn=64, c_gla=16)

    Bs, Ls, _ = x_sample.shape
    assert CONV_W - 1 <= Ls < SUBLANES
    sample_states = (state_dn, state_dn_conv, state_gla, state_ssd, state_ssd_conv, state_lru, state_lru_conv)
    bb_s = 8 if Bs % 8 == 0 else 1
    y_s, st_s = _trunk(x_sample, p_sample, sample_states, P, lpad=SUBLANES, bb=bb_s, tb=SUBLANES,
                       c_dn=SUBLANES, c_gla=SUBLANES)
    return (y_p, y_s, *st_p, *st_s)
```
